```python
import jax
import jax.numpy as jnp
from jax import lax
import numpy as np

D_MODEL = 1024
BATCH = 16
SEQ = 2048
DEPTH = 1

GRID_W = 64
CTX_LEN = 256
HEAD_DIM = 64
N_Q_HEADS = 8
N_KV_HEADS = 2
Q_PER_KV = N_Q_HEADS // N_KV_HEADS
ATTN_WIDTH = N_Q_HEADS * HEAD_DIM
KV_WIDTH = N_KV_HEADS * HEAD_DIM
WINDOW = 128
BLOCK = 128
ROPE_BASE = 10000.0
ROPE_PAIRS = HEAD_DIM // 4
CONV_WIDTH = D_MODEL // 2
CONV_K = 3
PEER_HEADS = 8
PEER_KEY_DIM = 256
PEER_N_KEYS = 128
PEER_N_EXPERTS = PEER_N_KEYS * PEER_N_KEYS
PEER_TOPK = 16
PEER_CHUNK = 128
N_MOD = 6
EPS = 1e-6
NEG_INF = -1e30

PROJ_SIZES = (ATTN_WIDTH, KV_WIDTH, KV_WIDTH, CONV_WIDTH, CONV_WIDTH, CONV_WIDTH, D_MODEL, D_MODEL)
PROJ_WIDTH = sum(PROJ_SIZES)
PROJ_SPLITS = tuple(int(s) for s in np.cumsum(PROJ_SIZES)[:-1])
KV_START = ATTN_WIDTH
KV_END = ATTN_WIDTH + 2 * KV_WIDTH

kernel_name = "hybrid_swa_shortconv_peer_dit_layer"


def rmsnorm(x, g):
    xf = x.astype(jnp.float32)
    y = xf * lax.rsqrt(jnp.mean(xf * xf, axis=-1, keepdims=True) + EPS)
    return (y * g.astype(jnp.float32)).astype(x.dtype)


def adaln_params(cond, w_mod, b_mod):
    m = jax.nn.silu(cond) @ w_mod + b_mod
    return jnp.split(m, N_MOD, axis=-1)


def axial_rope_tables(length):
    rows = length // GRID_W
    row = jnp.repeat(jnp.arange(rows, dtype=jnp.float32), GRID_W)
    col = jnp.tile(jnp.arange(GRID_W, dtype=jnp.float32), rows)
    inv_freq = ROPE_BASE ** (-jnp.arange(ROPE_PAIRS, dtype=jnp.float32) / ROPE_PAIRS)
    ang_r = row[:, None] * inv_freq
    ang_c = col[:, None] * inv_freq
    return (jnp.cos(ang_r), jnp.sin(ang_r), jnp.cos(ang_c), jnp.sin(ang_c))


def _rotate(xp, cos, sin):
    x1, x2 = jnp.split(xp, 2, axis=-1)
    c = cos[:, None, :]
    s = sin[:, None, :]
    return jnp.concatenate([x1 * c - x2 * s, x1 * s + x2 * c], axis=-1)


def apply_axial_rope(x, tables):
    cos_r, sin_r, cos_c, sin_c = tables
    xr, xc = jnp.split(x.astype(jnp.float32), 2, axis=-1)
    out = jnp.concatenate([_rotate(xr, cos_r, sin_r), _rotate(xc, cos_c, sin_c)], axis=-1)
    return out.astype(x.dtype)


def split_proj(z):
    return jnp.split(z, PROJ_SPLITS, axis=-1)


def windowed_attention(q, k, v, k_ctx, v_ctx, sink):
    B, L = q.shape[0], q.shape[1]
    nb = L // BLOCK
    scale = HEAD_DIM ** -0.5
    qb = q.reshape(B, nb, BLOCK, N_KV_HEADS, Q_PER_KV, HEAD_DIM)
    pad = ((0, 0), (BLOCK, BLOCK), (0, 0), (0, 0))
    kp = jnp.pad(k, pad).reshape(B, nb + 2, BLOCK, N_KV_HEADS, HEAD_DIM)
    vp = jnp.pad(v, pad).reshape(B, nb + 2, BLOCK, N_KV_HEADS, HEAD_DIM)
    kb = jnp.concatenate([kp[:, :-2], kp[:, 1:-1], kp[:, 2:]], axis=2)
    vb = jnp.concatenate([vp[:, :-2], vp[:, 1:-1], vp[:, 2:]], axis=2)
    s_loc = jnp.einsum('bnqhgd,bnkhd->bhgnqk', qb, kb, preferred_element_type=jnp.float32) * scale
    s_ctx = jnp.einsum('bnqhgd,bchd->bhgnqc', qb, k_ctx, preferred_element_type=jnp.float32) * scale
    blk = jnp.arange(nb)[:, None]
    qpos = blk * BLOCK + jnp.arange(BLOCK)[None, :]
    kpos = (blk - 1) * BLOCK + jnp.arange(3 * BLOCK)[None, :]
    valid = ((kpos[:, None, :] >= 0) & (kpos[:, None, :] < L)
             & (jnp.abs(qpos[:, :, None] - kpos[:, None, :]) <= WINDOW))
    s_loc = jnp.where(valid, s_loc, NEG_INF)
    sink_l = jnp.broadcast_to(
        sink.astype(jnp.float32).reshape(N_KV_HEADS, Q_PER_KV)[None, :, :, None, None, None],
        s_loc.shape[:-1] + (1,))
    p = jax.nn.softmax(jnp.concatenate([s_loc, s_ctx, sink_l], axis=-1), axis=-1)
    n_loc = 3 * BLOCK
    n_ctx = k_ctx.shape[1]
    p_loc = p[..., :n_loc].astype(v.dtype)
    p_ctx = p[..., n_loc:n_loc + n_ctx].astype(v.dtype)
    out = (jnp.einsum('bhgnqk,bnkhd->bnqhgd', p_loc, vb)
           + jnp.einsum('bhgnqc,bchd->bnqhgd', p_ctx, v_ctx))
    return out.reshape(B, L, ATTN_WIDTH)


def context_attention(q, k, v, sink):
    B, C = q.shape[0], q.shape[1]
    qg = q.reshape(B, C, N_KV_HEADS, Q_PER_KV, HEAD_DIM)
    s = jnp.einsum('bqhgd,bkhd->bhgqk', qg, k, preferred_element_type=jnp.float32) * HEAD_DIM ** -0.5
    sink_l = jnp.broadcast_to(
        sink.astype(jnp.float32).reshape(N_KV_HEADS, Q_PER_KV)[None, :, :, None, None],
        s.shape[:-1] + (1,))
    p = jax.nn.softmax(jnp.concatenate([s, sink_l], axis=-1), axis=-1)[..., :C].astype(v.dtype)
    out = jnp.einsum('bhgqk,bkhd->bqhgd', p, v)
    return out.reshape(B, C, ATTN_WIDTH)


def short_conv(z, w):
    L = z.shape[1]
    zp = jnp.pad(z, ((0, 0), (1, 1), (0, 0)))
    return zp[:, :L] * w[0] + zp[:, 1:L + 1] * w[1] + zp[:, 2:] * w[2]


def conv_branch(gate_b, gate_c, u, conv_w, w_conv_out):
    return (gate_b * short_conv(gate_c * u, conv_w)) @ w_conv_out


def merge_branches(y_attn, y_conv, ga, gc, w_mix_out):
    return (jax.nn.sigmoid(ga) * y_attn + jax.nn.sigmoid(gc) * y_conv) @ w_mix_out


def peer(h, w_q, sub_keys, u_tab, v_tab):
    B, L, D = h.shape
    tok = h.reshape(-1, PEER_CHUNK, D)

    def chunk(t):
        T = t.shape[0]
        qh = (t @ w_q).reshape(T, PEER_HEADS, 2, PEER_KEY_DIM // 2)
        s = jnp.einsum('thpd,hpnd->thpn', qh, sub_keys, preferred_element_type=jnp.float32)
        s_top, i_top = lax.top_k(s, PEER_TOPK)
        cand = s_top[:, :, 0, :, None] + s_top[:, :, 1, None, :]
        cand_idx = i_top[:, :, 0, :, None] * PEER_N_KEYS + i_top[:, :, 1, None, :]
        best, pos = lax.top_k(cand.reshape(T, PEER_HEADS, PEER_TOPK * PEER_TOPK), PEER_TOPK)
        idx = jnp.take_along_axis(cand_idx.reshape(T, PEER_HEADS, PEER_TOPK * PEER_TOPK), pos, axis=-1)
        g = jax.nn.softmax(best, axis=-1)
        ue = jnp.take(u_tab, idx, axis=0)
        ve = jnp.take(v_tab, idx, axis=0)
        act = jax.nn.gelu(jnp.einsum('thkd,td->thk', ue, t, preferred_element_type=jnp.float32),
                          approximate=False)
        return jnp.einsum('thk,thkd->td', (g * act).astype(t.dtype), ve)

    return lax.map(chunk, tok).reshape(B, L, D)


def trunk_layer(x, xc, c, c_ctx, w_mod, b_mod, n1, n2, w_in, sink, conv_w,
                w_attn_out, w_conv_out, w_mix_out, pw_q, p_keys, p_u, p_v, rope, update_ctx):
    B, L, _ = x.shape
    C = xc.shape[1]
    sh1, sc1, g1, sh2, sc2, g2 = [m[:, None, :] for m in adaln_params(c, w_mod, b_mod)]
    csh1, csc1, cg1, csh2, csc2, cg2 = adaln_params(c_ctx, w_mod, b_mod)

    hc = rmsnorm(xc, n1) * (1 + csc1) + csh1
    if update_ctx:
        qc, kc, vc, gbc, gcc, uc, gac, gvc = split_proj(hc @ w_in)
    else:
        kc, vc = jnp.split(hc @ w_in[:, KV_START:KV_END], 2, axis=-1)
    kc = kc.reshape(B, C, N_KV_HEADS, HEAD_DIM)
    vc = vc.reshape(B, C, N_KV_HEADS, HEAD_DIM)

    h = rmsnorm(x, n1) * (1 + sc1) + sh1
    q, k, v, gb, gcv, u, ga, gv = split_proj(h @ w_in)
    q = apply_axial_rope(q.reshape(B, L, N_Q_HEADS, HEAD_DIM), rope)
    k = apply_axial_rope(k.reshape(B, L, N_KV_HEADS, HEAD_DIM), rope)
    v = v.reshape(B, L, N_KV_HEADS, HEAD_DIM)
    y_attn = windowed_attention(q, k, v, kc, vc, sink) @ w_attn_out
    y_conv = conv_branch(gb, gcv, u, conv_w, w_conv_out)
    x = x + g1 * merge_branches(y_attn, y_conv, ga, gv, w_mix_out)
    h2 = rmsnorm(x, n2) * (1 + sc2) + sh2
    x = x + g2 * peer(h2, pw_q, p_keys, p_u, p_v)

    if update_ctx:
        yc_attn = context_attention(qc.reshape(B, C, N_Q_HEADS, HEAD_DIM), kc, vc, sink) @ w_attn_out
        yc_conv = conv_branch(gbc, gcc, uc, conv_w, w_conv_out)
        xc = xc + cg1 * merge_branches(yc_attn, yc_conv, gac, gvc, w_mix_out)
        hc2 = rmsnorm(xc, n2) * (1 + csc2) + csh2
        xc = xc + cg2 * peer(hc2, pw_q, p_keys, p_u, p_v)
    return x, xc


def setup_inputs(seed: int = 0) -> dict:
    key = jax.random.key(seed)
    ks = jax.random.split(key, 19)

    def nrm(k, shape, s):
        return jax.random.normal(k, shape, jnp.float32) * s

    return {
        "x": nrm(ks[0], (BATCH, SEQ, D_MODEL), 1.0),
        "c": nrm(ks[1], (BATCH, D_MODEL), 1.0),
        "ctx": nrm(ks[2], (BATCH, CTX_LEN, D_MODEL), 1.0),
        "c_ctx": nrm(ks[3], (D_MODEL,), 1.0),
        "w_mod": nrm(ks[4], (DEPTH, D_MODEL, N_MOD * D_MODEL), D_MODEL ** -0.5),
        "b_mod": nrm(ks[5], (DEPTH, N_MOD * D_MODEL), 0.02),
        "norm1_g": 1.0 + nrm(ks[6], (DEPTH, D_MODEL), 0.02),
        "norm2_g": 1.0 + nrm(ks[7], (DEPTH, D_MODEL), 0.02),
        "w_in": nrm(ks[8], (DEPTH, D_MODEL, PROJ_WIDTH), D_MODEL ** -0.5),
        "attn_sink": nrm(ks[9], (DEPTH, N_Q_HEADS), 1.0),
        "conv_w": nrm(ks[10], (DEPTH, CONV_K, CONV_WIDTH), CONV_K ** -0.5),
        "w_attn_out": nrm(ks[11], (DEPTH, ATTN_WIDTH, D_MODEL), ATTN_WIDTH ** -0.5),
        "w_conv_out": nrm(ks[12], (DEPTH, CONV_WIDTH, D_MODEL), CONV_WIDTH ** -0.5),
        "w_mix_out": nrm(ks[13], (DEPTH, D_MODEL, D_MODEL), D_MODEL ** -0.5),
        "peer_w_q": nrm(ks[14], (DEPTH, D_MODEL, PEER_HEADS * PEER_KEY_DIM), D_MODEL ** -0.5),
        "peer_sub_keys": nrm(ks[15], (DEPTH, PEER_HEADS, 2, PEER_N_KEYS, PEER_KEY_DIM // 2),
                             (PEER_KEY_DIM // 2) ** -0.5),
        "peer_u": nrm(ks[16], (DEPTH, PEER_N_EXPERTS, D_MODEL), D_MODEL ** -0.5),
        "peer_v": nrm(ks[17], (DEPTH, PEER_N_EXPERTS, D_MODEL), 0.5),
        "final_g": 1.0 + nrm(ks[18], (D_MODEL,), 0.02),
    }


def reference(x, c, ctx, c_ctx, w_mod, b_mod, norm1_g, norm2_g, w_in, attn_sink, conv_w,
              w_attn_out, w_conv_out, w_mix_out, peer_w_q, peer_sub_keys, peer_u, peer_v, final_g):
    rope = axial_rope_tables(x.shape[1])
    xc = ctx
    for l in range(DEPTH):
        x, xc = trunk_layer(x, xc, c, c_ctx, w_mod[l], b_mod[l], norm1_g[l], norm2_g[l], w_in[l],
                            attn_sink[l], conv_w[l], w_attn_out[l], w_conv_out[l], w_mix_out[l],
                            peer_w_q[l], peer_sub_keys[l], peer_u[l], peer_v[l], rope,
                            l < DEPTH - 1)
    return rmsnorm(x, final_g)
```

```python
import functools

import jax
import jax.numpy as jnp
from jax import lax
from jax.experimental import pallas as pl
from jax.experimental.pallas import tpu as pltpu

HEAD_DIM = 64
N_Q_HEADS = 8
N_KV_HEADS = 2
Q_PER_KV = N_Q_HEADS // N_KV_HEADS
WINDOW = 128
GRID_W = 64
ROPE_BASE = 10000.0
ROPE_PAIRS = HEAD_DIM // 4
PEER_HEADS = 8
PEER_N_KEYS = 128
PEER_TOPK = 16
N_MOD = 6
EPS = 1e-6
NEG_INF = -1e30

LANES = 128
VMEM_LIMIT = 56 * 1024 * 1024

F32 = jnp.float32
BF16 = jnp.bfloat16


def _cparams(*sem):
    return pltpu.CompilerParams(dimension_semantics=sem, vmem_limit_bytes=VMEM_LIMIT)


def _rmsnorm(x, g):
    return x * lax.rsqrt(jnp.mean(x * x, axis=-1, keepdims=True) + EPS) * g


def _gelu_exact(x):
    return 0.5 * x * (1.0 + lax.erf(x * (2.0 ** -0.5)))


def _adaln_kernel(cond_ref, w_ref, b_ref, o_ref):
    act = jax.nn.silu(cond_ref[...])
    o_ref[...] = jnp.dot(act, w_ref[...], precision=lax.Precision.HIGHEST,
                         preferred_element_type=F32) + b_ref[...]


def _adaln(cond, w_mod, b_mod):
    rows, d = cond.shape
    n = w_mod.shape[1]
    tn = d
    return pl.pallas_call(
        _adaln_kernel,
        out_shape=jax.ShapeDtypeStruct((rows, n), F32),
        grid=(n // tn,),
        in_specs=[pl.BlockSpec((rows, d), lambda j: (0, 0)),
                  pl.BlockSpec((d, tn), lambda j: (0, j)),
                  pl.BlockSpec((1, tn), lambda j: (0, j))],
        out_specs=pl.BlockSpec((rows, tn), lambda j: (0, j)),
        compiler_params=_cparams("parallel"),
        name="adaln",
    )(cond, w_mod, b_mod.reshape(1, n))


def _ctx_kv_kernel(xc_ref, mod_ref, n1_ref, w_ref, o_ref):
    d = xc_ref.shape[-1]
    mod = mod_ref[...]
    hc = _rmsnorm(xc_ref[...], n1_ref[...]) * (1.0 + mod[:, d:2 * d]) + mod[:, 0:d]
    o_ref[...] = jnp.dot(hc.astype(BF16), w_ref[...], preferred_element_type=F32).astype(BF16)


def _ctx_kv(ctx, mod3, ctx_row, n1, w_kv):
    b, c, d = ctx.shape
    kvw = w_kv.shape[1]
    return pl.pallas_call(
        _ctx_kv_kernel,
        out_shape=jax.ShapeDtypeStruct((b, c, kvw), BF16),
        grid=(b,),
        in_specs=[pl.BlockSpec((None, c, d), lambda i: (i, 0, 0)),
                  pl.BlockSpec((None, 1, N_MOD * d), lambda i: (ctx_row, 0, 0)),
                  pl.BlockSpec((1, d), lambda i: (0, 0)),
                  pl.BlockSpec((d, kvw), lambda i: (0, 0))],
        out_specs=pl.BlockSpec((None, c, kvw), lambda i: (i, 0, 0)),
        compiler_params=_cparams("parallel"),
        name="ctx_kv",
    )(ctx, mod3, n1, w_kv)


def _proj_kernel(x_ref, mod_ref, n1_ref, w_ref, cos_ref, sin_ref,
                 q_ref, kv_ref, gb_ref, cu_ref, ga_ref, gv_ref):
    d = x_ref.shape[-1]
    aw = q_ref.shape[-1]
    kw = kv_ref.shape[-1] // 2
    cw = gb_ref.shape[-1]
    mod = mod_ref[...]
    h = (_rmsnorm(x_ref[...], n1_ref[...]) * (1.0 + mod[:, d:2 * d]) + mod[:, 0:d]).astype(BF16)
    cos = cos_ref[...]
    sin = sin_ref[...]
    lane = lax.broadcasted_iota(jnp.int32, cos.shape, 1)
    first_half = (lane % (2 * ROPE_PAIRS)) < ROPE_PAIRS

    def rope(z):
        partner = jnp.where(first_half, pltpu.roll(z, LANES - ROPE_PAIRS, 1),
                            pltpu.roll(z, ROPE_PAIRS, 1))
        return z * cos + partner * sin

    def proj(lo, width):
        return jnp.dot(h, w_ref[:, lo:lo + width], preferred_element_type=F32)

    off = 0
    zq = proj(off, aw)
    for g in range(aw // LANES):
        q_ref[:, g * LANES:(g + 1) * LANES] = rope(zq[:, g * LANES:(g + 1) * LANES]).astype(BF16)
    off += aw
    zkv = proj(off, 2 * kw)
    for g in range(kw // LANES):
        kv_ref[:, g * LANES:(g + 1) * LANES] = rope(zkv[:, g * LANES:(g + 1) * LANES]).astype(BF16)
    kv_ref[:, kw:] = zkv[:, kw:].astype(BF16)
    off += 2 * kw
    gb_ref[...] = proj(off, cw)
    off += cw
    zc = proj(off, cw)
    off += cw
    cu_ref[...] = zc * proj(off, cw)
    off += cw
    ga_ref[...] = proj(off, d)
    off += d
    gv_ref[...] = proj(off, d)


def _proj(x2, mod3, n1, w_in, cos, sin, seq, tm):
    t, d = x2.shape
    aw = N_Q_HEADS * HEAD_DIM
    kw = N_KV_HEADS * HEAD_DIM
    cw = d // 2
    per_seq = seq // tm
    row = lambda i: (i, 0)
    return pl.pallas_call(
        _proj_kernel,
        out_shape=(jax.ShapeDtypeStruct((t, aw), BF16),
                   jax.ShapeDtypeStruct((t, 2 * kw), BF16),
                   jax.ShapeDtypeStruct((t, cw), F32),
                   jax.ShapeDtypeStruct((t, cw), F32),
                   jax.ShapeDtypeStruct((t, d), F32),
                   jax.ShapeDtypeStruct((t, d), F32)),
        grid=(t // tm,),
        in_specs=[pl.BlockSpec((tm, d), row),
                  pl.BlockSpec((None, 1, N_MOD * d), lambda i: (i // per_seq, 0, 0)),
                  pl.BlockSpec((1, d), lambda i: (0, 0)),
                  pl.BlockSpec(w_in.shape, lambda i: (0, 0)),
                  pl.BlockSpec((tm, LANES), lambda i: (i % per_seq, 0)),
                  pl.BlockSpec((tm, LANES), lambda i: (i % per_seq, 0))],
        out_specs=(pl.BlockSpec((tm, aw), row), pl.BlockSpec((tm, 2 * kw), row),
                   pl.BlockSpec((tm, cw), row), pl.BlockSpec((tm, cw), row),
                   pl.BlockSpec((tm, d), row), pl.BlockSpec((tm, d), row)),
        compiler_params=_cparams("parallel"),
        name="proj",
    )(x2, mod3, n1, w_in, cos, sin)


def _mixer_kernel(sink_ref, x_ref, q_ref, kv_ref, kvp_ref, kvn_ref, kvc_ref,
                  cu_ref, cup_ref, cun_ref, gb_ref, ga_ref, gv_ref, convw_ref,
                  wa_ref, wc_ref, wm_ref, mod_ref, o_ref, attn_scr):
    n = pl.program_id(1)
    has_prev = n > 0
    has_next = n < pl.num_programs(1) - 1
    tq, d = x_ref.shape
    kw = N_KV_HEADS * HEAD_DIM
    scale = HEAD_DIM ** -0.5
    nt = (((1,), (1,)), ((), ()))

    kext = jnp.concatenate([kvp_ref[...], kv_ref[...], kvn_ref[...]], axis=0)
    kctx = kvc_ref[...]
    qi = lax.broadcasted_iota(jnp.int32, (WINDOW, WINDOW), 0)
    ki = lax.broadcasted_iota(jnp.int32, (WINDOW, WINDOW), 1)
    nblk = tq // WINDOW
    for j in range(nblk):
        prev_ok = ki >= qi
        next_ok = ki <= qi
        if j == 0:
            prev_ok = jnp.logical_and(prev_ok, has_prev)
        if j == nblk - 1:
            next_ok = jnp.logical_and(next_ok, has_next)
        mask = jnp.concatenate([prev_ok, jnp.ones_like(prev_ok), next_ok], axis=1)
        rows = slice(j * WINDOW, (j + 1) * WINDOW)
        krows = slice(j * WINDOW, (j + 3) * WINDOW)
        for hq in range(N_Q_HEADS):
            kvh = hq // Q_PER_KV
            kcol = slice(kvh * HEAD_DIM, (kvh + 1) * HEAD_DIM)
            vcol = slice(kw + kvh * HEAD_DIM, kw + (kvh + 1) * HEAD_DIM)
            qh = q_ref[rows, hq * HEAD_DIM:(hq + 1) * HEAD_DIM]
            s_loc = lax.dot_general(qh, kext[krows, kcol], nt, preferred_element_type=F32) * scale
            s_loc = jnp.where(mask, s_loc, NEG_INF)
            s_ctx = lax.dot_general(qh, kctx[:, kcol], nt, preferred_element_type=F32) * scale
            sink = sink_ref[0, hq]
            m = jnp.maximum(jnp.max(s_loc, axis=-1, keepdims=True),
                            jnp.max(s_ctx, axis=-1, keepdims=True))
            m = jnp.maximum(m, sink)
            p_loc = jnp.exp(s_loc - m)
            p_ctx = jnp.exp(s_ctx - m)
            den = (jnp.sum(p_loc, axis=-1, keepdims=True) + jnp.sum(p_ctx, axis=-1, keepdims=True)
                   + jnp.exp(sink - m))
            o = (jnp.dot(p_loc.astype(BF16), kext[krows, vcol], preferred_element_type=F32)
                 + jnp.dot(p_ctx.astype(BF16), kctx[:, vcol], preferred_element_type=F32))
            attn_scr[rows, hq * HEAD_DIM:(hq + 1) * HEAD_DIM] = o / den

    y_attn = jnp.dot(attn_scr[...].astype(BF16), wa_ref[...], preferred_element_type=F32)

    cu = cu_ref[...]
    ri = lax.broadcasted_iota(jnp.int32, cu.shape, 0)
    prev_row = jnp.where(has_prev, cup_ref[7:8, :], 0.0)
    next_row = jnp.where(has_next, cun_ref[0:1, :], 0.0)
    cu_m1 = jnp.where(ri == 0, prev_row, pltpu.roll(cu, 1, 0))
    cu_p1 = jnp.where(ri == tq - 1, next_row, pltpu.roll(cu, tq - 1, 0))
    cw = convw_ref[...]
    conv = cu_m1 * cw[0:1, :] + cu * cw[1:2, :] + cu_p1 * cw[2:3, :]
    y_conv = jnp.dot((gb_ref[...] * conv).astype(BF16), wc_ref[...], preferred_element_type=F32)

    merged = jax.nn.sigmoid(ga_ref[...]) * y_attn + jax.nn.sigmoid(gv_ref[...]) * y_conv
    y = jnp.dot(merged.astype(BF16), wm_ref[...], preferred_element_type=F32)
    g1 = mod_ref[:, 2 * d:3 * d]
    o_ref[...] = x_ref[...] + g1 * y


def _mixer(x2, q, kv, kvc, cu, gb, ga, gv, sink, conv_w, wa, wc, wm, mod3, batch, seq, tq):
    t, d = x2.shape
    aw = q.shape[1]
    kv2 = kv.shape[1]
    cw = cu.shape[1]
    c = kvc.shape[1]
    nq = seq // tq
    nb = seq // WINDOW
    sub = tq // WINDOW
    kv3 = kv.reshape(t // WINDOW, WINDOW, kv2)
    cu3 = cu.reshape(t // 8, 8, cw)
    tile = lambda b, n: (b * nq + n, 0)
    const = lambda b, n: (0, 0)
    return pl.pallas_call(
        _mixer_kernel,
        out_shape=jax.ShapeDtypeStruct((t, d), F32),
        grid=(batch, nq),
        in_specs=[
            pl.BlockSpec(memory_space=pltpu.SMEM),
            pl.BlockSpec((tq, d), tile),
            pl.BlockSpec((tq, aw), tile),
            pl.BlockSpec((tq, kv2), tile),
            pl.BlockSpec((None, WINDOW, kv2), lambda b, n: (b * nb + jnp.maximum(n * sub - 1, 0), 0, 0)),
            pl.BlockSpec((None, WINDOW, kv2), lambda b, n: (b * nb + jnp.minimum((n + 1) * sub, nb - 1), 0, 0)),
            pl.BlockSpec((None, c, kv2), lambda b, n: (b, 0, 0)),
            pl.BlockSpec((tq, cw), tile),
            pl.BlockSpec((None, 8, cw), lambda b, n: (jnp.maximum((b * seq + n * tq) // 8 - 1, 0), 0, 0)),
            pl.BlockSpec((None, 8, cw), lambda b, n: (jnp.minimum((b * seq + (n + 1) * tq) // 8, t // 8 - 1), 0, 0)),
            pl.BlockSpec((tq, cw), tile),
            pl.BlockSpec((tq, d), tile),
            pl.BlockSpec((tq, d), tile),
            pl.BlockSpec(conv_w.shape, const),
            pl.BlockSpec(wa.shape, const),
            pl.BlockSpec(wc.shape, const),
            pl.BlockSpec(wm.shape, const),
            pl.BlockSpec((None, 1, N_MOD * d), lambda b, n: (b, 0, 0)),
        ],
        out_specs=pl.BlockSpec((tq, d), tile),
        scratch_shapes=[pltpu.VMEM((tq, aw), F32)],
        compiler_params=_cparams("parallel", "parallel"),
        name="mixer",
    )(sink, x2, q, kv, kv3, kv3, kvc, cu, cu3, cu3, gb, ga, gv, conv_w, wa, wc, wm, mod3)


def _topk_rows(s, k, payload=None):
    n = s.shape[0]
    iota = lax.broadcasted_iota(jnp.int32, s.shape, 0)
    vals, ids = [], []
    for _ in range(k):
        m = jnp.max(s, axis=0, keepdims=True)
        am = jnp.min(jnp.where(s == m, iota, n), axis=0, keepdims=True)
        hit = iota == am
        vals.append(m)
        if payload is None:
            ids.append(am)
        else:
            ids.append(jnp.max(jnp.where(hit, payload, -1), axis=0, keepdims=True))
        s = jnp.where(hit, -jnp.inf, s)
    return jnp.concatenate(vals, axis=0), jnp.concatenate(ids, axis=0)


def _route_kernel(x1_ref, mod_ref, n2_ref, wq_ref, keys_ref, h2_ref, idx_ref, gate_ref, h2b_scr):
    d = x1_ref.shape[-1]
    nt = (((1,), (1,)), ((), ()))

    @pl.when(pl.program_id(1) == 0)
    def _():
        mod = mod_ref[...]
        h2 = _rmsnorm(x1_ref[...], n2_ref[...]) * (1.0 + mod[:, 4 * d:5 * d]) + mod[:, 3 * d:4 * d]
        h2_ref[...] = h2
        h2b_scr[...] = h2.astype(BF16)

    qp = jnp.dot(h2b_scr[...], wq_ref[...], preferred_element_type=F32)
    half = qp.shape[1] // 2
    tops = []
    for p in range(2):
        qh = qp[:, p * half:(p + 1) * half].astype(BF16)
        st = lax.dot_general(keys_ref[p], qh, nt, preferred_element_type=F32)
        tops.append(_topk_rows(st, PEER_TOPK))
    (a, ia), (b, ib) = tops
    cand = jnp.concatenate([a[i:i + 1, :] + b for i in range(PEER_TOPK)], axis=0)
    cidx = jnp.concatenate([ia[i:i + 1, :] * PEER_N_KEYS + ib for i in range(PEER_TOPK)], axis=0)
    best, idx = _topk_rows(cand, PEER_TOPK, payload=cidx)
    e = jnp.exp(best - best[0:1, :])
    gate_ref[...] = e / jnp.sum(e, axis=0, keepdims=True)
    idx_ref[...] = idx


def _route(x1, mod3, n2, wq, keys, seq, tm):
    t, d = x1.shape
    hw = wq.shape[1] // PEER_HEADS
    per_seq = seq // tm
    return pl.pallas_call(
        _route_kernel,
        out_shape=(jax.ShapeDtypeStruct((t, d), F32),
                   jax.ShapeDtypeStruct((PEER_HEADS, PEER_TOPK, t), jnp.int32),
                   jax.ShapeDtypeStruct((PEER_HEADS, PEER_TOPK, t), F32)),
        grid=(t // tm, PEER_HEADS),
        in_specs=[pl.BlockSpec((tm, d), lambda i, h: (i, 0)),
                  pl.BlockSpec((None, 1, N_MOD * d), lambda i, h: (i // per_seq, 0, 0)),
                  pl.BlockSpec((1, d), lambda i, h: (0, 0)),
                  pl.BlockSpec((d, hw), lambda i, h: (0, h)),
                  pl.BlockSpec((None, 2, PEER_N_KEYS, hw // 2), lambda i, h: (h, 0, 0, 0))],
        out_specs=(pl.BlockSpec((tm, d), lambda i, h: (i, 0)),
                   pl.BlockSpec((None, PEER_TOPK, tm), lambda i, h: (h, 0, i)),
                   pl.BlockSpec((None, PEER_TOPK, tm), lambda i, h: (h, 0, i))),
        scratch_shapes=[pltpu.VMEM((tm, d), BF16)],
        compiler_params=_cparams("parallel", "arbitrary"),
        name="route",
    )(x1, mod3, n2, wq, keys)


def _peer_kernel(idx_ref, idxn_ref, gate_ref, h2_ref, x1_ref, mod_ref, fg_ref, u_hbm, v_hbm,
                 o_ref, ubuf, vbuf, sem):
    i = pl.program_id(0)
    tt, k = gate_ref.shape
    d = x1_ref.shape[-1]
    slot = i % 2

    def row_copy(tab, buf, which, sl, src_row, dst_row):
        return pltpu.make_async_copy(tab.at[pl.ds(src_row, 1)], buf.at[sl, pl.ds(dst_row, 1)],
                                     sem.at[which, sl])

    def issue(iref, sl):
        def tok(t, carry):
            for j in range(k):
                row = iref[t, j]
                row_copy(u_hbm, ubuf, 0, sl, row, t * k + j).start()
                row_copy(v_hbm, vbuf, 1, sl, row, t * k + j).start()
            return carry
        lax.fori_loop(0, tt, tok, 0)

    @pl.when(i == 0)
    def _():
        issue(idx_ref, 0)

    @pl.when(i + 1 < pl.num_programs(0))
    def _():
        issue(idxn_ref, 1 - slot)

    pltpu.make_async_copy(u_hbm.at[pl.ds(0, tt * k)], ubuf.at[slot], sem.at[0, slot]).wait()
    pltpu.make_async_copy(v_hbm.at[pl.ds(0, tt * k)], vbuf.at[slot], sem.at[1, slot]).wait()

    cols = []
    for t in range(tt):
        ub = ubuf[slot, t * k:(t + 1) * k, :]
        cols.append(jnp.sum(ub * h2_ref[t:t + 1, :], axis=1, keepdims=True))
    pre = jnp.concatenate(cols, axis=1)
    w = gate_ref[...].T * _gelu_exact(pre)
    outs = []
    for t in range(tt):
        vb = vbuf[slot, t * k:(t + 1) * k, :]
        outs.append(jnp.sum(w[:, t:t + 1] * vb, axis=0, keepdims=True))
    peer = jnp.concatenate(outs, axis=0)
    x2 = x1_ref[...] + mod_ref[:, 5 * d:6 * d] * peer
    o_ref[...] = _rmsnorm(x2, fg_ref[...])


def _peer(idx, gate, h2, x1, mod3, fg, u_tab, v_tab, seq, tt):
    t, d = x1.shape
    k = idx.shape[1]
    n = t // tt
    per_seq = seq // tt
    row = lambda i: (i, 0)
    return pl.pallas_call(
        _peer_kernel,
        out_shape=jax.ShapeDtypeStruct((t, d), F32),
        grid=(n,),
        in_specs=[pl.BlockSpec((tt, k), row, memory_space=pltpu.SMEM),
                  pl.BlockSpec((tt, k), lambda i: (jnp.minimum(i + 1, n - 1), 0), memory_space=pltpu.SMEM),
                  pl.BlockSpec((tt, k), row),
                  pl.BlockSpec((tt, d), row),
                  pl.BlockSpec((tt, d), row),
                  pl.BlockSpec((None, 1, N_MOD * d), lambda i: (i // per_seq, 0, 0)),
                  pl.BlockSpec((1, d), lambda i: (0, 0)),
                  pl.BlockSpec(memory_space=pl.ANY),
                  pl.BlockSpec(memory_space=pl.ANY)],
        out_specs=pl.BlockSpec((tt, d), row),
        scratch_shapes=[pltpu.VMEM((2, tt * k, d), F32),
                        pltpu.VMEM((2, tt * k, d), F32),
                        pltpu.SemaphoreType.DMA((2, 2))],
        compiler_params=_cparams("arbitrary"),
        name="peer",
    )(idx, idx, gate, h2, x1, mod3, fg, u_tab, v_tab)


def _rope_tables(length):
    rows = length // GRID_W
    row = jnp.repeat(jnp.arange(rows, dtype=F32), GRID_W)
    col = jnp.tile(jnp.arange(GRID_W, dtype=F32), rows)
    inv_freq = ROPE_BASE ** (-jnp.arange(ROPE_PAIRS, dtype=F32) / ROPE_PAIRS)
    ang_r = row[:, None] * inv_freq
    ang_c = col[:, None] * inv_freq
    cos = jnp.concatenate([jnp.cos(ang_r)] * 2 + [jnp.cos(ang_c)] * 2, axis=-1)
    sin = jnp.concatenate([-jnp.sin(ang_r), jnp.sin(ang_r), -jnp.sin(ang_c), jnp.sin(ang_c)], axis=-1)
    reps = LANES // HEAD_DIM
    return jnp.tile(cos, (1, reps)), jnp.tile(sin, (1, reps))


def _layer(x, c, ctx, c_ctx, w_mod, b_mod, n1, n2, w_in, sink, conv_w, w_attn_out, w_conv_out,
           w_mix_out, pw_q, p_keys, p_u, p_v, final_g, tm, tq, tr, tt):
    batch, seq, d = x.shape
    t = batch * seq
    aw = N_Q_HEADS * HEAD_DIM
    kw = N_KV_HEADS * HEAD_DIM

    rows = -(-(batch + 1) // 8) * 8
    cond = jnp.zeros((rows, d), F32).at[:batch].set(c).at[batch].set(c_ctx)
    mod3 = _adaln(cond, w_mod, b_mod).reshape(rows, 1, N_MOD * d)

    w_in_b = w_in.astype(BF16)
    kvc = _ctx_kv(ctx, mod3, batch, n1.reshape(1, d), w_in_b[:, aw:aw + 2 * kw])

    cos, sin = _rope_tables(seq)
    x2 = x.reshape(t, d)
    q, kv, gb, cu, ga, gv = _proj(x2, mod3, n1.reshape(1, d), w_in_b, cos, sin, seq, tm)
    x1 = _mixer(x2, q, kv, kvc, cu, gb, ga, gv, sink.reshape(1, N_Q_HEADS), conv_w,
                w_attn_out.astype(BF16), w_conv_out.astype(BF16), w_mix_out.astype(BF16),
                mod3, batch, seq, tq)

    keys = p_keys.astype(BF16)
    h2, idx_t, gate_t = _route(x1, mod3, n2.reshape(1, d), pw_q.astype(BF16), keys, seq, tr)
    nsel = PEER_HEADS * PEER_TOPK
    idx = idx_t.reshape(nsel, t).T
    gate = gate_t.reshape(nsel, t).T
    out = _peer(idx, gate, h2, x1, mod3, final_g.reshape(1, d), p_u, p_v, seq, tt)
    return out.reshape(batch, seq, d)


def kernel(x, c, ctx, c_ctx, w_mod, b_mod, norm1_g, norm2_g, w_in, attn_sink, conv_w, w_attn_out,
           w_conv_out, w_mix_out, peer_w_q, peer_sub_keys, peer_u, peer_v, final_g):
    assert w_mod.shape[0] == 1, "only the single-layer configuration is implemented"
    seq = x.shape[1]
    return _layer(x, c, ctx, c_ctx, w_mod[0], b_mod[0], norm1_g[0], norm2_g[0], w_in[0],
                  attn_sink[0], conv_w[0], w_attn_out[0], w_conv_out[0], w_mix_out[0],
                  peer_w_q[0], peer_sub_keys[0], peer_u[0], peer_v[0], final_g,
                  tm=min(512, seq), tq=min(256, seq), tr=min(256, seq), tt=8)
```

```python
import dataclasses

import jax
import jax.numpy as jnp
from jax import lax
from jax.experimental import pallas as pl
from jax.experimental.pallas import tpu as pltpu
from jax.experimental.pallas import tpu_sc as plsc

HEAD_DIM = 64
N_Q_HEADS = 8
N_KV_HEADS = 2
Q_PER_KV = N_Q_HEADS // N_KV_HEADS
WINDOW = 128
GRID_W = 64
ROPE_BASE = 10000.0
ROPE_PAIRS = HEAD_DIM // 4
PEER_HEADS = 8
PEER_N_KEYS = 128
PEER_TOPK = 16
N_MOD = 6
EPS = 1e-6
NEG_INF = -1e30

LANES = 128
VMEM_LIMIT = 56 * 1024 * 1024

F32 = jnp.float32
BF16 = jnp.bfloat16


def _cparams(*sem):
    return pltpu.CompilerParams(dimension_semantics=sem, vmem_limit_bytes=VMEM_LIMIT)


def _rmsnorm(x, g):
    return x * lax.rsqrt(jnp.mean(x * x, axis=-1, keepdims=True) + EPS) * g


def _gelu_exact(x):
    return 0.5 * x * (1.0 + lax.erf(x * (2.0 ** -0.5)))


def _adaln_kernel(cond_ref, w_ref, b_ref, o_ref):
    act = jax.nn.silu(cond_ref[...])
    o_ref[...] = jnp.dot(act, w_ref[...], precision=lax.Precision.HIGHEST,
                         preferred_element_type=F32) + b_ref[...]


def _adaln(cond, w_mod, b_mod):
    rows, d = cond.shape
    n = w_mod.shape[1]
    tn = d
    return pl.pallas_call(
        _adaln_kernel,
        out_shape=jax.ShapeDtypeStruct((rows, n), F32),
        grid=(n // tn,),
        in_specs=[pl.BlockSpec((rows, d), lambda j: (0, 0)),
                  pl.BlockSpec((d, tn), lambda j: (0, j)),
                  pl.BlockSpec((1, tn), lambda j: (0, j))],
        out_specs=pl.BlockSpec((rows, tn), lambda j: (0, j)),
        compiler_params=_cparams("parallel"),
        name="adaln",
    )(cond, w_mod, b_mod.reshape(1, n))


def _ctx_kv_kernel(xc_ref, mod_ref, n1_ref, w_ref, o_ref):
    d = xc_ref.shape[-1]
    mod = mod_ref[...]
    hc = _rmsnorm(xc_ref[...], n1_ref[...]) * (1.0 + mod[:, d:2 * d]) + mod[:, 0:d]
    o_ref[...] = jnp.dot(hc.astype(BF16), w_ref[...], preferred_element_type=F32).astype(BF16)


def _ctx_kv(ctx, mod3, ctx_row, n1, w_kv):
    b, c, d = ctx.shape
    kvw = w_kv.shape[1]
    return pl.pallas_call(
        _ctx_kv_kernel,
        out_shape=jax.ShapeDtypeStruct((b, c, kvw), BF16),
        grid=(b,),
        in_specs=[pl.BlockSpec((None, c, d), lambda i: (i, 0, 0)),
                  pl.BlockSpec((None, 1, N_MOD * d), lambda i: (ctx_row, 0, 0)),
                  pl.BlockSpec((1, d), lambda i: (0, 0)),
                  pl.BlockSpec((d, kvw), lambda i: (0, 0))],
        out_specs=pl.BlockSpec((None, c, kvw), lambda i: (i, 0, 0)),
        compiler_params=_cparams("parallel"),
        name="ctx_kv",
    )(ctx, mod3, n1, w_kv)


def _proj_kernel(x_ref, mod_ref, n1_ref, w_ref, cos_ref, sin_ref,
                 q_ref, kv_ref, gb_ref, cu_ref, ga_ref, gv_ref):
    d = x_ref.shape[-1]
    aw = q_ref.shape[-1]
    kw = kv_ref.shape[-1] // 2
    cw = gb_ref.shape[-1]
    mod = mod_ref[...]
    h = (_rmsnorm(x_ref[...], n1_ref[...]) * (1.0 + mod[:, d:2 * d]) + mod[:, 0:d]).astype(BF16)
    cos = cos_ref[...]
    sin = sin_ref[...]
    lane = lax.broadcasted_iota(jnp.int32, cos.shape, 1)
    first_half = (lane % (2 * ROPE_PAIRS)) < ROPE_PAIRS

    def rope(z):
        partner = jnp.where(first_half, pltpu.roll(z, LANES - ROPE_PAIRS, 1),
                            pltpu.roll(z, ROPE_PAIRS, 1))
        return z * cos + partner * sin

    def proj(lo, width):
        return jnp.dot(h, w_ref[:, lo:lo + width], preferred_element_type=F32)

    off = 0
    zq = proj(off, aw)
    for g in range(aw // LANES):
        q_ref[:, g * LANES:(g + 1) * LANES] = rope(zq[:, g * LANES:(g + 1) * LANES]).astype(BF16)
    off += aw
    zkv = proj(off, 2 * kw)
    for g in range(kw // LANES):
        kv_ref[:, g * LANES:(g + 1) * LANES] = rope(zkv[:, g * LANES:(g + 1) * LANES]).astype(BF16)
    kv_ref[:, kw:] = zkv[:, kw:].astype(BF16)
    off += 2 * kw
    gb_ref[...] = proj(off, cw)
    off += cw
    zc = proj(off, cw)
    off += cw
    cu_ref[...] = zc * proj(off, cw)
    off += cw
    ga_ref[...] = proj(off, d)
    off += d
    gv_ref[...] = proj(off, d)


def _proj(x2, mod3, n1, w_in, cos, sin, seq, tm):
    t, d = x2.shape
    aw = N_Q_HEADS * HEAD_DIM
    kw = N_KV_HEADS * HEAD_DIM
    cw = d // 2
    per_seq = seq // tm
    row = lambda i: (i, 0)
    return pl.pallas_call(
        _proj_kernel,
        out_shape=(jax.ShapeDtypeStruct((t, aw), BF16),
                   jax.ShapeDtypeStruct((t, 2 * kw), BF16),
                   jax.ShapeDtypeStruct((t, cw), F32),
                   jax.ShapeDtypeStruct((t, cw), F32),
                   jax.ShapeDtypeStruct((t, d), F32),
                   jax.ShapeDtypeStruct((t, d), F32)),
        grid=(t // tm,),
        in_specs=[pl.BlockSpec((tm, d), row),
                  pl.BlockSpec((None, 1, N_MOD * d), lambda i: (i // per_seq, 0, 0)),
                  pl.BlockSpec((1, d), lambda i: (0, 0)),
                  pl.BlockSpec(w_in.shape, lambda i: (0, 0)),
                  pl.BlockSpec((tm, LANES), lambda i: (i % per_seq, 0)),
                  pl.BlockSpec((tm, LANES), lambda i: (i % per_seq, 0))],
        out_specs=(pl.BlockSpec((tm, aw), row), pl.BlockSpec((tm, 2 * kw), row),
                   pl.BlockSpec((tm, cw), row), pl.BlockSpec((tm, cw), row),
                   pl.BlockSpec((tm, d), row), pl.BlockSpec((tm, d), row)),
        compiler_params=_cparams("parallel"),
        name="proj",
    )(x2, mod3, n1, w_in, cos, sin)


def _mixer_kernel(sink_ref, x_ref, q_ref, kv_ref, kvp_ref, kvn_ref, kvc_ref,
                  cu_ref, cup_ref, cun_ref, gb_ref, ga_ref, gv_ref, convw_ref,
                  wa_ref, wc_ref, wm_ref, mod_ref, o_ref, attn_scr):
    n = pl.program_id(1)
    has_prev = n > 0
    has_next = n < pl.num_programs(1) - 1
    tq, d = x_ref.shape
    kw = N_KV_HEADS * HEAD_DIM
    scale = HEAD_DIM ** -0.5
    nt = (((1,), (1,)), ((), ()))

    kext = jnp.concatenate([kvp_ref[...], kv_ref[...], kvn_ref[...]], axis=0)
    kctx = kvc_ref[...]
    qi = lax.broadcasted_iota(jnp.int32, (WINDOW, WINDOW), 0)
    ki = lax.broadcasted_iota(jnp.int32, (WINDOW, WINDOW), 1)
    nblk = tq // WINDOW
    for j in range(nblk):
        prev_ok = ki >= qi
        next_ok = ki <= qi
        if j == 0:
            prev_ok = jnp.logical_and(prev_ok, has_prev)
        if j == nblk - 1:
            next_ok = jnp.logical_and(next_ok, has_next)
        mask = jnp.concatenate([prev_ok, jnp.ones_like(prev_ok), next_ok], axis=1)
        rows = slice(j * WINDOW, (j + 1) * WINDOW)
        krows = slice(j * WINDOW, (j + 3) * WINDOW)
        for hq in range(N_Q_HEADS):
            kvh = hq // Q_PER_KV
            kcol = slice(kvh * HEAD_DIM, (kvh + 1) * HEAD_DIM)
            vcol = slice(kw + kvh * HEAD_DIM, kw + (kvh + 1) * HEAD_DIM)
            qh = q_ref[rows, hq * HEAD_DIM:(hq + 1) * HEAD_DIM]
            s_loc = lax.dot_general(qh, kext[krows, kcol], nt, preferred_element_type=F32) * scale
            s_loc = jnp.where(mask, s_loc, NEG_INF)
            s_ctx = lax.dot_general(qh, kctx[:, kcol], nt, preferred_element_type=F32) * scale
            sink = sink_ref[0, hq]
            m = jnp.maximum(jnp.max(s_loc, axis=-1, keepdims=True),
                            jnp.max(s_ctx, axis=-1, keepdims=True))
            m = jnp.maximum(m, sink)
            p_loc = jnp.exp(s_loc - m)
            p_ctx = jnp.exp(s_ctx - m)
            den = (jnp.sum(p_loc, axis=-1, keepdims=True) + jnp.sum(p_ctx, axis=-1, keepdims=True)
                   + jnp.exp(sink - m))
            o = (jnp.dot(p_loc.astype(BF16), kext[krows, vcol], preferred_element_type=F32)
                 + jnp.dot(p_ctx.astype(BF16), kctx[:, vcol], preferred_element_type=F32))
            attn_scr[rows, hq * HEAD_DIM:(hq + 1) * HEAD_DIM] = o / den

    y_attn = jnp.dot(attn_scr[...].astype(BF16), wa_ref[...], preferred_element_type=F32)

    cu = cu_ref[...]
    ri = lax.broadcasted_iota(jnp.int32, cu.shape, 0)
    prev_row = jnp.where(has_prev, cup_ref[7:8, :], 0.0)
    next_row = jnp.where(has_next, cun_ref[0:1, :], 0.0)
    cu_m1 = jnp.where(ri == 0, prev_row, pltpu.roll(cu, 1, 0))
    cu_p1 = jnp.where(ri == tq - 1, next_row, pltpu.roll(cu, tq - 1, 0))
    cw = convw_ref[...]
    conv = cu_m1 * cw[0:1, :] + cu * cw[1:2, :] + cu_p1 * cw[2:3, :]
    y_conv = jnp.dot((gb_ref[...] * conv).astype(BF16), wc_ref[...], preferred_element_type=F32)

    merged = jax.nn.sigmoid(ga_ref[...]) * y_attn + jax.nn.sigmoid(gv_ref[...]) * y_conv
    y = jnp.dot(merged.astype(BF16), wm_ref[...], preferred_element_type=F32)
    g1 = mod_ref[:, 2 * d:3 * d]
    o_ref[...] = x_ref[...] + g1 * y


def _mixer(x2, q, kv, kvc, cu, gb, ga, gv, sink, conv_w, wa, wc, wm, mod3, batch, seq, tq):
    t, d = x2.shape
    aw = q.shape[1]
    kv2 = kv.shape[1]
    cw = cu.shape[1]
    c = kvc.shape[1]
    nq = seq // tq
    nb = seq // WINDOW
    sub = tq // WINDOW
    kv3 = kv.reshape(t // WINDOW, WINDOW, kv2)
    cu3 = cu.reshape(t // 8, 8, cw)
    tile = lambda b, n: (b * nq + n, 0)
    const = lambda b, n: (0, 0)
    return pl.pallas_call(
        _mixer_kernel,
        out_shape=jax.ShapeDtypeStruct((t, d), F32),
        grid=(batch, nq),
        in_specs=[
            pl.BlockSpec(memory_space=pltpu.SMEM),
            pl.BlockSpec((tq, d), tile),
            pl.BlockSpec((tq, aw), tile),
            pl.BlockSpec((tq, kv2), tile),
            pl.BlockSpec((None, WINDOW, kv2), lambda b, n: (b * nb + jnp.maximum(n * sub - 1, 0), 0, 0)),
            pl.BlockSpec((None, WINDOW, kv2), lambda b, n: (b * nb + jnp.minimum((n + 1) * sub, nb - 1), 0, 0)),
            pl.BlockSpec((None, c, kv2), lambda b, n: (b, 0, 0)),
            pl.BlockSpec((tq, cw), tile),
            pl.BlockSpec((None, 8, cw), lambda b, n: (jnp.maximum((b * seq + n * tq) // 8 - 1, 0), 0, 0)),
            pl.BlockSpec((None, 8, cw), lambda b, n: (jnp.minimum((b * seq + (n + 1) * tq) // 8, t // 8 - 1), 0, 0)),
            pl.BlockSpec((tq, cw), tile),
            pl.BlockSpec((tq, d), tile),
            pl.BlockSpec((tq, d), tile),
            pl.BlockSpec(conv_w.shape, const),
            pl.BlockSpec(wa.shape, const),
            pl.BlockSpec(wc.shape, const),
            pl.BlockSpec(wm.shape, const),
            pl.BlockSpec((None, 1, N_MOD * d), lambda b, n: (b, 0, 0)),
        ],
        out_specs=pl.BlockSpec((tq, d), tile),
        scratch_shapes=[pltpu.VMEM((tq, aw), F32)],
        compiler_params=_cparams("parallel", "parallel"),
        name="mixer",
    )(sink, x2, q, kv, kv3, kv3, kvc, cu, cu3, cu3, gb, ga, gv, conv_w, wa, wc, wm, mod3)


def _topk_rows(s, k, payload=None):
    n = s.shape[0]
    iota = lax.broadcasted_iota(jnp.int32, s.shape, 0)
    vals, ids = [], []
    for _ in range(k):
        m = jnp.max(s, axis=0, keepdims=True)
        am = jnp.min(jnp.where(s == m, iota, n), axis=0, keepdims=True)
        hit = iota == am
        vals.append(m)
        if payload is None:
            ids.append(am)
        else:
            ids.append(jnp.max(jnp.where(hit, payload, -1), axis=0, keepdims=True))
        s = jnp.where(hit, -jnp.inf, s)
    return jnp.concatenate(vals, axis=0), jnp.concatenate(ids, axis=0)


def _route_kernel(x1_ref, mod_ref, n2_ref, wq_ref, keys_ref, h2_ref, idx_ref, gate_ref, h2b_scr):
    d = x1_ref.shape[-1]
    nt = (((1,), (1,)), ((), ()))

    @pl.when(pl.program_id(1) == 0)
    def _():
        mod = mod_ref[...]
        h2 = _rmsnorm(x1_ref[...], n2_ref[...]) * (1.0 + mod[:, 4 * d:5 * d]) + mod[:, 3 * d:4 * d]
        h2_ref[...] = h2
        h2b_scr[...] = h2.astype(BF16)

    qp = jnp.dot(h2b_scr[...], wq_ref[...], preferred_element_type=F32)
    half = qp.shape[1] // 2
    tops = []
    for p in range(2):
        qh = qp[:, p * half:(p + 1) * half].astype(BF16)
        st = lax.dot_general(keys_ref[p], qh, nt, preferred_element_type=F32)
        tops.append(_topk_rows(st, PEER_TOPK))
    (a, ia), (b, ib) = tops
    cand = jnp.concatenate([a[i:i + 1, :] + b for i in range(PEER_TOPK)], axis=0)
    cidx = jnp.concatenate([ia[i:i + 1, :] * PEER_N_KEYS + ib for i in range(PEER_TOPK)], axis=0)
    best, idx = _topk_rows(cand, PEER_TOPK, payload=cidx)
    e = jnp.exp(best - best[0:1, :])
    gate_ref[...] = e / jnp.sum(e, axis=0, keepdims=True)
    idx_ref[...] = idx


def _route(x1, mod3, n2, wq, keys, seq, tm):
    t, d = x1.shape
    hw = wq.shape[1] // PEER_HEADS
    per_seq = seq // tm
    return pl.pallas_call(
        _route_kernel,
        out_shape=(jax.ShapeDtypeStruct((t, d), F32),
                   jax.ShapeDtypeStruct((PEER_HEADS, PEER_TOPK, t), jnp.int32),
                   jax.ShapeDtypeStruct((PEER_HEADS, PEER_TOPK, t), F32)),
        grid=(t // tm, PEER_HEADS),
        in_specs=[pl.BlockSpec((tm, d), lambda i, h: (i, 0)),
                  pl.BlockSpec((None, 1, N_MOD * d), lambda i, h: (i // per_seq, 0, 0)),
                  pl.BlockSpec((1, d), lambda i, h: (0, 0)),
                  pl.BlockSpec((d, hw), lambda i, h: (0, h)),
                  pl.BlockSpec((None, 2, PEER_N_KEYS, hw // 2), lambda i, h: (h, 0, 0, 0))],
        out_specs=(pl.BlockSpec((tm, d), lambda i, h: (i, 0)),
                   pl.BlockSpec((None, PEER_TOPK, tm), lambda i, h: (h, 0, i)),
                   pl.BlockSpec((None, PEER_TOPK, tm), lambda i, h: (h, 0, i))),
        scratch_shapes=[pltpu.VMEM((tm, d), BF16)],
        compiler_params=_cparams("parallel", "arbitrary"),
        name="route",
    )(x1, mod3, n2, wq, keys)


SC_CORES = 2
SC_SUBCORES = 16
SC_LANES = 16
SC_WORKERS = SC_CORES * SC_SUBCORES
SC_ROWS = 32
SC_TOKENS = 16


def _sc_mesh():
    return plsc.VectorSubcoreMesh(core_axis_name="c", subcore_axis_name="s")


def _sc_params():
    return dataclasses.replace(pltpu.CompilerParams(), needs_layout_passes=False)


def _sc_worker_base(tokens_per_worker):
    return (lax.axis_index("s") * SC_CORES + lax.axis_index("c")) * tokens_per_worker


def _sc_chunk_pipeline(tab_hbm, idx_v, bufs, n_chunks, compute):
    def gather(g, b):
        buf, sem = bufs[b]
        return pltpu.make_async_copy(tab_hbm.at[idx_v.at[pl.ds(g * SC_ROWS, SC_ROWS)]], buf, sem)

    gather(0, 0).start()

    @pl.loop(0, n_chunks, step=2)
    def _(g):
        gather(g, 0).wait()
        gather(g + 1, 1).start()
        compute(g, bufs[0][0])
        gather(g + 1, 1).wait()

        @pl.when(g + 2 < n_chunks)
        def _():
            gather(g + 2, 0).start()

        compute(g + 1, bufs[1][0])


def _peer_u_sc(u_tab, idx_flat, h2, k):
    t, d = h2.shape
    tpw = t // SC_WORKERS
    cpt = k // SC_ROWS
    cpb = SC_TOKENS * cpt
    half_w = d // 2
    nj = half_w // SC_LANES

    def body(u_hbm, idx_hbm, h_hbm, o_hbm, idx_v, h_v, buf0, buf1, acc_v, pre_v, sem0, sem1):
        base = _sc_worker_base(tpw)
        lanes = lax.iota(jnp.int32, SC_LANES)

        def compute(g, buf):
            tl = g // cpt
            c = g % cpt
            for half in range(2):
                hreg = [h_v[tl, pl.ds(half * half_w + j * SC_LANES, SC_LANES)] for j in range(nj)]

                @plsc.parallel_loop(0, SC_ROWS)
                def _(r):
                    parts = []
                    for a in range(4):
                        acc = buf[r, pl.ds(half * half_w + a * SC_LANES, SC_LANES)] * hreg[a]
                        for j in range(a + 4, nj, 4):
                            acc = acc + buf[r, pl.ds(half * half_w + j * SC_LANES, SC_LANES)] * hreg[j]
                        parts.append(acc)
                    tot = (parts[0] + parts[1]) + (parts[2] + parts[3])
                    if half == 0:
                        acc_v[r, :] = tot
                    else:
                        acc_v[r, :] = acc_v[r, :] + tot

            for q in range(SC_ROWS // SC_LANES):
                s = plsc.load_gather(acc_v, [lanes + q * SC_LANES, jnp.zeros((SC_LANES,), jnp.int32)])
                for l in range(1, SC_LANES):
                    s = s + plsc.load_gather(acc_v, [lanes + q * SC_LANES,
                                                     jnp.full((SC_LANES,), l, jnp.int32)])
                pre_v[tl, pl.ds(c * SC_ROWS + q * SC_LANES, SC_LANES)] = s

        @pl.loop(0, tpw // SC_TOKENS)
        def _(blk):
            tok0 = base + blk * SC_TOKENS
            pltpu.sync_copy(idx_hbm.at[pl.ds(tok0 * k, SC_TOKENS * k)], idx_v)
            pltpu.sync_copy(h_hbm.at[pl.ds(tok0, SC_TOKENS)], h_v)
            _sc_chunk_pipeline(u_hbm, idx_v, ((buf0, sem0), (buf1, sem1)), cpb, compute)
            pltpu.sync_copy(pre_v, o_hbm.at[pl.ds(tok0, SC_TOKENS)])

    return pl.kernel(
        body,
        out_type=jax.ShapeDtypeStruct((t, k), F32),
        mesh=_sc_mesh(),
        scratch_types=[pltpu.VMEM((SC_TOKENS * k,), jnp.int32),
                       pltpu.VMEM((SC_TOKENS, d), F32),
                       pltpu.VMEM((SC_ROWS, d), F32),
                       pltpu.VMEM((SC_ROWS, d), F32),
                       pltpu.VMEM((SC_ROWS, SC_LANES), F32),
                       pltpu.VMEM((SC_TOKENS, k), F32),
                       pltpu.SemaphoreType.DMA,
                       pltpu.SemaphoreType.DMA],
        compiler_params=_sc_params(),
        name="peer_u_sc",
    )(u_tab, idx_flat, h2)


def _peer_v_sc(v_tab, idx_flat, w_flat, t, k):
    d = v_tab.shape[1]
    tpw = t // SC_WORKERS
    cpt = k // SC_ROWS
    cpb = SC_TOKENS * cpt
    half_w = d // 2
    nj = half_w // SC_LANES

    def body(v_hbm, idx_hbm, w_hbm, o_hbm, idx_v, w_v, buf0, buf1, out_v, sem0, sem1):
        base = _sc_worker_base(tpw)

        def compute(g, buf):
            tl = g // cpt
            c = g % cpt
            for half in range(2):
                cols = [pl.ds(half * half_w + j * SC_LANES, SC_LANES) for j in range(nj)]

                def row(r, accs):
                    wb = plsc.load_gather(w_v, [jnp.full((SC_LANES,), g * SC_ROWS + r, jnp.int32)])
                    return tuple(accs[j] + buf[r, cols[j]] * wb for j in range(nj))

                init = tuple(jnp.where(c == 0, 0.0, out_v[tl, cols[j]]) for j in range(nj))
                accs = lax.fori_loop(0, SC_ROWS, row, init)
                for j in range(nj):
                    out_v[tl, cols[j]] = accs[j]

        @pl.loop(0, tpw // SC_TOKENS)
        def _(blk):
            tok0 = base + blk * SC_TOKENS
            pltpu.sync_copy(idx_hbm.at[pl.ds(tok0 * k, SC_TOKENS * k)], idx_v)
            pltpu.sync_copy(w_hbm.at[pl.ds(tok0 * k, SC_TOKENS * k)], w_v)
            _sc_chunk_pipeline(v_hbm, idx_v, ((buf0, sem0), (buf1, sem1)), cpb, compute)
            pltpu.sync_copy(out_v, o_hbm.at[pl.ds(tok0, SC_TOKENS)])

    return pl.kernel(
        body,
        out_type=jax.ShapeDtypeStruct((t, d), F32),
        mesh=_sc_mesh(),
        scratch_types=[pltpu.VMEM((SC_TOKENS * k,), jnp.int32),
                       pltpu.VMEM((SC_TOKENS * k,), F32),
                       pltpu.VMEM((SC_ROWS, d), F32),
                       pltpu.VMEM((SC_ROWS, d), F32),
                       pltpu.VMEM((SC_TOKENS, d), F32),
                       pltpu.SemaphoreType.DMA,
                       pltpu.SemaphoreType.DMA],
        compiler_params=_sc_params(),
        name="peer_v_sc",
    )(v_tab, idx_flat, w_flat)


def _act_kernel(gate_ref, pre_ref, o_ref):
    o_ref[...] = gate_ref[...] * _gelu_exact(pre_ref[...])


def _act(gate, pre, tile):
    t, k = gate.shape
    spec = pl.BlockSpec((tile, k), lambda i: (i, 0))
    return pl.pallas_call(
        _act_kernel,
        out_shape=jax.ShapeDtypeStruct((t, k), F32),
        grid=(t // tile,),
        in_specs=[spec, spec],
        out_specs=spec,
        compiler_params=_cparams("parallel"),
        name="expert_act",
    )(gate, pre)


def _final_kernel(x1_ref, y_ref, mod_ref, fg_ref, o_ref):
    d = x1_ref.shape[-1]
    x2 = x1_ref[...] + mod_ref[:, 5 * d:6 * d] * y_ref[...]
    o_ref[...] = _rmsnorm(x2, fg_ref[...])


def _final(x1, y, mod3, fg, seq, tile):
    t, d = x1.shape
    per_seq = seq // tile
    row = pl.BlockSpec((tile, d), lambda i: (i, 0))
    return pl.pallas_call(
        _final_kernel,
        out_shape=jax.ShapeDtypeStruct((t, d), F32),
        grid=(t // tile,),
        in_specs=[row, row,
                  pl.BlockSpec((None, 1, N_MOD * d), lambda i: (i // per_seq, 0, 0)),
                  pl.BlockSpec((1, d), lambda i: (0, 0))],
        out_specs=row,
        compiler_params=_cparams("parallel"),
        name="final",
    )(x1, y, mod3, fg)


def _peer(idx, gate, h2, x1, mod3, fg, u_tab, v_tab, seq, tile):
    t, k = idx.shape
    assert t % (SC_WORKERS * SC_TOKENS) == 0 and k % (2 * SC_ROWS) == 0
    idx_flat = idx.reshape(t * k)
    pre = _peer_u_sc(u_tab, idx_flat, h2, k)
    w = _act(gate, pre, tile)
    y = _peer_v_sc(v_tab, idx_flat, w.reshape(t * k), t, k)
    return _final(x1, y, mod3, fg, seq, tile)


def _rope_tables(length):
    rows = length // GRID_W
    row = jnp.repeat(jnp.arange(rows, dtype=F32), GRID_W)
    col = jnp.tile(jnp.arange(GRID_W, dtype=F32), rows)
    inv_freq = ROPE_BASE ** (-jnp.arange(ROPE_PAIRS, dtype=F32) / ROPE_PAIRS)
    ang_r = row[:, None] * inv_freq
    ang_c = col[:, None] * inv_freq
    cos = jnp.concatenate([jnp.cos(ang_r)] * 2 + [jnp.cos(ang_c)] * 2, axis=-1)
    sin = jnp.concatenate([-jnp.sin(ang_r), jnp.sin(ang_r), -jnp.sin(ang_c), jnp.sin(ang_c)], axis=-1)
    reps = LANES // HEAD_DIM
    return jnp.tile(cos, (1, reps)), jnp.tile(sin, (1, reps))


def _layer(x, c, ctx, c_ctx, w_mod, b_mod, n1, n2, w_in, sink, conv_w, w_attn_out, w_conv_out,
           w_mix_out, pw_q, p_keys, p_u, p_v, final_g, tm, tq, tr, tt):
    batch, seq, d = x.shape
    t = batch * seq
    aw = N_Q_HEADS * HEAD_DIM
    kw = N_KV_HEADS * HEAD_DIM

    rows = -(-(batch + 1) // 8) * 8
    cond = jnp.zeros((rows, d), F32).at[:batch].set(c).at[batch].set(c_ctx)
    mod3 = _adaln(cond, w_mod, b_mod).reshape(rows, 1, N_MOD * d)

    w_in_b = w_in.astype(BF16)
    kvc = _ctx_kv(ctx, mod3, batch, n1.reshape(1, d), w_in_b[:, aw:aw + 2 * kw])

    cos, sin = _rope_tables(seq)
    x2 = x.reshape(t, d)
    q, kv, gb, cu, ga, gv = _proj(x2, mod3, n1.reshape(1, d), w_in_b, cos, sin, seq, tm)
    x1 = _mixer(x2, q, kv, kvc, cu, gb, ga, gv, sink.reshape(1, N_Q_HEADS), conv_w,
                w_attn_out.astype(BF16), w_conv_out.astype(BF16), w_mix_out.astype(BF16),
                mod3, batch, seq, tq)

    keys = p_keys.astype(BF16)
    h2, idx_t, gate_t = _route(x1, mod3, n2.reshape(1, d), pw_q.astype(BF16), keys, seq, tr)
    nsel = PEER_HEADS * PEER_TOPK
    idx = idx_t.reshape(nsel, t).T
    gate = gate_t.reshape(nsel, t).T
    out = _peer(idx, gate, h2, x1, mod3, final_g.reshape(1, d), p_u, p_v, seq, tt)
    return out.reshape(batch, seq, d)


def kernel(x, c, ctx, c_ctx, w_mod, b_mod, norm1_g, norm2_g, w_in, attn_sink, conv_w, w_attn_out,
           w_conv_out, w_mix_out, peer_w_q, peer_sub_keys, peer_u, peer_v, final_g):
    assert w_mod.shape[0] == 1, "only the single-layer configuration is implemented"
    seq = x.shape[1]
    return _layer(x, c, ctx, c_ctx, w_mod[0], b_mod[0], norm1_g[0], norm2_g[0], w_in[0],
                  attn_sink[0], conv_w[0], w_attn_out[0], w_conv_out[0], w_mix_out[0],
                  peer_w_q[0], peer_sub_keys[0], peer_u[0], peer_v[0], final_g,
                  tm=min(512, seq), tq=min(256, seq), tr=min(256, seq), tt=min(512, seq))
```

```python
import dataclasses

import jax
import jax.numpy as jnp
from jax import lax
from jax.experimental import pallas as pl
from jax.experimental.pallas import tpu as pltpu
from jax.experimental.pallas import tpu_sc as plsc

HEAD_DIM = 64
N_Q_HEADS = 8
N_KV_HEADS = 2
Q_PER_KV = N_Q_HEADS // N_KV_HEADS
WINDOW = 128
GRID_W = 64
ROPE_BASE = 10000.0
ROPE_PAIRS = HEAD_DIM // 4
PEER_HEADS = 8
PEER_N_KEYS = 128
PEER_TOPK = 16
N_MOD = 6
EPS = 1e-6
NEG_INF = -1e30

LANES = 128
VMEM_LIMIT = 56 * 1024 * 1024

F32 = jnp.float32
BF16 = jnp.bfloat16


def _cparams(*sem):
    return pltpu.CompilerParams(dimension_semantics=sem, vmem_limit_bytes=VMEM_LIMIT)


def _rmsnorm(x, g):
    return x * lax.rsqrt(jnp.mean(x * x, axis=-1, keepdims=True) + EPS) * g


def _gelu_exact(x):
    return 0.5 * x * (1.0 + lax.erf(x * (2.0 ** -0.5)))


def _adaln_kernel(cond_ref, w_ref, b_ref, o_ref):
    act = jax.nn.silu(cond_ref[...])
    o_ref[...] = jnp.dot(act, w_ref[...], precision=lax.Precision.HIGHEST,
                         preferred_element_type=F32) + b_ref[...]


def _adaln(cond, w_mod, b_mod):
    rows, d = cond.shape
    n = w_mod.shape[1]
    tn = d
    return pl.pallas_call(
        _adaln_kernel,
        out_shape=jax.ShapeDtypeStruct((rows, n), F32),
        grid=(n // tn,),
        in_specs=[pl.BlockSpec((rows, d), lambda j: (0, 0)),
                  pl.BlockSpec((d, tn), lambda j: (0, j)),
                  pl.BlockSpec((1, tn), lambda j: (0, j))],
        out_specs=pl.BlockSpec((rows, tn), lambda j: (0, j)),
        compiler_params=_cparams("parallel"),
        name="adaln",
    )(cond, w_mod, b_mod.reshape(1, n))


def _ctx_kv_kernel(xc_ref, mod_ref, n1_ref, w_ref, o_ref):
    d = xc_ref.shape[-1]
    mod = mod_ref[...]
    hc = _rmsnorm(xc_ref[...], n1_ref[...]) * (1.0 + mod[:, d:2 * d]) + mod[:, 0:d]
    o_ref[...] = jnp.dot(hc.astype(BF16), w_ref[...], preferred_element_type=F32).astype(BF16)


def _ctx_kv(ctx, mod3, ctx_row, n1, w_kv):
    b, c, d = ctx.shape
    kvw = w_kv.shape[1]
    return pl.pallas_call(
        _ctx_kv_kernel,
        out_shape=jax.ShapeDtypeStruct((b, c, kvw), BF16),
        grid=(b,),
        in_specs=[pl.BlockSpec((None, c, d), lambda i: (i, 0, 0)),
                  pl.BlockSpec((None, 1, N_MOD * d), lambda i: (ctx_row, 0, 0)),
                  pl.BlockSpec((1, d), lambda i: (0, 0)),
                  pl.BlockSpec((d, kvw), lambda i: (0, 0))],
        out_specs=pl.BlockSpec((None, c, kvw), lambda i: (i, 0, 0)),
        compiler_params=_cparams("parallel"),
        name="ctx_kv",
    )(ctx, mod3, n1, w_kv)


def _proj_kernel(x_ref, mod_ref, n1_ref, w_ref, cos_ref, sin_ref,
                 q_ref, kv_ref, gb_ref, cu_ref, ga_ref, gv_ref):
    d = x_ref.shape[-1]
    aw = q_ref.shape[-1]
    kw = kv_ref.shape[-1] // 2
    cw = gb_ref.shape[-1]
    mod = mod_ref[...]
    h = (_rmsnorm(x_ref[...], n1_ref[...]) * (1.0 + mod[:, d:2 * d]) + mod[:, 0:d]).astype(BF16)
    cos = cos_ref[...]
    sin = sin_ref[...]
    lane = lax.broadcasted_iota(jnp.int32, cos.shape, 1)
    first_half = (lane % (2 * ROPE_PAIRS)) < ROPE_PAIRS

    def rope(z):
        partner = jnp.where(first_half, pltpu.roll(z, LANES - ROPE_PAIRS, 1),
                            pltpu.roll(z, ROPE_PAIRS, 1))
        return z * cos + partner * sin

    def proj(lo, width):
        return jnp.dot(h, w_ref[:, lo:lo + width], preferred_element_type=F32)

    off = 0
    zq = proj(off, aw)
    for g in range(aw // LANES):
        q_ref[:, g * LANES:(g + 1) * LANES] = rope(zq[:, g * LANES:(g + 1) * LANES]).astype(BF16)
    off += aw
    zkv = proj(off, 2 * kw)
    for g in range(kw // LANES):
        kv_ref[:, g * LANES:(g + 1) * LANES] = rope(zkv[:, g * LANES:(g + 1) * LANES]).astype(BF16)
    kv_ref[:, kw:] = zkv[:, kw:].astype(BF16)
    off += 2 * kw
    gb_ref[...] = proj(off, cw)
    off += cw
    zc = proj(off, cw)
    off += cw
    cu_ref[...] = zc * proj(off, cw)
    off += cw
    ga_ref[...] = proj(off, d)
    off += d
    gv_ref[...] = proj(off, d)


def _proj(x2, mod3, n1, w_in, cos, sin, seq, tm):
    t, d = x2.shape
    aw = N_Q_HEADS * HEAD_DIM
    kw = N_KV_HEADS * HEAD_DIM
    cw = d // 2
    per_seq = seq // tm
    row = lambda i: (i, 0)
    return pl.pallas_call(
        _proj_kernel,
        out_shape=(jax.ShapeDtypeStruct((t, aw), BF16),
                   jax.ShapeDtypeStruct((t, 2 * kw), BF16),
                   jax.ShapeDtypeStruct((t, cw), F32),
                   jax.ShapeDtypeStruct((t, cw), F32),
                   jax.ShapeDtypeStruct((t, d), F32),
                   jax.ShapeDtypeStruct((t, d), F32)),
        grid=(t // tm,),
        in_specs=[pl.BlockSpec((tm, d), row),
                  pl.BlockSpec((None, 1, N_MOD * d), lambda i: (i // per_seq, 0, 0)),
                  pl.BlockSpec((1, d), lambda i: (0, 0)),
                  pl.BlockSpec(w_in.shape, lambda i: (0, 0)),
                  pl.BlockSpec((tm, LANES), lambda i: (i % per_seq, 0)),
                  pl.BlockSpec((tm, LANES), lambda i: (i % per_seq, 0))],
        out_specs=(pl.BlockSpec((tm, aw), row), pl.BlockSpec((tm, 2 * kw), row),
                   pl.BlockSpec((tm, cw), row), pl.BlockSpec((tm, cw), row),
                   pl.BlockSpec((tm, d), row), pl.BlockSpec((tm, d), row)),
        compiler_params=_cparams("parallel"),
        name="proj",
    )(x2, mod3, n1, w_in, cos, sin)


def _mixer_kernel(sink_ref, x_ref, q_ref, kv_ref, kvp_ref, kvn_ref, kvc_ref,
                  cu_ref, cup_ref, cun_ref, gb_ref, ga_ref, gv_ref, convw_ref,
                  wa_ref, wc_ref, wm_ref, mod_ref, o_ref, attn_scr):
    n = pl.program_id(1)
    has_prev = n > 0
    has_next = n < pl.num_programs(1) - 1
    tq, d = x_ref.shape
    kw = N_KV_HEADS * HEAD_DIM
    scale = HEAD_DIM ** -0.5
    nt = (((1,), (1,)), ((), ()))

    kext = jnp.concatenate([kvp_ref[...], kv_ref[...], kvn_ref[...]], axis=0)
    kctx = kvc_ref[...]
    qi = lax.broadcasted_iota(jnp.int32, (WINDOW, WINDOW), 0)
    ki = lax.broadcasted_iota(jnp.int32, (WINDOW, WINDOW), 1)
    nblk = tq // WINDOW
    for j in range(nblk):
        prev_ok = ki >= qi
        next_ok = ki <= qi
        if j == 0:
            prev_ok = jnp.logical_and(prev_ok, has_prev)
        if j == nblk - 1:
            next_ok = jnp.logical_and(next_ok, has_next)
        mask = jnp.concatenate([prev_ok, jnp.ones_like(prev_ok), next_ok], axis=1)
        rows = slice(j * WINDOW, (j + 1) * WINDOW)
        krows = slice(j * WINDOW, (j + 3) * WINDOW)
        for hq in range(N_Q_HEADS):
            kvh = hq // Q_PER_KV
            kcol = slice(kvh * HEAD_DIM, (kvh + 1) * HEAD_DIM)
            vcol = slice(kw + kvh * HEAD_DIM, kw + (kvh + 1) * HEAD_DIM)
            qh = q_ref[rows, hq * HEAD_DIM:(hq + 1) * HEAD_DIM]
            s_loc = lax.dot_general(qh, kext[krows, kcol], nt, preferred_element_type=F32) * scale
            s_loc = jnp.where(mask, s_loc, NEG_INF)
            s_ctx = lax.dot_general(qh, kctx[:, kcol], nt, preferred_element_type=F32) * scale
            sink = sink_ref[0, hq]
            m = jnp.maximum(jnp.max(s_loc, axis=-1, keepdims=True),
                            jnp.max(s_ctx, axis=-1, keepdims=True))
            m = jnp.maximum(m, sink)
            p_loc = jnp.exp(s_loc - m)
            p_ctx = jnp.exp(s_ctx - m)
            den = (jnp.sum(p_loc, axis=-1, keepdims=True) + jnp.sum(p_ctx, axis=-1, keepdims=True)
                   + jnp.exp(sink - m))
            o = (jnp.dot(p_loc.astype(BF16), kext[krows, vcol], preferred_element_type=F32)
                 + jnp.dot(p_ctx.astype(BF16), kctx[:, vcol], preferred_element_type=F32))
            attn_scr[rows, hq * HEAD_DIM:(hq + 1) * HEAD_DIM] = o / den

    y_attn = jnp.dot(attn_scr[...].astype(BF16), wa_ref[...], preferred_element_type=F32)

    cu = cu_ref[...]
    ri = lax.broadcasted_iota(jnp.int32, cu.shape, 0)
    prev_row = jnp.where(has_prev, cup_ref[7:8, :], 0.0)
    next_row = jnp.where(has_next, cun_ref[0:1, :], 0.0)
    cu_m1 = jnp.where(ri == 0, prev_row, pltpu.roll(cu, 1, 0))
    cu_p1 = jnp.where(ri == tq - 1, next_row, pltpu.roll(cu, tq - 1, 0))
    cw = convw_ref[...]
    conv = cu_m1 * cw[0:1, :] + cu * cw[1:2, :] + cu_p1 * cw[2:3, :]
    y_conv = jnp.dot((gb_ref[...] * conv).astype(BF16), wc_ref[...], preferred_element_type=F32)

    merged = jax.nn.sigmoid(ga_ref[...]) * y_attn + jax.nn.sigmoid(gv_ref[...]) * y_conv
    y = jnp.dot(merged.astype(BF16), wm_ref[...], preferred_element_type=F32)
    g1 = mod_ref[:, 2 * d:3 * d]
    o_ref[...] = x_ref[...] + g1 * y


def _mixer(x2, q, kv, kvc, cu, gb, ga, gv, sink, conv_w, wa, wc, wm, mod3, batch, seq, tq):
    t, d = x2.shape
    aw = q.shape[1]
    kv2 = kv.shape[1]
    cw = cu.shape[1]
    c = kvc.shape[1]
    nq = seq // tq
    nb = seq // WINDOW
    sub = tq // WINDOW
    kv3 = kv.reshape(t // WINDOW, WINDOW, kv2)
    cu3 = cu.reshape(t // 8, 8, cw)
    tile = lambda b, n: (b * nq + n, 0)
    const = lambda b, n: (0, 0)
    return pl.pallas_call(
        _mixer_kernel,
        out_shape=jax.ShapeDtypeStruct((t, d), F32),
        grid=(batch, nq),
        in_specs=[
            pl.BlockSpec(memory_space=pltpu.SMEM),
            pl.BlockSpec((tq, d), tile),
            pl.BlockSpec((tq, aw), tile),
            pl.BlockSpec((tq, kv2), tile),
            pl.BlockSpec((None, WINDOW, kv2), lambda b, n: (b * nb + jnp.maximum(n * sub - 1, 0), 0, 0)),
            pl.BlockSpec((None, WINDOW, kv2), lambda b, n: (b * nb + jnp.minimum((n + 1) * sub, nb - 1), 0, 0)),
            pl.BlockSpec((None, c, kv2), lambda b, n: (b, 0, 0)),
            pl.BlockSpec((tq, cw), tile),
            pl.BlockSpec((None, 8, cw), lambda b, n: (jnp.maximum((b * seq + n * tq) // 8 - 1, 0), 0, 0)),
            pl.BlockSpec((None, 8, cw), lambda b, n: (jnp.minimum((b * seq + (n + 1) * tq) // 8, t // 8 - 1), 0, 0)),
            pl.BlockSpec((tq, cw), tile),
            pl.BlockSpec((tq, d), tile),
            pl.BlockSpec((tq, d), tile),
            pl.BlockSpec(conv_w.shape, const),
            pl.BlockSpec(wa.shape, const),
            pl.BlockSpec(wc.shape, const),
            pl.BlockSpec(wm.shape, const),
            pl.BlockSpec((None, 1, N_MOD * d), lambda b, n: (b, 0, 0)),
        ],
        out_specs=pl.BlockSpec((tq, d), tile),
        scratch_shapes=[pltpu.VMEM((tq, aw), F32)],
        compiler_params=_cparams("parallel", "parallel"),
        name="mixer",
    )(sink, x2, q, kv, kv3, kv3, kvc, cu, cu3, cu3, gb, ga, gv, conv_w, wa, wc, wm, mod3)


def _topk_rows(s, k, payload=None):
    n = s.shape[0]
    iota = lax.broadcasted_iota(jnp.int32, s.shape, 0)
    vals, ids = [], []
    for _ in range(k):
        m = jnp.max(s, axis=0, keepdims=True)
        am = jnp.min(jnp.where(s == m, iota, n), axis=0, keepdims=True)
        hit = iota == am
        vals.append(m)
        if payload is None:
            ids.append(am)
        else:
            ids.append(jnp.max(jnp.where(hit, payload, -1), axis=0, keepdims=True))
        s = jnp.where(hit, -jnp.inf, s)
    return jnp.concatenate(vals, axis=0), jnp.concatenate(ids, axis=0)


def _route_kernel(x1_ref, mod_ref, n2_ref, wq_ref, keys_ref, h2_ref, idx_ref, gate_ref, h2b_scr):
    d = x1_ref.shape[-1]
    nt = (((1,), (1,)), ((), ()))

    @pl.when(pl.program_id(1) == 0)
    def _():
        mod = mod_ref[...]
        h2 = _rmsnorm(x1_ref[...], n2_ref[...]) * (1.0 + mod[:, 4 * d:5 * d]) + mod[:, 3 * d:4 * d]
        h2_ref[...] = h2
        h2b_scr[...] = h2.astype(BF16)

    qp = jnp.dot(h2b_scr[...], wq_ref[...], preferred_element_type=F32)
    half = qp.shape[1] // 2
    tops = []
    for p in range(2):
        qh = qp[:, p * half:(p + 1) * half].astype(BF16)
        st = lax.dot_general(keys_ref[p], qh, nt, preferred_element_type=F32)
        tops.append(_topk_rows(st, PEER_TOPK))
    (a, ia), (b, ib) = tops
    cand = jnp.concatenate([a[i:i + 1, :] + b for i in range(PEER_TOPK)], axis=0)
    cidx = jnp.concatenate([ia[i:i + 1, :] * PEER_N_KEYS + ib for i in range(PEER_TOPK)], axis=0)
    best, idx = _topk_rows(cand, PEER_TOPK, payload=cidx)
    e = jnp.exp(best - best[0:1, :])
    gate_ref[...] = e / jnp.sum(e, axis=0, keepdims=True)
    idx_ref[...] = idx


def _route(x1, mod3, n2, wq, keys, seq, tm):
    t, d = x1.shape
    hw = wq.shape[1] // PEER_HEADS
    per_seq = seq // tm
    return pl.pallas_call(
        _route_kernel,
        out_shape=(jax.ShapeDtypeStruct((t, d), F32),
                   jax.ShapeDtypeStruct((PEER_HEADS, PEER_TOPK, t), jnp.int32),
                   jax.ShapeDtypeStruct((PEER_HEADS, PEER_TOPK, t), F32)),
        grid=(t // tm, PEER_HEADS),
        in_specs=[pl.BlockSpec((tm, d), lambda i, h: (i, 0)),
                  pl.BlockSpec((None, 1, N_MOD * d), lambda i, h: (i // per_seq, 0, 0)),
                  pl.BlockSpec((1, d), lambda i, h: (0, 0)),
                  pl.BlockSpec((d, hw), lambda i, h: (0, h)),
                  pl.BlockSpec((None, 2, PEER_N_KEYS, hw // 2), lambda i, h: (h, 0, 0, 0))],
        out_specs=(pl.BlockSpec((tm, d), lambda i, h: (i, 0)),
                   pl.BlockSpec((None, PEER_TOPK, tm), lambda i, h: (h, 0, i)),
                   pl.BlockSpec((None, PEER_TOPK, tm), lambda i, h: (h, 0, i))),
        scratch_shapes=[pltpu.VMEM((tm, d), BF16)],
        compiler_params=_cparams("parallel", "arbitrary"),
        name="route",
    )(x1, mod3, n2, wq, keys)


SC_CORES = 2
SC_SUBCORES = 16
SC_LANES = 16
SC_WORKERS = SC_CORES * SC_SUBCORES
SC_ROWS = 32
SC_TOKENS = 16


def _sc_mesh():
    return plsc.VectorSubcoreMesh(core_axis_name="c", subcore_axis_name="s")


def _sc_params():
    return dataclasses.replace(pltpu.CompilerParams(), needs_layout_passes=False)


def _sc_worker_base(tokens_per_worker):
    return (lax.axis_index("s") * SC_CORES + lax.axis_index("c")) * tokens_per_worker


def _sc_chunk_pipeline(tab_hbm, idx_v, bufs, n_chunks, compute):
    def gather(g, b):
        buf, sem = bufs[b]
        return pltpu.make_async_copy(tab_hbm.at[idx_v.at[pl.ds(g * SC_ROWS, SC_ROWS)]], buf, sem)

    gather(0, 0).start()

    @pl.loop(0, n_chunks, step=2)
    def _(g):
        gather(g, 0).wait()
        gather(g + 1, 1).start()
        compute(g, bufs[0][0])
        gather(g + 1, 1).wait()

        @pl.when(g + 2 < n_chunks)
        def _():
            gather(g + 2, 0).start()

        compute(g + 1, bufs[1][0])


def _pack_table(tab):
    half = tab.shape[1] // 2
    lo = lax.bitcast_convert_type(tab[:, :half].astype(BF16), jnp.uint16).astype(jnp.uint32)
    hi = lax.bitcast_convert_type(tab[:, half:].astype(BF16), jnp.uint16).astype(jnp.uint32)
    return lax.bitcast_convert_type((hi << 16) | lo, jnp.int32)


def _unpack_lo(x):
    return lax.bitcast_convert_type(x << 16, F32)


def _unpack_hi(x):
    return lax.bitcast_convert_type(x & jnp.int32(-65536), F32)


def _peer_u_sc(u_pk, idx_flat, h2, k):
    t, d = h2.shape
    words = u_pk.shape[1]
    tpw = t // SC_WORKERS
    cpt = k // SC_ROWS
    cpb = SC_TOKENS * cpt
    pw = words // 2
    nj = pw // SC_LANES

    def body(u_hbm, idx_hbm, h_hbm, o_hbm, idx_v, h_v, buf0, buf1, acc_v, pre_v, sem0, sem1):
        base = _sc_worker_base(tpw)
        lanes = lax.iota(jnp.int32, SC_LANES)

        def compute(g, buf):
            tl = g // cpt
            c = g % cpt
            for p in range(2):
                h_lo = [h_v[tl, pl.ds(p * pw + j * SC_LANES, SC_LANES)] for j in range(nj)]
                h_hi = [h_v[tl, pl.ds(words + p * pw + j * SC_LANES, SC_LANES)] for j in range(nj)]

                @plsc.parallel_loop(0, SC_ROWS)
                def _(r):
                    parts = [None] * 4
                    for j in range(nj):
                        x = buf[r, pl.ds(p * pw + j * SC_LANES, SC_LANES)]
                        term = _unpack_lo(x) * h_lo[j] + _unpack_hi(x) * h_hi[j]
                        parts[j % 4] = term if parts[j % 4] is None else parts[j % 4] + term
                    tot = (parts[0] + parts[1]) + (parts[2] + parts[3])
                    if p == 0:
                        acc_v[r, :] = tot
                    else:
                        acc_v[r, :] = acc_v[r, :] + tot

            for q in range(SC_ROWS // SC_LANES):
                s = plsc.load_gather(acc_v, [lanes + q * SC_LANES, jnp.zeros((SC_LANES,), jnp.int32)])
                for l in range(1, SC_LANES):
                    s = s + plsc.load_gather(acc_v, [lanes + q * SC_LANES,
                                                     jnp.full((SC_LANES,), l, jnp.int32)])
                pre_v[tl, pl.ds(c * SC_ROWS + q * SC_LANES, SC_LANES)] = s

        @pl.loop(0, tpw // SC_TOKENS)
        def _(blk):
            tok0 = base + blk * SC_TOKENS
            pltpu.sync_copy(idx_hbm.at[pl.ds(tok0 * k, SC_TOKENS * k)], idx_v)
            pltpu.sync_copy(h_hbm.at[pl.ds(tok0, SC_TOKENS)], h_v)
            _sc_chunk_pipeline(u_hbm, idx_v, ((buf0, sem0), (buf1, sem1)), cpb, compute)
            pltpu.sync_copy(pre_v, o_hbm.at[pl.ds(tok0, SC_TOKENS)])

    return pl.kernel(
        body,
        out_type=jax.ShapeDtypeStruct((t, k), F32),
        mesh=_sc_mesh(),
        scratch_types=[pltpu.VMEM((SC_TOKENS * k,), jnp.int32),
                       pltpu.VMEM((SC_TOKENS, d), F32),
                       pltpu.VMEM((SC_ROWS, words), jnp.int32),
                       pltpu.VMEM((SC_ROWS, words), jnp.int32),
                       pltpu.VMEM((SC_ROWS, SC_LANES), F32),
                       pltpu.VMEM((SC_TOKENS, k), F32),
                       pltpu.SemaphoreType.DMA,
                       pltpu.SemaphoreType.DMA],
        compiler_params=_sc_params(),
        name="peer_u_sc",
    )(u_pk, idx_flat, h2)


def _peer_v_sc(v_pk, idx_flat, w_flat, t, k):
    words = v_pk.shape[1]
    d = 2 * words
    tpw = t // SC_WORKERS
    cpt = k // SC_ROWS
    cpb = SC_TOKENS * cpt
    pw = words // 2
    nj = pw // SC_LANES

    def body(v_hbm, idx_hbm, w_hbm, o_hbm, idx_v, w_v, buf0, buf1, out_v, sem0, sem1):
        base = _sc_worker_base(tpw)

        def compute(g, buf):
            tl = g // cpt
            c = g % cpt
            for p in range(2):
                lo_cols = [pl.ds(p * pw + j * SC_LANES, SC_LANES) for j in range(nj)]
                hi_cols = [pl.ds(words + p * pw + j * SC_LANES, SC_LANES) for j in range(nj)]

                def row(r, accs):
                    wb = plsc.load_gather(w_v, [jnp.full((SC_LANES,), g * SC_ROWS + r, jnp.int32)])
                    new = []
                    for j in range(nj):
                        x = buf[r, lo_cols[j]]
                        new.append(accs[2 * j] + _unpack_lo(x) * wb)
                        new.append(accs[2 * j + 1] + _unpack_hi(x) * wb)
                    return tuple(new)

                init = []
                for j in range(nj):
                    init.append(jnp.where(c == 0, 0.0, out_v[tl, lo_cols[j]]))
                    init.append(jnp.where(c == 0, 0.0, out_v[tl, hi_cols[j]]))
                accs = lax.fori_loop(0, SC_ROWS, row, tuple(init))
                for j in range(nj):
                    out_v[tl, lo_cols[j]] = accs[2 * j]
                    out_v[tl, hi_cols[j]] = accs[2 * j + 1]

        @pl.loop(0, tpw // SC_TOKENS)
        def _(blk):
            tok0 = base + blk * SC_TOKENS
            pltpu.sync_copy(idx_hbm.at[pl.ds(tok0 * k, SC_TOKENS * k)], idx_v)
            pltpu.sync_copy(w_hbm.at[pl.ds(tok0 * k, SC_TOKENS * k)], w_v)
            _sc_chunk_pipeline(v_hbm, idx_v, ((buf0, sem0), (buf1, sem1)), cpb, compute)
            pltpu.sync_copy(out_v, o_hbm.at[pl.ds(tok0, SC_TOKENS)])

    return pl.kernel(
        body,
        out_type=jax.ShapeDtypeStruct((t, d), F32),
        mesh=_sc_mesh(),
        scratch_types=[pltpu.VMEM((SC_TOKENS * k,), jnp.int32),
                       pltpu.VMEM((SC_TOKENS * k,), F32),
                       pltpu.VMEM((SC_ROWS, words), jnp.int32),
                       pltpu.VMEM((SC_ROWS, words), jnp.int32),
                       pltpu.VMEM((SC_TOKENS, d), F32),
                       pltpu.SemaphoreType.DMA,
                       pltpu.SemaphoreType.DMA],
        compiler_params=_sc_params(),
        name="peer_v_sc",
    )(v_pk, idx_flat, w_flat)


def _act_kernel(gate_ref, pre_ref, o_ref):
    o_ref[...] = gate_ref[...] * _gelu_exact(pre_ref[...])


def _act(gate, pre, tile):
    t, k = gate.shape
    spec = pl.BlockSpec((tile, k), lambda i: (i, 0))
    return pl.pallas_call(
        _act_kernel,
        out_shape=jax.ShapeDtypeStruct((t, k), F32),
        grid=(t // tile,),
        in_specs=[spec, spec],
        out_specs=spec,
        compiler_params=_cparams("parallel"),
        name="expert_act",
    )(gate, pre)


def _final_kernel(x1_ref, y_ref, mod_ref, fg_ref, o_ref):
    d = x1_ref.shape[-1]
    x2 = x1_ref[...] + mod_ref[:, 5 * d:6 * d] * y_ref[...]
    o_ref[...] = _rmsnorm(x2, fg_ref[...])


def _final(x1, y, mod3, fg, seq, tile):
    t, d = x1.shape
    per_seq = seq // tile
    row = pl.BlockSpec((tile, d), lambda i: (i, 0))
    return pl.pallas_call(
        _final_kernel,
        out_shape=jax.ShapeDtypeStruct((t, d), F32),
        grid=(t // tile,),
        in_specs=[row, row,
                  pl.BlockSpec((None, 1, N_MOD * d), lambda i: (i // per_seq, 0, 0)),
                  pl.BlockSpec((1, d), lambda i: (0, 0))],
        out_specs=row,
        compiler_params=_cparams("parallel"),
        name="final",
    )(x1, y, mod3, fg)


def _peer(idx, gate, h2, x1, mod3, fg, u_tab, v_tab, seq, tile):
    t, k = idx.shape
    assert t % (SC_WORKERS * SC_TOKENS) == 0 and k % (2 * SC_ROWS) == 0
    idx_flat = idx.reshape(t * k)
    pre = _peer_u_sc(_pack_table(u_tab), idx_flat, h2, k)
    w = _act(gate, pre, tile)
    y = _peer_v_sc(_pack_table(v_tab), idx_flat, w.reshape(t * k), t, k)
    return _final(x1, y, mod3, fg, seq, tile)


def _rope_tables(length):
    rows = length // GRID_W
    row = jnp.repeat(jnp.arange(rows, dtype=F32), GRID_W)
    col = jnp.tile(jnp.arange(GRID_W, dtype=F32), rows)
    inv_freq = ROPE_BASE ** (-jnp.arange(ROPE_PAIRS, dtype=F32) / ROPE_PAIRS)
    ang_r = row[:, None] * inv_freq
    ang_c = col[:, None] * inv_freq
    cos = jnp.concatenate([jnp.cos(ang_r)] * 2 + [jnp.cos(ang_c)] * 2, axis=-1)
    sin = jnp.concatenate([-jnp.sin(ang_r), jnp.sin(ang_r), -jnp.sin(ang_c), jnp.sin(ang_c)], axis=-1)
    reps = LANES // HEAD_DIM
    return jnp.tile(cos, (1, reps)), jnp.tile(sin, (1, reps))


def _layer(x, c, ctx, c_ctx, w_mod, b_mod, n1, n2, w_in, sink, conv_w, w_attn_out, w_conv_out,
           w_mix_out, pw_q, p_keys, p_u, p_v, final_g, tm, tq, tr, tt):
    batch, seq, d = x.shape
    t = batch * seq
    aw = N_Q_HEADS * HEAD_DIM
    kw = N_KV_HEADS * HEAD_DIM

    rows = -(-(batch + 1) // 8) * 8
    cond = jnp.zeros((rows, d), F32).at[:batch].set(c).at[batch].set(c_ctx)
    mod3 = _adaln(cond, w_mod, b_mod).reshape(rows, 1, N_MOD * d)

    w_in_b = w_in.astype(BF16)
    kvc = _ctx_kv(ctx, mod3, batch, n1.reshape(1, d), w_in_b[:, aw:aw + 2 * kw])

    cos, sin = _rope_tables(seq)
    x2 = x.reshape(t, d)
    q, kv, gb, cu, ga, gv = _proj(x2, mod3, n1.reshape(1, d), w_in_b, cos, sin, seq, tm)
    x1 = _mixer(x2, q, kv, kvc, cu, gb, ga, gv, sink.reshape(1, N_Q_HEADS), conv_w,
                w_attn_out.astype(BF16), w_conv_out.astype(BF16), w_mix_out.astype(BF16),
                mod3, batch, seq, tq)

    keys = p_keys.astype(BF16)
    h2, idx_t, gate_t = _route(x1, mod3, n2.reshape(1, d), pw_q.astype(BF16), keys, seq, tr)
    nsel = PEER_HEADS * PEER_TOPK
    idx = idx_t.reshape(nsel, t).T
    gate = gate_t.reshape(nsel, t).T
    out = _peer(idx, gate, h2, x1, mod3, final_g.reshape(1, d), p_u, p_v, seq, tt)
    return out.reshape(batch, seq, d)


def kernel(x, c, ctx, c_ctx, w_mod, b_mod, norm1_g, norm2_g, w_in, attn_sink, conv_w, w_attn_out,
           w_conv_out, w_mix_out, peer_w_q, peer_sub_keys, peer_u, peer_v, final_g):
    assert w_mod.shape[0] == 1, "only the single-layer configuration is implemented"
    seq = x.shape[1]
    return _layer(x, c, ctx, c_ctx, w_mod[0], b_mod[0], norm1_g[0], norm2_g[0], w_in[0],
                  attn_sink[0], conv_w[0], w_attn_out[0], w_conv_out[0], w_mix_out[0],
                  peer_w_q[0], peer_sub_keys[0], peer_u[0], peer_v[0], final_g,
                  tm=min(512, seq), tq=min(256, seq), tr=min(256, seq), tt=min(512, seq))
```

```python
import dataclasses

import jax
import jax.numpy as jnp
from jax import lax
from jax.experimental import pallas as pl
from jax.experimental.pallas import tpu as pltpu
from jax.experimental.pallas import tpu_sc as plsc

HEAD_DIM = 64
N_Q_HEADS = 8
N_KV_HEADS = 2
Q_PER_KV = N_Q_HEADS // N_KV_HEADS
WINDOW = 128
GRID_W = 64
ROPE_BASE = 10000.0
ROPE_PAIRS = HEAD_DIM // 4
PEER_HEADS = 8
PEER_N_KEYS = 128
PEER_TOPK = 16
N_MOD = 6
EPS = 1e-6
NEG_INF = -1e30

LANES = 128
VMEM_LIMIT = 56 * 1024 * 1024

F32 = jnp.float32
BF16 = jnp.bfloat16


def _cparams(*sem):
    return pltpu.CompilerParams(dimension_semantics=sem, vmem_limit_bytes=VMEM_LIMIT)


def _rmsnorm(x, g):
    return x * lax.rsqrt(jnp.mean(x * x, axis=-1, keepdims=True) + EPS) * g


def _gelu_exact(x):
    return 0.5 * x * (1.0 + lax.erf(x * (2.0 ** -0.5)))


def _adaln_kernel(cond_ref, w_ref, b_ref, o_ref):
    act = jax.nn.silu(cond_ref[...])
    o_ref[...] = jnp.dot(act, w_ref[...], precision=lax.Precision.HIGHEST,
                         preferred_element_type=F32) + b_ref[...]


def _adaln(cond, w_mod, b_mod):
    rows, d = cond.shape
    n = w_mod.shape[1]
    tn = d
    return pl.pallas_call(
        _adaln_kernel,
        out_shape=jax.ShapeDtypeStruct((rows, n), F32),
        grid=(n // tn,),
        in_specs=[pl.BlockSpec((rows, d), lambda j: (0, 0)),
                  pl.BlockSpec((d, tn), lambda j: (0, j)),
                  pl.BlockSpec((1, tn), lambda j: (0, j))],
        out_specs=pl.BlockSpec((rows, tn), lambda j: (0, j)),
        compiler_params=_cparams("parallel"),
        name="adaln",
    )(cond, w_mod, b_mod.reshape(1, n))


def _ctx_kv_kernel(xc_ref, mod_ref, n1_ref, w_ref, o_ref):
    d = xc_ref.shape[-1]
    mod = mod_ref[...]
    hc = _rmsnorm(xc_ref[...], n1_ref[...]) * (1.0 + mod[:, d:2 * d]) + mod[:, 0:d]
    o_ref[...] = jnp.dot(hc.astype(BF16), w_ref[...], preferred_element_type=F32).astype(BF16)


def _ctx_kv(ctx, mod3, ctx_row, n1, w_kv):
    b, c, d = ctx.shape
    kvw = w_kv.shape[1]
    return pl.pallas_call(
        _ctx_kv_kernel,
        out_shape=jax.ShapeDtypeStruct((b, c, kvw), BF16),
        grid=(b,),
        in_specs=[pl.BlockSpec((None, c, d), lambda i: (i, 0, 0)),
                  pl.BlockSpec((None, 1, N_MOD * d), lambda i: (ctx_row, 0, 0)),
                  pl.BlockSpec((1, d), lambda i: (0, 0)),
                  pl.BlockSpec((d, kvw), lambda i: (0, 0))],
        out_specs=pl.BlockSpec((None, c, kvw), lambda i: (i, 0, 0)),
        compiler_params=_cparams("parallel"),
        name="ctx_kv",
    )(ctx, mod3, n1, w_kv)


def _proj_kernel(x_ref, mod_ref, n1_ref, w_ref, cos_ref, sin_ref,
                 q_ref, kv_ref, gb_ref, cu_ref, ga_ref, gv_ref):
    d = x_ref.shape[-1]
    aw = q_ref.shape[-1]
    kw = kv_ref.shape[-1] // 2
    cw = gb_ref.shape[-1]
    mod = mod_ref[...]
    h = (_rmsnorm(x_ref[...], n1_ref[...]) * (1.0 + mod[:, d:2 * d]) + mod[:, 0:d]).astype(BF16)
    cos = cos_ref[...]
    sin = sin_ref[...]
    lane = lax.broadcasted_iota(jnp.int32, cos.shape, 1)
    first_half = (lane % (2 * ROPE_PAIRS)) < ROPE_PAIRS

    def rope(z):
        partner = jnp.where(first_half, pltpu.roll(z, LANES - ROPE_PAIRS, 1),
                            pltpu.roll(z, ROPE_PAIRS, 1))
        return z * cos + partner * sin

    def proj(lo, width):
        return jnp.dot(h, w_ref[:, lo:lo + width], preferred_element_type=F32)

    off = 0
    zq = proj(off, aw)
    for g in range(aw // LANES):
        q_ref[:, g * LANES:(g + 1) * LANES] = rope(zq[:, g * LANES:(g + 1) * LANES]).astype(BF16)
    off += aw
    zkv = proj(off, 2 * kw)
    for g in range(kw // LANES):
        kv_ref[:, g * LANES:(g + 1) * LANES] = rope(zkv[:, g * LANES:(g + 1) * LANES]).astype(BF16)
    kv_ref[:, kw:] = zkv[:, kw:].astype(BF16)
    off += 2 * kw
    gb_ref[...] = proj(off, cw)
    off += cw
    zc = proj(off, cw)
    off += cw
    cu_ref[...] = zc * proj(off, cw)
    off += cw
    ga_ref[...] = proj(off, d)
    off += d
    gv_ref[...] = proj(off, d)


def _proj(x2, mod3, n1, w_in, cos, sin, seq, tm, b0, nb):
    d = x2.shape[1]
    t = nb * seq
    aw = N_Q_HEADS * HEAD_DIM
    kw = N_KV_HEADS * HEAD_DIM
    cw = d // 2
    per_seq = seq // tm
    row = lambda i: (i, 0)
    return pl.pallas_call(
        _proj_kernel,
        out_shape=(jax.ShapeDtypeStruct((t, aw), BF16),
                   jax.ShapeDtypeStruct((t, 2 * kw), BF16),
                   jax.ShapeDtypeStruct((t, cw), F32),
                   jax.ShapeDtypeStruct((t, cw), F32),
                   jax.ShapeDtypeStruct((t, d), F32),
                   jax.ShapeDtypeStruct((t, d), F32)),
        grid=(t // tm,),
        in_specs=[pl.BlockSpec((tm, d), lambda i: (b0 * per_seq + i, 0)),
                  pl.BlockSpec((None, 1, N_MOD * d), lambda i: (b0 + i // per_seq, 0, 0)),
                  pl.BlockSpec((1, d), lambda i: (0, 0)),
                  pl.BlockSpec(w_in.shape, lambda i: (0, 0)),
                  pl.BlockSpec((tm, LANES), lambda i: (i % per_seq, 0)),
                  pl.BlockSpec((tm, LANES), lambda i: (i % per_seq, 0))],
        out_specs=(pl.BlockSpec((tm, aw), row), pl.BlockSpec((tm, 2 * kw), row),
                   pl.BlockSpec((tm, cw), row), pl.BlockSpec((tm, cw), row),
                   pl.BlockSpec((tm, d), row), pl.BlockSpec((tm, d), row)),
        compiler_params=_cparams("parallel"),
        name="proj",
    )(x2, mod3, n1, w_in, cos, sin)


def _mixer_kernel(sink_ref, x_ref, q_ref, kv_ref, kvp_ref, kvn_ref, kvc_ref,
                  cu_ref, cup_ref, cun_ref, gb_ref, ga_ref, gv_ref, convw_ref,
                  wa_ref, wc_ref, wm_ref, mod_ref, o_ref, attn_scr):
    n = pl.program_id(1)
    has_prev = n > 0
    has_next = n < pl.num_programs(1) - 1
    tq, d = x_ref.shape
    kw = N_KV_HEADS * HEAD_DIM
    scale = HEAD_DIM ** -0.5
    nt = (((1,), (1,)), ((), ()))

    kext = jnp.concatenate([kvp_ref[...], kv_ref[...], kvn_ref[...]], axis=0)
    kctx = kvc_ref[...]
    qi = lax.broadcasted_iota(jnp.int32, (WINDOW, WINDOW), 0)
    ki = lax.broadcasted_iota(jnp.int32, (WINDOW, WINDOW), 1)
    nblk = tq // WINDOW
    for j in range(nblk):
        prev_ok = ki >= qi
        next_ok = ki <= qi
        if j == 0:
            prev_ok = jnp.logical_and(prev_ok, has_prev)
        if j == nblk - 1:
            next_ok = jnp.logical_and(next_ok, has_next)
        mask = jnp.concatenate([prev_ok, jnp.ones_like(prev_ok), next_ok], axis=1)
        rows = slice(j * WINDOW, (j + 1) * WINDOW)
        krows = slice(j * WINDOW, (j + 3) * WINDOW)
        for hq in range(N_Q_HEADS):
            kvh = hq // Q_PER_KV
            kcol = slice(kvh * HEAD_DIM, (kvh + 1) * HEAD_DIM)
            vcol = slice(kw + kvh * HEAD_DIM, kw + (kvh + 1) * HEAD_DIM)
            qh = q_ref[rows, hq * HEAD_DIM:(hq + 1) * HEAD_DIM]
            s_loc = lax.dot_general(qh, kext[krows, kcol], nt, preferred_element_type=F32) * scale
            s_loc = jnp.where(mask, s_loc, NEG_INF)
            s_ctx = lax.dot_general(qh, kctx[:, kcol], nt, preferred_element_type=F32) * scale
            sink = sink_ref[0, hq]
            m = jnp.maximum(jnp.max(s_loc, axis=-1, keepdims=True),
                            jnp.max(s_ctx, axis=-1, keepdims=True))
            m = jnp.maximum(m, sink)
            p_loc = jnp.exp(s_loc - m)
            p_ctx = jnp.exp(s_ctx - m)
            den = (jnp.sum(p_loc, axis=-1, keepdims=True) + jnp.sum(p_ctx, axis=-1, keepdims=True)
                   + jnp.exp(sink - m))
            o = (jnp.dot(p_loc.astype(BF16), kext[krows, vcol], preferred_element_type=F32)
                 + jnp.dot(p_ctx.astype(BF16), kctx[:, vcol], preferred_element_type=F32))
            attn_scr[rows, hq * HEAD_DIM:(hq + 1) * HEAD_DIM] = o / den

    y_attn = jnp.dot(attn_scr[...].astype(BF16), wa_ref[...], preferred_element_type=F32)

    cu = cu_ref[...]
    ri = lax.broadcasted_iota(jnp.int32, cu.shape, 0)
    prev_row = jnp.where(has_prev, cup_ref[7:8, :], 0.0)
    next_row = jnp.where(has_next, cun_ref[0:1, :], 0.0)
    cu_m1 = jnp.where(ri == 0, prev_row, pltpu.roll(cu, 1, 0))
    cu_p1 = jnp.where(ri == tq - 1, next_row, pltpu.roll(cu, tq - 1, 0))
    cw = convw_ref[...]
    conv = cu_m1 * cw[0:1, :] + cu * cw[1:2, :] + cu_p1 * cw[2:3, :]
    y_conv = jnp.dot((gb_ref[...] * conv).astype(BF16), wc_ref[...], preferred_element_type=F32)

    merged = jax.nn.sigmoid(ga_ref[...]) * y_attn + jax.nn.sigmoid(gv_ref[...]) * y_conv
    y = jnp.dot(merged.astype(BF16), wm_ref[...], preferred_element_type=F32)
    g1 = mod_ref[:, 2 * d:3 * d]
    o_ref[...] = x_ref[...] + g1 * y


def _mixer(x2, q, kv, kvc, cu, gb, ga, gv, sink, conv_w, wa, wc, wm, mod3, seq, tq, b0, batch):
    d = x2.shape[1]
    t, aw = q.shape
    kv2 = kv.shape[1]
    cw = cu.shape[1]
    c = kvc.shape[1]
    nq = seq // tq
    nb = seq // WINDOW
    sub = tq // WINDOW
    kv3 = kv.reshape(t // WINDOW, WINDOW, kv2)
    cu3 = cu.reshape(t // 8, 8, cw)
    tile = lambda b, n: (b * nq + n, 0)
    const = lambda b, n: (0, 0)
    return pl.pallas_call(
        _mixer_kernel,
        out_shape=jax.ShapeDtypeStruct((t, d), F32),
        grid=(batch, nq),
        in_specs=[
            pl.BlockSpec(memory_space=pltpu.SMEM),
            pl.BlockSpec((tq, d), lambda b, n: ((b0 + b) * nq + n, 0)),
            pl.BlockSpec((tq, aw), tile),
            pl.BlockSpec((tq, kv2), tile),
            pl.BlockSpec((None, WINDOW, kv2), lambda b, n: (b * nb + jnp.maximum(n * sub - 1, 0), 0, 0)),
            pl.BlockSpec((None, WINDOW, kv2), lambda b, n: (b * nb + jnp.minimum((n + 1) * sub, nb - 1), 0, 0)),
            pl.BlockSpec((None, c, kv2), lambda b, n: (b0 + b, 0, 0)),
            pl.BlockSpec((tq, cw), tile),
            pl.BlockSpec((None, 8, cw), lambda b, n: (jnp.maximum((b * seq + n * tq) // 8 - 1, 0), 0, 0)),
            pl.BlockSpec((None, 8, cw), lambda b, n: (jnp.minimum((b * seq + (n + 1) * tq) // 8, t // 8 - 1), 0, 0)),
            pl.BlockSpec((tq, cw), tile),
            pl.BlockSpec((tq, d), tile),
            pl.BlockSpec((tq, d), tile),
            pl.BlockSpec(conv_w.shape, const),
            pl.BlockSpec(wa.shape, const),
            pl.BlockSpec(wc.shape, const),
            pl.BlockSpec(wm.shape, const),
            pl.BlockSpec((None, 1, N_MOD * d), lambda b, n: (b0 + b, 0, 0)),
        ],
        out_specs=pl.BlockSpec((tq, d), tile),
        scratch_shapes=[pltpu.VMEM((tq, aw), F32)],
        compiler_params=_cparams("parallel", "parallel"),
        name="mixer",
    )(sink, x2, q, kv, kv3, kv3, kvc, cu, cu3, cu3, gb, ga, gv, conv_w, wa, wc, wm, mod3)


def _topk_rows(s, k, payload=None):
    n = s.shape[0]
    iota = lax.broadcasted_iota(jnp.int32, s.shape, 0)
    vals, ids = [], []
    for _ in range(k):
        m = jnp.max(s, axis=0, keepdims=True)
        am = jnp.min(jnp.where(s == m, iota, n), axis=0, keepdims=True)
        hit = iota == am
        vals.append(m)
        if payload is None:
            ids.append(am)
        else:
            ids.append(jnp.max(jnp.where(hit, payload, -1), axis=0, keepdims=True))
        s = jnp.where(hit, -jnp.inf, s)
    return jnp.concatenate(vals, axis=0), jnp.concatenate(ids, axis=0)


def _route_kernel(x1_ref, mod_ref, n2_ref, wq_ref, keys_ref, h2_ref, idx_ref, gate_ref, h2b_scr):
    d = x1_ref.shape[-1]
    nt = (((1,), (1,)), ((), ()))

    @pl.when(pl.program_id(1) == 0)
    def _():
        mod = mod_ref[...]
        h2 = _rmsnorm(x1_ref[...], n2_ref[...]) * (1.0 + mod[:, 4 * d:5 * d]) + mod[:, 3 * d:4 * d]
        h2_ref[...] = h2
        h2b_scr[...] = h2.astype(BF16)

    qp = jnp.dot(h2b_scr[...], wq_ref[...], preferred_element_type=F32)
    half = qp.shape[1] // 2
    tops = []
    for p in range(2):
        qh = qp[:, p * half:(p + 1) * half].astype(BF16)
        st = lax.dot_general(keys_ref[p], qh, nt, preferred_element_type=F32)
        tops.append(_topk_rows(st, PEER_TOPK))
    (a, ia), (b, ib) = tops
    cand = jnp.concatenate([a[i:i + 1, :] + b for i in range(PEER_TOPK)], axis=0)
    cidx = jnp.concatenate([ia[i:i + 1, :] * PEER_N_KEYS + ib for i in range(PEER_TOPK)], axis=0)
    best, idx = _topk_rows(cand, PEER_TOPK, payload=cidx)
    e = jnp.exp(best - best[0:1, :])
    gate_ref[...] = e / jnp.sum(e, axis=0, keepdims=True)
    idx_ref[...] = idx


def _route(x1, mod3, n2, wq, keys, seq, tm, b0):
    t, d = x1.shape
    hw = wq.shape[1] // PEER_HEADS
    per_seq = seq // tm
    return pl.pallas_call(
        _route_kernel,
        out_shape=(jax.ShapeDtypeStruct((t, d), F32),
                   jax.ShapeDtypeStruct((PEER_HEADS, PEER_TOPK, t), jnp.int32),
                   jax.ShapeDtypeStruct((PEER_HEADS, PEER_TOPK, t), F32)),
        grid=(t // tm, PEER_HEADS),
        in_specs=[pl.BlockSpec((tm, d), lambda i, h: (i, 0)),
                  pl.BlockSpec((None, 1, N_MOD * d), lambda i, h: (b0 + i // per_seq, 0, 0)),
                  pl.BlockSpec((1, d), lambda i, h: (0, 0)),
                  pl.BlockSpec((d, hw), lambda i, h: (0, h)),
                  pl.BlockSpec((None, 2, PEER_N_KEYS, hw // 2), lambda i, h: (h, 0, 0, 0))],
        out_specs=(pl.BlockSpec((tm, d), lambda i, h: (i, 0)),
                   pl.BlockSpec((None, PEER_TOPK, tm), lambda i, h: (h, 0, i)),
                   pl.BlockSpec((None, PEER_TOPK, tm), lambda i, h: (h, 0, i))),
        scratch_shapes=[pltpu.VMEM((tm, d), BF16)],
        compiler_params=_cparams("parallel", "arbitrary"),
        name="route",
    )(x1, mod3, n2, wq, keys)


SC_CORES = 2
SC_SUBCORES = 16
SC_LANES = 16
SC_WORKERS = SC_CORES * SC_SUBCORES
SC_ROWS = 32
SC_TOKENS = 16


def _sc_mesh():
    return plsc.VectorSubcoreMesh(core_axis_name="c", subcore_axis_name="s")


def _sc_params():
    return dataclasses.replace(pltpu.CompilerParams(), needs_layout_passes=False)


def _sc_worker_base(tokens_per_worker):
    return (lax.axis_index("s") * SC_CORES + lax.axis_index("c")) * tokens_per_worker


def _sc_chunk_pipeline(tab_hbm, idx_v, bufs, n_chunks, compute):
    def gather(g, b):
        buf, sem = bufs[b]
        return pltpu.make_async_copy(tab_hbm.at[idx_v.at[pl.ds(g * SC_ROWS, SC_ROWS)]], buf, sem)

    gather(0, 0).start()

    @pl.loop(0, n_chunks, step=2)
    def _(g):
        gather(g, 0).wait()
        gather(g + 1, 1).start()
        compute(g, bufs[0][0])
        gather(g + 1, 1).wait()

        @pl.when(g + 2 < n_chunks)
        def _():
            gather(g + 2, 0).start()

        compute(g + 1, bufs[1][0])


def _pack_table(tab):
    half = tab.shape[1] // 2
    lo = lax.bitcast_convert_type(tab[:, :half].astype(BF16), jnp.uint16).astype(jnp.uint32)
    hi = lax.bitcast_convert_type(tab[:, half:].astype(BF16), jnp.uint16).astype(jnp.uint32)
    return lax.bitcast_convert_type((hi << 16) | lo, jnp.int32)


def _unpack_lo(x):
    return lax.bitcast_convert_type(x << 16, F32)


def _unpack_hi(x):
    return lax.bitcast_convert_type(x & jnp.int32(-65536), F32)


def _peer_u_sc(u_pk, idx_flat, h2, k):
    t, d = h2.shape
    words = u_pk.shape[1]
    tpw = t // SC_WORKERS
    cpt = k // SC_ROWS
    cpb = SC_TOKENS * cpt
    pw = words // 2
    nj = pw // SC_LANES

    def body(u_hbm, idx_hbm, h_hbm, o_hbm, idx_v, h_v, buf0, buf1, acc_v, pre_v, sem0, sem1):
        base = _sc_worker_base(tpw)
        lanes = lax.iota(jnp.int32, SC_LANES)

        def compute(g, buf):
            tl = g // cpt
            c = g % cpt
            for p in range(2):
                h_lo = [h_v[tl, pl.ds(p * pw + j * SC_LANES, SC_LANES)] for j in range(nj)]
                h_hi = [h_v[tl, pl.ds(words + p * pw + j * SC_LANES, SC_LANES)] for j in range(nj)]

                @plsc.parallel_loop(0, SC_ROWS)
                def _(r):
                    parts = [None] * 4
                    for j in range(nj):
                        x = buf[r, pl.ds(p * pw + j * SC_LANES, SC_LANES)]
                        term = _unpack_lo(x) * h_lo[j] + _unpack_hi(x) * h_hi[j]
                        parts[j % 4] = term if parts[j % 4] is None else parts[j % 4] + term
                    tot = (parts[0] + parts[1]) + (parts[2] + parts[3])
                    if p == 0:
                        acc_v[r, :] = tot
                    else:
                        acc_v[r, :] = acc_v[r, :] + tot

            for q in range(SC_ROWS // SC_LANES):
                s = plsc.load_gather(acc_v, [lanes + q * SC_LANES, jnp.zeros((SC_LANES,), jnp.int32)])
                for l in range(1, SC_LANES):
                    s = s + plsc.load_gather(acc_v, [lanes + q * SC_LANES,
                                                     jnp.full((SC_LANES,), l, jnp.int32)])
                pre_v[tl, pl.ds(c * SC_ROWS + q * SC_LANES, SC_LANES)] = s

        @pl.loop(0, tpw // SC_TOKENS)
        def _(blk):
            tok0 = base + blk * SC_TOKENS
            pltpu.sync_copy(idx_hbm.at[pl.ds(tok0 * k, SC_TOKENS * k)], idx_v)
            pltpu.sync_copy(h_hbm.at[pl.ds(tok0, SC_TOKENS)], h_v)
            _sc_chunk_pipeline(u_hbm, idx_v, ((buf0, sem0), (buf1, sem1)), cpb, compute)
            pltpu.sync_copy(pre_v, o_hbm.at[pl.ds(tok0, SC_TOKENS)])

    return pl.kernel(
        body,
        out_type=jax.ShapeDtypeStruct((t, k), F32),
        mesh=_sc_mesh(),
        scratch_types=[pltpu.VMEM((SC_TOKENS * k,), jnp.int32),
                       pltpu.VMEM((SC_TOKENS, d), F32),
                       pltpu.VMEM((SC_ROWS, words), jnp.int32),
                       pltpu.VMEM((SC_ROWS, words), jnp.int32),
                       pltpu.VMEM((SC_ROWS, SC_LANES), F32),
                       pltpu.VMEM((SC_TOKENS, k), F32),
                       pltpu.SemaphoreType.DMA,
                       pltpu.SemaphoreType.DMA],
        compiler_params=_sc_params(),
        name="peer_u_sc",
    )(u_pk, idx_flat, h2)


def _peer_v_sc(v_pk, idx_flat, w_flat, t, k):
    words = v_pk.shape[1]
    d = 2 * words
    tpw = t // SC_WORKERS
    cpt = k // SC_ROWS
    cpb = SC_TOKENS * cpt
    pw = words // 2
    nj = pw // SC_LANES

    def body(v_hbm, idx_hbm, w_hbm, o_hbm, idx_v, w_v, buf0, buf1, out_v, sem0, sem1):
        base = _sc_worker_base(tpw)

        def compute(g, buf):
            tl = g // cpt
            c = g % cpt
            for p in range(2):
                lo_cols = [pl.ds(p * pw + j * SC_LANES, SC_LANES) for j in range(nj)]
                hi_cols = [pl.ds(words + p * pw + j * SC_LANES, SC_LANES) for j in range(nj)]

                def row(r, accs):
                    wb = plsc.load_gather(w_v, [jnp.full((SC_LANES,), g * SC_ROWS + r, jnp.int32)])
                    new = []
                    for j in range(nj):
                        x = buf[r, lo_cols[j]]
                        new.append(accs[2 * j] + _unpack_lo(x) * wb)
                        new.append(accs[2 * j + 1] + _unpack_hi(x) * wb)
                    return tuple(new)

                init = []
                for j in range(nj):
                    init.append(jnp.where(c == 0, 0.0, out_v[tl, lo_cols[j]]))
                    init.append(jnp.where(c == 0, 0.0, out_v[tl, hi_cols[j]]))
                accs = lax.fori_loop(0, SC_ROWS, row, tuple(init))
                for j in range(nj):
                    out_v[tl, lo_cols[j]] = accs[2 * j]
                    out_v[tl, hi_cols[j]] = accs[2 * j + 1]

        @pl.loop(0, tpw // SC_TOKENS)
        def _(blk):
            tok0 = base + blk * SC_TOKENS
            pltpu.sync_copy(idx_hbm.at[pl.ds(tok0 * k, SC_TOKENS * k)], idx_v)
            pltpu.sync_copy(w_hbm.at[pl.ds(tok0 * k, SC_TOKENS * k)], w_v)
            _sc_chunk_pipeline(v_hbm, idx_v, ((buf0, sem0), (buf1, sem1)), cpb, compute)
            pltpu.sync_copy(out_v, o_hbm.at[pl.ds(tok0, SC_TOKENS)])

    return pl.kernel(
        body,
        out_type=jax.ShapeDtypeStruct((t, d), F32),
        mesh=_sc_mesh(),
        scratch_types=[pltpu.VMEM((SC_TOKENS * k,), jnp.int32),
                       pltpu.VMEM((SC_TOKENS * k,), F32),
                       pltpu.VMEM((SC_ROWS, words), jnp.int32),
                       pltpu.VMEM((SC_ROWS, words), jnp.int32),
                       pltpu.VMEM((SC_TOKENS, d), F32),
                       pltpu.SemaphoreType.DMA,
                       pltpu.SemaphoreType.DMA],
        compiler_params=_sc_params(),
        name="peer_v_sc",
    )(v_pk, idx_flat, w_flat)


def _act_kernel(gate_ref, pre_ref, o_ref):
    o_ref[...] = gate_ref[...] * _gelu_exact(pre_ref[...])


def _act(gate, pre, tile):
    t, k = gate.shape
    spec = pl.BlockSpec((tile, k), lambda i: (i, 0))
    return pl.pallas_call(
        _act_kernel,
        out_shape=jax.ShapeDtypeStruct((t, k), F32),
        grid=(t // tile,),
        in_specs=[spec, spec],
        out_specs=spec,
        compiler_params=_cparams("parallel"),
        name="expert_act",
    )(gate, pre)


def _final_kernel(x1_ref, y_ref, mod_ref, fg_ref, out_so_far_ref, o_ref):
    del out_so_far_ref
    d = x1_ref.shape[-1]
    x2 = x1_ref[...] + mod_ref[:, 5 * d:6 * d] * y_ref[...]
    o_ref[...] = _rmsnorm(x2, fg_ref[...])


def _final(x1, y, mod3, fg, out_so_far, seq, tile, b0):
    t, d = x1.shape
    per_seq = seq // tile
    row = pl.BlockSpec((tile, d), lambda i: (i, 0))
    return pl.pallas_call(
        _final_kernel,
        out_shape=jax.ShapeDtypeStruct(out_so_far.shape, F32),
        grid=(t // tile,),
        in_specs=[row, row,
                  pl.BlockSpec((None, 1, N_MOD * d), lambda i: (b0 + i // per_seq, 0, 0)),
                  pl.BlockSpec((1, d), lambda i: (0, 0)),
                  pl.BlockSpec(memory_space=pl.ANY)],
        out_specs=pl.BlockSpec((tile, d), lambda i: (b0 * per_seq + i, 0)),
        input_output_aliases={4: 0},
        compiler_params=_cparams("parallel"),
        name="final",
    )(x1, y, mod3, fg, out_so_far)


def _peer(idx, gate, h2, u_pk, v_pk, tile):
    t, k = idx.shape
    assert t % (SC_WORKERS * SC_TOKENS) == 0 and k % (2 * SC_ROWS) == 0
    idx_flat = idx.reshape(t * k)
    pre = _peer_u_sc(u_pk, idx_flat, h2, k)
    w = _act(gate, pre, tile)
    return _peer_v_sc(v_pk, idx_flat, w.reshape(t * k), t, k)


def _rope_tables(length):
    rows = length // GRID_W
    row = jnp.repeat(jnp.arange(rows, dtype=F32), GRID_W)
    col = jnp.tile(jnp.arange(GRID_W, dtype=F32), rows)
    inv_freq = ROPE_BASE ** (-jnp.arange(ROPE_PAIRS, dtype=F32) / ROPE_PAIRS)
    ang_r = row[:, None] * inv_freq
    ang_c = col[:, None] * inv_freq
    cos = jnp.concatenate([jnp.cos(ang_r)] * 2 + [jnp.cos(ang_c)] * 2, axis=-1)
    sin = jnp.concatenate([-jnp.sin(ang_r), jnp.sin(ang_r), -jnp.sin(ang_c), jnp.sin(ang_c)], axis=-1)
    reps = LANES // HEAD_DIM
    return jnp.tile(cos, (1, reps)), jnp.tile(sin, (1, reps))


def _layer(x, c, ctx, c_ctx, w_mod, b_mod, n1, n2, w_in, sink, conv_w, w_attn_out, w_conv_out,
           w_mix_out, pw_q, p_keys, p_u, p_v, final_g, tm, tq, tr, tt, groups):
    batch, seq, d = x.shape
    t = batch * seq
    aw = N_Q_HEADS * HEAD_DIM
    kw = N_KV_HEADS * HEAD_DIM

    rows = -(-(batch + 1) // 8) * 8
    cond = jnp.zeros((rows, d), F32).at[:batch].set(c).at[batch].set(c_ctx)
    mod3 = _adaln(cond, w_mod, b_mod).reshape(rows, 1, N_MOD * d)

    w_in_b = w_in.astype(BF16)
    kvc = _ctx_kv(ctx, mod3, batch, n1.reshape(1, d), w_in_b[:, aw:aw + 2 * kw])

    cos, sin = _rope_tables(seq)
    x2 = x.reshape(t, d)
    wa, wc, wm = w_attn_out.astype(BF16), w_conv_out.astype(BF16), w_mix_out.astype(BF16)
    wq, keys = pw_q.astype(BF16), p_keys.astype(BF16)
    u_pk, v_pk = _pack_table(p_u), _pack_table(p_v)
    nsel = PEER_HEADS * PEER_TOPK

    nb = batch // groups
    tg = nb * seq
    out = pl.empty((t, d), F32)
    for g in range(groups):
        b0 = g * nb
        q, kv, gb, cu, ga, gv = _proj(x2, mod3, n1.reshape(1, d), w_in_b, cos, sin, seq, tm, b0, nb)
        x1 = _mixer(x2, q, kv, kvc, cu, gb, ga, gv, sink.reshape(1, N_Q_HEADS), conv_w,
                    wa, wc, wm, mod3, seq, tq, b0, nb)
        h2, idx_t, gate_t = _route(x1, mod3, n2.reshape(1, d), wq, keys, seq, tr, b0)
        y = _peer(idx_t.reshape(nsel, tg).T, gate_t.reshape(nsel, tg).T, h2, u_pk, v_pk, tt)
        out = _final(x1, y, mod3, final_g.reshape(1, d), out, seq, tt, b0)
    return out.reshape(batch, seq, d)


MAX_TOKEN_GROUPS = 4


def _token_groups(batch, seq):
    for groups in range(min(MAX_TOKEN_GROUPS, batch), 0, -1):
        if batch % groups == 0 and (batch // groups * seq) % (SC_WORKERS * SC_TOKENS) == 0:
            return groups
    raise ValueError("token count must be a multiple of the SparseCore work split")


def kernel(x, c, ctx, c_ctx, w_mod, b_mod, norm1_g, norm2_g, w_in, attn_sink, conv_w, w_attn_out,
           w_conv_out, w_mix_out, peer_w_q, peer_sub_keys, peer_u, peer_v, final_g):
    assert w_mod.shape[0] == 1, "only the single-layer configuration is implemented"
    seq = x.shape[1]
    return _layer(x, c, ctx, c_ctx, w_mod[0], b_mod[0], norm1_g[0], norm2_g[0], w_in[0],
                  attn_sink[0], conv_w[0], w_attn_out[0], w_conv_out[0], w_mix_out[0],
                  peer_w_q[0], peer_sub_keys[0], peer_u[0], peer_v[0], final_g,
                  tm=min(512, seq), tq=min(256, seq), tr=min(256, seq), tt=min(512, seq),
                  groups=_token_groups(x.shape[0], seq))
```

```python
import dataclasses

import jax
import jax.numpy as jnp
from jax import lax
from jax.experimental import pallas as pl
from jax.experimental.pallas import tpu as pltpu
from jax.experimental.pallas import tpu_sc as plsc

HEAD_DIM = 64
N_Q_HEADS = 8
N_KV_HEADS = 2
Q_PER_KV = N_Q_HEADS // N_KV_HEADS
WINDOW = 128
GRID_W = 64
ROPE_BASE = 10000.0
ROPE_PAIRS = HEAD_DIM // 4
PEER_HEADS = 8
PEER_N_KEYS = 128
PEER_TOPK = 16
N_MOD = 6
EPS = 1e-6
NEG_INF = -1e30

LANES = 128
VMEM_LIMIT = 56 * 1024 * 1024

F32 = jnp.float32
BF16 = jnp.bfloat16


def _cparams(*sem):
    return pltpu.CompilerParams(dimension_semantics=sem, vmem_limit_bytes=VMEM_LIMIT)


def _rmsnorm(x, g):
    return x * lax.rsqrt(jnp.mean(x * x, axis=-1, keepdims=True) + EPS) * g


def _gelu_exact(x):
    return 0.5 * x * (1.0 + lax.erf(x * (2.0 ** -0.5)))


def _adaln_kernel(cond_ref, w_ref, b_ref, o_ref):
    act = jax.nn.silu(cond_ref[...])
    o_ref[...] = jnp.dot(act, w_ref[...], precision=lax.Precision.HIGHEST,
                         preferred_element_type=F32) + b_ref[...]


def _adaln(cond, w_mod, b_mod):
    rows, d = cond.shape
    n = w_mod.shape[1]
    tn = d
    return pl.pallas_call(
        _adaln_kernel,
        out_shape=jax.ShapeDtypeStruct((rows, n), F32),
        grid=(n // tn,),
        in_specs=[pl.BlockSpec((rows, d), lambda j: (0, 0)),
                  pl.BlockSpec((d, tn), lambda j: (0, j)),
                  pl.BlockSpec((1, tn), lambda j: (0, j))],
        out_specs=pl.BlockSpec((rows, tn), lambda j: (0, j)),
        compiler_params=_cparams("parallel"),
        name="adaln",
    )(cond, w_mod, b_mod.reshape(1, n))


def _ctx_kv_kernel(xc_ref, mod_ref, n1_ref, w_ref, o_ref):
    d = xc_ref.shape[-1]
    mod = mod_ref[...]
    hc = _rmsnorm(xc_ref[...], n1_ref[...]) * (1.0 + mod[:, d:2 * d]) + mod[:, 0:d]
    o_ref[...] = jnp.dot(hc.astype(BF16), w_ref[...], preferred_element_type=F32).astype(BF16)


def _ctx_kv(ctx, mod3, ctx_row, n1, w_kv):
    b, c, d = ctx.shape
    kvw = w_kv.shape[1]
    return pl.pallas_call(
        _ctx_kv_kernel,
        out_shape=jax.ShapeDtypeStruct((b, c, kvw), BF16),
        grid=(b,),
        in_specs=[pl.BlockSpec((None, c, d), lambda i: (i, 0, 0)),
                  pl.BlockSpec((None, 1, N_MOD * d), lambda i: (ctx_row, 0, 0)),
                  pl.BlockSpec((1, d), lambda i: (0, 0)),
                  pl.BlockSpec((d, kvw), lambda i: (0, 0))],
        out_specs=pl.BlockSpec((None, c, kvw), lambda i: (i, 0, 0)),
        compiler_params=_cparams("parallel"),
        name="ctx_kv",
    )(ctx, mod3, n1, w_kv)


def _proj_kernel(x_ref, mod_ref, n1_ref, w_ref, cos_ref, sin_ref,
                 q_ref, kv_ref, gb_ref, cu_ref, ga_ref, gv_ref):
    d = x_ref.shape[-1]
    aw = q_ref.shape[-1]
    kw = kv_ref.shape[-1] // 2
    cw = gb_ref.shape[-1]
    mod = mod_ref[...]
    h = (_rmsnorm(x_ref[...], n1_ref[...]) * (1.0 + mod[:, d:2 * d]) + mod[:, 0:d]).astype(BF16)
    cos = cos_ref[...]
    sin = sin_ref[...]
    lane = lax.broadcasted_iota(jnp.int32, cos.shape, 1)
    first_half = (lane % (2 * ROPE_PAIRS)) < ROPE_PAIRS

    def rope(z):
        partner = jnp.where(first_half, pltpu.roll(z, LANES - ROPE_PAIRS, 1),
                            pltpu.roll(z, ROPE_PAIRS, 1))
        return z * cos + partner * sin

    def proj(lo, width):
        return jnp.dot(h, w_ref[:, lo:lo + width], preferred_element_type=F32)

    off = 0
    zq = proj(off, aw)
    for g in range(aw // LANES):
        q_ref[:, g * LANES:(g + 1) * LANES] = rope(zq[:, g * LANES:(g + 1) * LANES]).astype(BF16)
    off += aw
    zkv = proj(off, 2 * kw)
    for g in range(kw // LANES):
        kv_ref[:, g * LANES:(g + 1) * LANES] = rope(zkv[:, g * LANES:(g + 1) * LANES]).astype(BF16)
    kv_ref[:, kw:] = zkv[:, kw:].astype(BF16)
    off += 2 * kw
    gb_ref[...] = proj(off, cw)
    off += cw
    zc = proj(off, cw)
    off += cw
    cu_ref[...] = zc * proj(off, cw)
    off += cw
    ga_ref[...] = proj(off, d)
    off += d
    gv_ref[...] = proj(off, d)


def _proj(x2, mod3, n1, w_in, cos, sin, seq, tm, b0, nb):
    d = x2.shape[1]
    t = nb * seq
    aw = N_Q_HEADS * HEAD_DIM
    kw = N_KV_HEADS * HEAD_DIM
    cw = d // 2
    per_seq = seq // tm
    row = lambda i: (i, 0)
    return pl.pallas_call(
        _proj_kernel,
        out_shape=(jax.ShapeDtypeStruct((t, aw), BF16),
                   jax.ShapeDtypeStruct((t, 2 * kw), BF16),
                   jax.ShapeDtypeStruct((t, cw), F32),
                   jax.ShapeDtypeStruct((t, cw), F32),
                   jax.ShapeDtypeStruct((t, d), F32),
                   jax.ShapeDtypeStruct((t, d), F32)),
        grid=(t // tm,),
        in_specs=[pl.BlockSpec((tm, d), lambda i: (b0 * per_seq + i, 0)),
                  pl.BlockSpec((None, 1, N_MOD * d), lambda i: (b0 + i // per_seq, 0, 0)),
                  pl.BlockSpec((1, d), lambda i: (0, 0)),
                  pl.BlockSpec(w_in.shape, lambda i: (0, 0)),
                  pl.BlockSpec((tm, LANES), lambda i: (i % per_seq, 0)),
                  pl.BlockSpec((tm, LANES), lambda i: (i % per_seq, 0))],
        out_specs=(pl.BlockSpec((tm, aw), row), pl.BlockSpec((tm, 2 * kw), row),
                   pl.BlockSpec((tm, cw), row), pl.BlockSpec((tm, cw), row),
                   pl.BlockSpec((tm, d), row), pl.BlockSpec((tm, d), row)),
        compiler_params=_cparams("parallel"),
        name="proj",
    )(x2, mod3, n1, w_in, cos, sin)


def _mixer_kernel(sink_ref, x_ref, q_ref, kv_ref, kvp_ref, kvn_ref, kvc_ref,
                  cu_ref, cup_ref, cun_ref, gb_ref, ga_ref, gv_ref, convw_ref,
                  wa_ref, wc_ref, wm_ref, mod_ref, o_ref, attn_scr):
    n = pl.program_id(1)
    has_prev = n > 0
    has_next = n < pl.num_programs(1) - 1
    tq, d = x_ref.shape
    kw = N_KV_HEADS * HEAD_DIM
    scale = HEAD_DIM ** -0.5
    nt = (((1,), (1,)), ((), ()))

    kext = jnp.concatenate([kvp_ref[...], kv_ref[...], kvn_ref[...]], axis=0)
    kctx = kvc_ref[...]
    qi = lax.broadcasted_iota(jnp.int32, (WINDOW, WINDOW), 0)
    ki = lax.broadcasted_iota(jnp.int32, (WINDOW, WINDOW), 1)
    nblk = tq // WINDOW
    for j in range(nblk):
        prev_ok = ki >= qi
        next_ok = ki <= qi
        if j == 0:
            prev_ok = jnp.logical_and(prev_ok, has_prev)
        if j == nblk - 1:
            next_ok = jnp.logical_and(next_ok, has_next)
        mask = jnp.concatenate([prev_ok, jnp.ones_like(prev_ok), next_ok], axis=1)
        rows = slice(j * WINDOW, (j + 1) * WINDOW)
        krows = slice(j * WINDOW, (j + 3) * WINDOW)
        for hq in range(N_Q_HEADS):
            kvh = hq // Q_PER_KV
            kcol = slice(kvh * HEAD_DIM, (kvh + 1) * HEAD_DIM)
            vcol = slice(kw + kvh * HEAD_DIM, kw + (kvh + 1) * HEAD_DIM)
            qh = q_ref[rows, hq * HEAD_DIM:(hq + 1) * HEAD_DIM]
            s_loc = lax.dot_general(qh, kext[krows, kcol], nt, preferred_element_type=F32) * scale
            s_loc = jnp.where(mask, s_loc, NEG_INF)
            s_ctx = lax.dot_general(qh, kctx[:, kcol], nt, preferred_element_type=F32) * scale
            sink = sink_ref[0, hq]
            m = jnp.maximum(jnp.max(s_loc, axis=-1, keepdims=True),
                            jnp.max(s_ctx, axis=-1, keepdims=True))
            m = jnp.maximum(m, sink)
            p_loc = jnp.exp(s_loc - m)
            p_ctx = jnp.exp(s_ctx - m)
            den = (jnp.sum(p_loc, axis=-1, keepdims=True) + jnp.sum(p_ctx, axis=-1, keepdims=True)
                   + jnp.exp(sink - m))
            o = (jnp.dot(p_loc.astype(BF16), kext[krows, vcol], preferred_element_type=F32)
                 + jnp.dot(p_ctx.astype(BF16), kctx[:, vcol], preferred_element_type=F32))
            attn_scr[rows, hq * HEAD_DIM:(hq + 1) * HEAD_DIM] = o / den

    y_attn = jnp.dot(attn_scr[...].astype(BF16), wa_ref[...], preferred_element_type=F32)

    cu = cu_ref[...]
    ri = lax.broadcasted_iota(jnp.int32, cu.shape, 0)
    prev_row = jnp.where(has_prev, cup_ref[7:8, :], 0.0)
    next_row = jnp.where(has_next, cun_ref[0:1, :], 0.0)
    cu_m1 = jnp.where(ri == 0, prev_row, pltpu.roll(cu, 1, 0))
    cu_p1 = jnp.where(ri == tq - 1, next_row, pltpu.roll(cu, tq - 1, 0))
    cw = convw_ref[...]
    conv = cu_m1 * cw[0:1, :] + cu * cw[1:2, :] + cu_p1 * cw[2:3, :]
    y_conv = jnp.dot((gb_ref[...] * conv).astype(BF16), wc_ref[...], preferred_element_type=F32)

    merged = jax.nn.sigmoid(ga_ref[...]) * y_attn + jax.nn.sigmoid(gv_ref[...]) * y_conv
    y = jnp.dot(merged.astype(BF16), wm_ref[...], preferred_element_type=F32)
    g1 = mod_ref[:, 2 * d:3 * d]
    o_ref[...] = x_ref[...] + g1 * y


def _mixer(x2, q, kv, kvc, cu, gb, ga, gv, sink, conv_w, wa, wc, wm, mod3, seq, tq, b0, batch):
    d = x2.shape[1]
    t, aw = q.shape
    kv2 = kv.shape[1]
    cw = cu.shape[1]
    c = kvc.shape[1]
    nq = seq // tq
    nb = seq // WINDOW
    sub = tq // WINDOW
    kv3 = kv.reshape(t // WINDOW, WINDOW, kv2)
    cu3 = cu.reshape(t // 8, 8, cw)
    tile = lambda b, n: (b * nq + n, 0)
    const = lambda b, n: (0, 0)
    return pl.pallas_call(
        _mixer_kernel,
        out_shape=jax.ShapeDtypeStruct((t, d), F32),
        grid=(batch, nq),
        in_specs=[
            pl.BlockSpec(memory_space=pltpu.SMEM),
            pl.BlockSpec((tq, d), lambda b, n: ((b0 + b) * nq + n, 0)),
            pl.BlockSpec((tq, aw), tile),
            pl.BlockSpec((tq, kv2), tile),
            pl.BlockSpec((None, WINDOW, kv2), lambda b, n: (b * nb + jnp.maximum(n * sub - 1, 0), 0, 0)),
            pl.BlockSpec((None, WINDOW, kv2), lambda b, n: (b * nb + jnp.minimum((n + 1) * sub, nb - 1), 0, 0)),
            pl.BlockSpec((None, c, kv2), lambda b, n: (b0 + b, 0, 0)),
            pl.BlockSpec((tq, cw), tile),
            pl.BlockSpec((None, 8, cw), lambda b, n: (jnp.maximum((b * seq + n * tq) // 8 - 1, 0), 0, 0)),
            pl.BlockSpec((None, 8, cw), lambda b, n: (jnp.minimum((b * seq + (n + 1) * tq) // 8, t // 8 - 1), 0, 0)),
            pl.BlockSpec((tq, cw), tile),
            pl.BlockSpec((tq, d), tile),
            pl.BlockSpec((tq, d), tile),
            pl.BlockSpec(conv_w.shape, const),
            pl.BlockSpec(wa.shape, const),
            pl.BlockSpec(wc.shape, const),
            pl.BlockSpec(wm.shape, const),
            pl.BlockSpec((None, 1, N_MOD * d), lambda b, n: (b0 + b, 0, 0)),
        ],
        out_specs=pl.BlockSpec((tq, d), tile),
        scratch_shapes=[pltpu.VMEM((tq, aw), F32)],
        compiler_params=_cparams("parallel", "parallel"),
        name="mixer",
    )(sink, x2, q, kv, kv3, kv3, kvc, cu, cu3, cu3, gb, ga, gv, conv_w, wa, wc, wm, mod3)


def _topk_rows(s, k, payload=None):
    n = s.shape[0]
    iota = lax.broadcasted_iota(jnp.int32, s.shape, 0)
    vals, ids = [], []
    for _ in range(k):
        m = jnp.max(s, axis=0, keepdims=True)
        am = jnp.min(jnp.where(s == m, iota, n), axis=0, keepdims=True)
        hit = iota == am
        vals.append(m)
        if payload is None:
            ids.append(am)
        else:
            ids.append(jnp.max(jnp.where(hit, payload, -1), axis=0, keepdims=True))
        s = jnp.where(hit, -jnp.inf, s)
    return jnp.concatenate(vals, axis=0), jnp.concatenate(ids, axis=0)


def _route_kernel(x1_ref, mod_ref, n2_ref, wq_ref, keys_ref, h2_ref, idx_ref, gate_ref, h2b_scr):
    d = x1_ref.shape[-1]
    nt = (((1,), (1,)), ((), ()))

    @pl.when(pl.program_id(1) == 0)
    def _():
        mod = mod_ref[...]
        h2 = _rmsnorm(x1_ref[...], n2_ref[...]) * (1.0 + mod[:, 4 * d:5 * d]) + mod[:, 3 * d:4 * d]
        h2_ref[...] = h2
        h2b_scr[...] = h2.astype(BF16)

    qp = jnp.dot(h2b_scr[...], wq_ref[...], preferred_element_type=F32)
    half = qp.shape[1] // 2
    tops = []
    for p in range(2):
        qh = qp[:, p * half:(p + 1) * half].astype(BF16)
        st = lax.dot_general(keys_ref[p], qh, nt, preferred_element_type=F32)
        tops.append(_topk_rows(st, PEER_TOPK))
    (a, ia), (b, ib) = tops
    cand = jnp.concatenate([a[i:i + 1, :] + b for i in range(PEER_TOPK)], axis=0)
    cidx = jnp.concatenate([ia[i:i + 1, :] * PEER_N_KEYS + ib for i in range(PEER_TOPK)], axis=0)
    best, idx = _topk_rows(cand, PEER_TOPK, payload=cidx)
    e = jnp.exp(best - best[0:1, :])
    gate_ref[...] = e / jnp.sum(e, axis=0, keepdims=True)
    idx_ref[...] = idx


def _route(x1, mod3, n2, wq, keys, seq, tm, b0):
    t, d = x1.shape
    hw = wq.shape[1] // PEER_HEADS
    per_seq = seq // tm
    return pl.pallas_call(
        _route_kernel,
        out_shape=(jax.ShapeDtypeStruct((t, d), F32),
                   jax.ShapeDtypeStruct((PEER_HEADS, PEER_TOPK, t), jnp.int32),
                   jax.ShapeDtypeStruct((PEER_HEADS, PEER_TOPK, t), F32)),
        grid=(t // tm, PEER_HEADS),
        in_specs=[pl.BlockSpec((tm, d), lambda i, h: (i, 0)),
                  pl.BlockSpec((None, 1, N_MOD * d), lambda i, h: (b0 + i // per_seq, 0, 0)),
                  pl.BlockSpec((1, d), lambda i, h: (0, 0)),
                  pl.BlockSpec((d, hw), lambda i, h: (0, h)),
                  pl.BlockSpec((None, 2, PEER_N_KEYS, hw // 2), lambda i, h: (h, 0, 0, 0))],
        out_specs=(pl.BlockSpec((tm, d), lambda i, h: (i, 0)),
                   pl.BlockSpec((None, PEER_TOPK, tm), lambda i, h: (h, 0, i)),
                   pl.BlockSpec((None, PEER_TOPK, tm), lambda i, h: (h, 0, i))),
        scratch_shapes=[pltpu.VMEM((tm, d), BF16)],
        compiler_params=_cparams("parallel", "arbitrary"),
        name="route",
    )(x1, mod3, n2, wq, keys)


SC_CORES = 2
SC_SUBCORES = 16
SC_LANES = 16
SC_WORKERS = SC_CORES * SC_SUBCORES
SC_ROWS_U = 32
SC_ROWS_V = 64
SC_ROWS_PER_ITER = 8
SC_TOKENS = 16


def _sc_mesh():
    return plsc.VectorSubcoreMesh(core_axis_name="c", subcore_axis_name="s")


def _sc_params():
    return dataclasses.replace(pltpu.CompilerParams(), needs_layout_passes=False)


def _sc_worker_base(tokens_per_worker):
    return (lax.axis_index("s") * SC_CORES + lax.axis_index("c")) * tokens_per_worker


def _sc_chunk_pipeline(tab_hbm, idx_v, bufs, n_chunks, compute):
    rows = bufs[0][0].shape[0]

    def gather(g, b):
        buf, sem = bufs[b]
        return pltpu.make_async_copy(tab_hbm.at[idx_v.at[pl.ds(g * rows, rows)]], buf, sem)

    gather(0, 0).start()

    @pl.loop(0, n_chunks, step=2)
    def _(g):
        gather(g, 0).wait()
        gather(g + 1, 1).start()
        compute(g, bufs[0][0])
        gather(g + 1, 1).wait()

        @pl.when(g + 2 < n_chunks)
        def _():
            gather(g + 2, 0).start()

        compute(g + 1, bufs[1][0])


def _pack_table(tab):
    half = tab.shape[1] // 2
    lo = lax.bitcast_convert_type(tab[:, :half].astype(BF16), jnp.uint16).astype(jnp.uint32)
    hi = lax.bitcast_convert_type(tab[:, half:].astype(BF16), jnp.uint16).astype(jnp.uint32)
    return lax.bitcast_convert_type((hi << 16) | lo, jnp.int32)


def _unpack_lo(x):
    return lax.bitcast_convert_type(x << 16, F32)


def _unpack_hi(x):
    return lax.bitcast_convert_type(x & jnp.int32(-65536), F32)


def _peer_u_sc(u_pk, idx_flat, h2, k):
    t, d = h2.shape
    words = u_pk.shape[1]
    tpw = t // SC_WORKERS
    rows = SC_ROWS_U
    cpt = k // rows
    cpb = SC_TOKENS * cpt
    nj = words // SC_LANES

    def body(u_hbm, idx_hbm, h_hbm, o_hbm, idx_v, h_v, buf0, buf1, acc_v, pre_v, sem0, sem1):
        base = _sc_worker_base(tpw)
        lanes = lax.iota(jnp.int32, SC_LANES)

        def compute(g, buf):
            tl = g // cpt
            c = g % cpt

            @plsc.parallel_loop(0, rows // SC_ROWS_PER_ITER)
            def _(it):
                r0 = it * SC_ROWS_PER_ITER
                accs = [[None, None] for _ in range(SC_ROWS_PER_ITER)]
                for j in range(nj):
                    h_lo = h_v[tl, pl.ds(j * SC_LANES, SC_LANES)]
                    h_hi = h_v[tl, pl.ds(words + j * SC_LANES, SC_LANES)]
                    for a in range(SC_ROWS_PER_ITER):
                        x = buf[r0 + a, pl.ds(j * SC_LANES, SC_LANES)]
                        term = _unpack_lo(x) * h_lo + _unpack_hi(x) * h_hi
                        accs[a][j % 2] = term if accs[a][j % 2] is None else accs[a][j % 2] + term
                for a in range(SC_ROWS_PER_ITER):
                    acc_v[r0 + a, :] = accs[a][0] + accs[a][1]

            for q in range(rows // SC_LANES):
                s = plsc.load_gather(acc_v, [lanes + q * SC_LANES, jnp.zeros((SC_LANES,), jnp.int32)])
                for l in range(1, SC_LANES):
                    s = s + plsc.load_gather(acc_v, [lanes + q * SC_LANES,
                                                     jnp.full((SC_LANES,), l, jnp.int32)])
                pre_v[tl, pl.ds(c * rows + q * SC_LANES, SC_LANES)] = s

        @pl.loop(0, tpw // SC_TOKENS)
        def _(blk):
            tok0 = base + blk * SC_TOKENS
            pltpu.sync_copy(idx_hbm.at[pl.ds(tok0 * k, SC_TOKENS * k)], idx_v)
            pltpu.sync_copy(h_hbm.at[pl.ds(tok0, SC_TOKENS)], h_v)
            _sc_chunk_pipeline(u_hbm, idx_v, ((buf0, sem0), (buf1, sem1)), cpb, compute)
            pltpu.sync_copy(pre_v, o_hbm.at[pl.ds(tok0, SC_TOKENS)])

    return pl.kernel(
        body,
        out_type=jax.ShapeDtypeStruct((t, k), F32),
        mesh=_sc_mesh(),
        scratch_types=[pltpu.VMEM((SC_TOKENS * k,), jnp.int32),
                       pltpu.VMEM((SC_TOKENS, d), F32),
                       pltpu.VMEM((rows, words), jnp.int32),
                       pltpu.VMEM((rows, words), jnp.int32),
                       pltpu.VMEM((rows, SC_LANES), F32),
                       pltpu.VMEM((SC_TOKENS, k), F32),
                       pltpu.SemaphoreType.DMA,
                       pltpu.SemaphoreType.DMA],
        compiler_params=_sc_params(),
        name="peer_u_sc",
    )(u_pk, idx_flat, h2)


def _peer_v_sc(v_pk, idx_flat, w_flat, t, k):
    words = v_pk.shape[1]
    d = 2 * words
    tpw = t // SC_WORKERS
    rows = SC_ROWS_V
    cpt = k // rows
    cpb = SC_TOKENS * cpt
    pw = words // 2
    nj = pw // SC_LANES

    def body(v_hbm, idx_hbm, w_hbm, o_hbm, idx_v, w_v, buf0, buf1, out_v, sem0, sem1):
        base = _sc_worker_base(tpw)

        def compute(g, buf):
            tl = g // cpt
            c = g % cpt
            for p in range(2):
                lo_cols = [pl.ds(p * pw + j * SC_LANES, SC_LANES) for j in range(nj)]
                hi_cols = [pl.ds(words + p * pw + j * SC_LANES, SC_LANES) for j in range(nj)]

                def row(r, accs):
                    wb = plsc.load_gather(w_v, [jnp.full((SC_LANES,), g * rows + r, jnp.int32)])
                    new = []
                    for j in range(nj):
                        x = buf[r, lo_cols[j]]
                        new.append(accs[2 * j] + _unpack_lo(x) * wb)
                        new.append(accs[2 * j + 1] + _unpack_hi(x) * wb)
                    return tuple(new)

                init = []
                for j in range(nj):
                    init.append(jnp.where(c == 0, 0.0, out_v[tl, lo_cols[j]]))
                    init.append(jnp.where(c == 0, 0.0, out_v[tl, hi_cols[j]]))
                accs = lax.fori_loop(0, rows, row, tuple(init))
                for j in range(nj):
                    out_v[tl, lo_cols[j]] = accs[2 * j]
                    out_v[tl, hi_cols[j]] = accs[2 * j + 1]

        @pl.loop(0, tpw // SC_TOKENS)
        def _(blk):
            tok0 = base + blk * SC_TOKENS
            pltpu.sync_copy(idx_hbm.at[pl.ds(tok0 * k, SC_TOKENS * k)], idx_v)
            pltpu.sync_copy(w_hbm.at[pl.ds(tok0 * k, SC_TOKENS * k)], w_v)
            _sc_chunk_pipeline(v_hbm, idx_v, ((buf0, sem0), (buf1, sem1)), cpb, compute)
            pltpu.sync_copy(out_v, o_hbm.at[pl.ds(tok0, SC_TOKENS)])

    return pl.kernel(
        body,
        out_type=jax.ShapeDtypeStruct((t, d), F32),
        mesh=_sc_mesh(),
        scratch_types=[pltpu.VMEM((SC_TOKENS * k,), jnp.int32),
                       pltpu.VMEM((SC_TOKENS * k,), F32),
                       pltpu.VMEM((rows, words), jnp.int32),
                       pltpu.VMEM((rows, words), jnp.int32),
                       pltpu.VMEM((SC_TOKENS, d), F32),
                       pltpu.SemaphoreType.DMA,
                       pltpu.SemaphoreType.DMA],
        compiler_params=_sc_params(),
        name="peer_v_sc",
    )(v_pk, idx_flat, w_flat)


def _act_kernel(gate_ref, pre_ref, o_ref):
    o_ref[...] = gate_ref[...] * _gelu_exact(pre_ref[...])


def _act(gate, pre, tile):
    t, k = gate.shape
    spec = pl.BlockSpec((tile, k), lambda i: (i, 0))
    return pl.pallas_call(
        _act_kernel,
        out_shape=jax.ShapeDtypeStruct((t, k), F32),
        grid=(t // tile,),
        in_specs=[spec, spec],
        out_specs=spec,
        compiler_params=_cparams("parallel"),
        name="expert_act",
    )(gate, pre)


def _final_kernel(x1_ref, y_ref, mod_ref, fg_ref, out_so_far_ref, o_ref):
    del out_so_far_ref
    d = x1_ref.shape[-1]
    x2 = x1_ref[...] + mod_ref[:, 5 * d:6 * d] * y_ref[...]
    o_ref[...] = _rmsnorm(x2, fg_ref[...])


def _final(x1, y, mod3, fg, out_so_far, seq, tile, b0):
    t, d = x1.shape
    per_seq = seq // tile
    row = pl.BlockSpec((tile, d), lambda i: (i, 0))
    return pl.pallas_call(
        _final_kernel,
        out_shape=jax.ShapeDtypeStruct(out_so_far.shape, F32),
        grid=(t // tile,),
        in_specs=[row, row,
                  pl.BlockSpec((None, 1, N_MOD * d), lambda i: (b0 + i // per_seq, 0, 0)),
                  pl.BlockSpec((1, d), lambda i: (0, 0)),
                  pl.BlockSpec(memory_space=pl.ANY)],
        out_specs=pl.BlockSpec((tile, d), lambda i: (b0 * per_seq + i, 0)),
        input_output_aliases={4: 0},
        compiler_params=_cparams("parallel"),
        name="final",
    )(x1, y, mod3, fg, out_so_far)


def _peer(idx, gate, h2, u_pk, v_pk, tile):
    t, k = idx.shape
    assert t % (SC_WORKERS * SC_TOKENS) == 0 and k % SC_ROWS_U == 0 and k % SC_ROWS_V == 0
    idx_flat = idx.reshape(t * k)
    pre = _peer_u_sc(u_pk, idx_flat, h2, k)
    w = _act(gate, pre, tile)
    return _peer_v_sc(v_pk, idx_flat, w.reshape(t * k), t, k)


def _rope_tables(length):
    rows = length // GRID_W
    row = jnp.repeat(jnp.arange(rows, dtype=F32), GRID_W)
    col = jnp.tile(jnp.arange(GRID_W, dtype=F32), rows)
    inv_freq = ROPE_BASE ** (-jnp.arange(ROPE_PAIRS, dtype=F32) / ROPE_PAIRS)
    ang_r = row[:, None] * inv_freq
    ang_c = col[:, None] * inv_freq
    cos = jnp.concatenate([jnp.cos(ang_r)] * 2 + [jnp.cos(ang_c)] * 2, axis=-1)
    sin = jnp.concatenate([-jnp.sin(ang_r), jnp.sin(ang_r), -jnp.sin(ang_c), jnp.sin(ang_c)], axis=-1)
    reps = LANES // HEAD_DIM
    return jnp.tile(cos, (1, reps)), jnp.tile(sin, (1, reps))


def _layer(x, c, ctx, c_ctx, w_mod, b_mod, n1, n2, w_in, sink, conv_w, w_attn_out, w_conv_out,
           w_mix_out, pw_q, p_keys, p_u, p_v, final_g, tm, tq, tr, tt, groups):
    batch, seq, d = x.shape
    t = batch * seq
    aw = N_Q_HEADS * HEAD_DIM
    kw = N_KV_HEADS * HEAD_DIM

    rows = -(-(batch + 1) // 8) * 8
    cond = jnp.zeros((rows, d), F32).at[:batch].set(c).at[batch].set(c_ctx)
    mod3 = _adaln(cond, w_mod, b_mod).reshape(rows, 1, N_MOD * d)

    w_in_b = w_in.astype(BF16)
    kvc = _ctx_kv(ctx, mod3, batch, n1.reshape(1, d), w_in_b[:, aw:aw + 2 * kw])

    cos, sin = _rope_tables(seq)
    x2 = x.reshape(t, d)
    wa, wc, wm = w_attn_out.astype(BF16), w_conv_out.astype(BF16), w_mix_out.astype(BF16)
    wq, keys = pw_q.astype(BF16), p_keys.astype(BF16)
    u_pk, v_pk = _pack_table(p_u), _pack_table(p_v)
    nsel = PEER_HEADS * PEER_TOPK

    nb = batch // groups
    tg = nb * seq
    out = pl.empty((t, d), F32)
    for g in range(groups):
        b0 = g * nb
        q, kv, gb, cu, ga, gv = _proj(x2, mod3, n1.reshape(1, d), w_in_b, cos, sin, seq, tm, b0, nb)
        x1 = _mixer(x2, q, kv, kvc, cu, gb, ga, gv, sink.reshape(1, N_Q_HEADS), conv_w,
                    wa, wc, wm, mod3, seq, tq, b0, nb)
        h2, idx_t, gate_t = _route(x1, mod3, n2.reshape(1, d), wq, keys, seq, tr, b0)
        y = _peer(idx_t.reshape(nsel, tg).T, gate_t.reshape(nsel, tg).T, h2, u_pk, v_pk, tt)
        out = _final(x1, y, mod3, final_g.reshape(1, d), out, seq, tt, b0)
    return out.reshape(batch, seq, d)


MAX_TOKEN_GROUPS = 4


def _token_groups(batch, seq):
    for groups in range(min(MAX_TOKEN_GROUPS, batch), 0, -1):
        if batch % groups == 0 and (batch // groups * seq) % (SC_WORKERS * SC_TOKENS) == 0:
            return groups
    raise ValueError("token count must be a multiple of the SparseCore work split")


def kernel(x, c, ctx, c_ctx, w_mod, b_mod, norm1_g, norm2_g, w_in, attn_sink, conv_w, w_attn_out,
           w_conv_out, w_mix_out, peer_w_q, peer_sub_keys, peer_u, peer_v, final_g):
    assert w_mod.shape[0] == 1, "only the single-layer configuration is implemented"
    seq = x.shape[1]
    return _layer(x, c, ctx, c_ctx, w_mod[0], b_mod[0], norm1_g[0], norm2_g[0], w_in[0],
                  attn_sink[0], conv_w[0], w_attn_out[0], w_conv_out[0], w_mix_out[0],
                  peer_w_q[0], peer_sub_keys[0], peer_u[0], peer_v[0], final_g,
                  tm=min(512, seq), tq=min(256, seq), tr=min(256, seq), tt=min(512, seq),
                  groups=_token_groups(x.shape[0], seq))
```

```python
import dataclasses

import jax
import jax.numpy as jnp
from jax import lax
from jax.experimental import pallas as pl
from jax.experimental.pallas import tpu as pltpu
from jax.experimental.pallas import tpu_sc as plsc

HEAD_DIM = 64
N_Q_HEADS = 8
N_KV_HEADS = 2
Q_PER_KV = N_Q_HEADS // N_KV_HEADS
WINDOW = 128
GRID_W = 64
ROPE_BASE = 10000.0
ROPE_PAIRS = HEAD_DIM // 4
PEER_HEADS = 8
PEER_N_KEYS = 128
PEER_TOPK = 16
N_MOD = 6
EPS = 1e-6
NEG_INF = -1e30

LANES = 128
VMEM_LIMIT = 56 * 1024 * 1024

F32 = jnp.float32
BF16 = jnp.bfloat16


def _cparams(*sem):
    return pltpu.CompilerParams(dimension_semantics=sem, vmem_limit_bytes=VMEM_LIMIT)


def _rmsnorm(x, g):
    return x * lax.rsqrt(jnp.mean(x * x, axis=-1, keepdims=True) + EPS) * g


def _gelu_exact(x):
    return 0.5 * x * (1.0 + lax.erf(x * (2.0 ** -0.5)))


def _adaln_kernel(cond_ref, w_ref, b_ref, o_ref):
    act = jax.nn.silu(cond_ref[...])
    o_ref[...] = jnp.dot(act, w_ref[...], precision=lax.Precision.HIGHEST,
                         preferred_element_type=F32) + b_ref[...]


def _adaln(cond, w_mod, b_mod):
    rows, d = cond.shape
    n = w_mod.shape[1]
    tn = d
    return pl.pallas_call(
        _adaln_kernel,
        out_shape=jax.ShapeDtypeStruct((rows, n), F32),
        grid=(n // tn,),
        in_specs=[pl.BlockSpec((rows, d), lambda j: (0, 0)),
                  pl.BlockSpec((d, tn), lambda j: (0, j)),
                  pl.BlockSpec((1, tn), lambda j: (0, j))],
        out_specs=pl.BlockSpec((rows, tn), lambda j: (0, j)),
        compiler_params=_cparams("parallel"),
        name="adaln",
    )(cond, w_mod, b_mod.reshape(1, n))


def _ctx_kv_kernel(xc_ref, mod_ref, n1_ref, w_ref, o_ref):
    d = xc_ref.shape[-1]
    mod = mod_ref[...]
    hc = _rmsnorm(xc_ref[...], n1_ref[...]) * (1.0 + mod[:, d:2 * d]) + mod[:, 0:d]
    o_ref[...] = jnp.dot(hc.astype(BF16), w_ref[...], preferred_element_type=F32).astype(BF16)


def _ctx_kv(ctx, mod3, ctx_row, n1, w_kv):
    b, c, d = ctx.shape
    kvw = w_kv.shape[1]
    return pl.pallas_call(
        _ctx_kv_kernel,
        out_shape=jax.ShapeDtypeStruct((b, c, kvw), BF16),
        grid=(b,),
        in_specs=[pl.BlockSpec((None, c, d), lambda i: (i, 0, 0)),
                  pl.BlockSpec((None, 1, N_MOD * d), lambda i: (ctx_row, 0, 0)),
                  pl.BlockSpec((1, d), lambda i: (0, 0)),
                  pl.BlockSpec((d, kvw), lambda i: (0, 0))],
        out_specs=pl.BlockSpec((None, c, kvw), lambda i: (i, 0, 0)),
        compiler_params=_cparams("parallel"),
        name="ctx_kv",
    )(ctx, mod3, n1, w_kv)


def _proj_kernel(x_ref, mod_ref, n1_ref, w_ref, cos_ref, sin_ref,
                 q_ref, kv_ref, gb_ref, cu_ref, ga_ref, gv_ref):
    d = x_ref.shape[-1]
    aw = q_ref.shape[-1]
    kw = kv_ref.shape[-1] // 2
    cw = gb_ref.shape[-1]
    mod = mod_ref[...]
    h = (_rmsnorm(x_ref[...], n1_ref[...]) * (1.0 + mod[:, d:2 * d]) + mod[:, 0:d]).astype(BF16)
    cos = cos_ref[...]
    sin = sin_ref[...]
    lane = lax.broadcasted_iota(jnp.int32, cos.shape, 1)
    first_half = (lane % (2 * ROPE_PAIRS)) < ROPE_PAIRS

    def rope(z):
        partner = jnp.where(first_half, pltpu.roll(z, LANES - ROPE_PAIRS, 1),
                            pltpu.roll(z, ROPE_PAIRS, 1))
        return z * cos + partner * sin

    def proj(lo, width):
        return jnp.dot(h, w_ref[:, lo:lo + width], preferred_element_type=F32)

    off = 0
    zq = proj(off, aw)
    for g in range(aw // LANES):
        q_ref[:, g * LANES:(g + 1) * LANES] = rope(zq[:, g * LANES:(g + 1) * LANES]).astype(BF16)
    off += aw
    zkv = proj(off, 2 * kw)
    for g in range(kw // LANES):
        kv_ref[:, g * LANES:(g + 1) * LANES] = rope(zkv[:, g * LANES:(g + 1) * LANES]).astype(BF16)
    kv_ref[:, kw:] = zkv[:, kw:].astype(BF16)
    off += 2 * kw
    gb_ref[...] = proj(off, cw)
    off += cw
    zc = proj(off, cw)
    off += cw
    cu_ref[...] = zc * proj(off, cw)
    off += cw
    ga_ref[...] = proj(off, d)
    off += d
    gv_ref[...] = proj(off, d)


def _proj(x2, mod3, n1, w_in, cos, sin, seq, tm, b0, nb):
    d = x2.shape[1]
    t = nb * seq
    aw = N_Q_HEADS * HEAD_DIM
    kw = N_KV_HEADS * HEAD_DIM
    cw = d // 2
    per_seq = seq // tm
    row = lambda i: (i, 0)
    return pl.pallas_call(
        _proj_kernel,
        out_shape=(jax.ShapeDtypeStruct((t, aw), BF16),
                   jax.ShapeDtypeStruct((t, 2 * kw), BF16),
                   jax.ShapeDtypeStruct((t, cw), F32),
                   jax.ShapeDtypeStruct((t, cw), F32),
                   jax.ShapeDtypeStruct((t, d), F32),
                   jax.ShapeDtypeStruct((t, d), F32)),
        grid=(t // tm,),
        in_specs=[pl.BlockSpec((tm, d), lambda i: (b0 * per_seq + i, 0)),
                  pl.BlockSpec((None, 1, N_MOD * d), lambda i: (b0 + i // per_seq, 0, 0)),
                  pl.BlockSpec((1, d), lambda i: (0, 0)),
                  pl.BlockSpec(w_in.shape, lambda i: (0, 0)),
                  pl.BlockSpec((tm, LANES), lambda i: (i % per_seq, 0)),
                  pl.BlockSpec((tm, LANES), lambda i: (i % per_seq, 0))],
        out_specs=(pl.BlockSpec((tm, aw), row), pl.BlockSpec((tm, 2 * kw), row),
                   pl.BlockSpec((tm, cw), row), pl.BlockSpec((tm, cw), row),
                   pl.BlockSpec((tm, d), row), pl.BlockSpec((tm, d), row)),
        compiler_params=_cparams("parallel"),
        name="proj",
    )(x2, mod3, n1, w_in, cos, sin)


def _mixer_kernel(sink_ref, x_ref, q_ref, kv_ref, kvp_ref, kvn_ref, kvc_ref,
                  cu_ref, cup_ref, cun_ref, gb_ref, ga_ref, gv_ref, convw_ref,
                  wa_ref, wc_ref, wm_ref, mod_ref, o_ref, attn_scr):
    n = pl.program_id(1)
    has_prev = n > 0
    has_next = n < pl.num_programs(1) - 1
    tq, d = x_ref.shape
    kw = N_KV_HEADS * HEAD_DIM
    scale = HEAD_DIM ** -0.5
    nt = (((1,), (1,)), ((), ()))

    kext = jnp.concatenate([kvp_ref[...], kv_ref[...], kvn_ref[...]], axis=0)
    kctx = kvc_ref[...]
    qi = lax.broadcasted_iota(jnp.int32, (WINDOW, WINDOW), 0)
    ki = lax.broadcasted_iota(jnp.int32, (WINDOW, WINDOW), 1)
    nblk = tq // WINDOW
    for j in range(nblk):
        prev_ok = ki >= qi
        next_ok = ki <= qi
        if j == 0:
            prev_ok = jnp.logical_and(prev_ok, has_prev)
        if j == nblk - 1:
            next_ok = jnp.logical_and(next_ok, has_next)
        mask = jnp.concatenate([prev_ok, jnp.ones_like(prev_ok), next_ok], axis=1)
        rows = slice(j * WINDOW, (j + 1) * WINDOW)
        krows = slice(j * WINDOW, (j + 3) * WINDOW)
        for hq in range(N_Q_HEADS):
            kvh = hq // Q_PER_KV
            kcol = slice(kvh * HEAD_DIM, (kvh + 1) * HEAD_DIM)
            vcol = slice(kw + kvh * HEAD_DIM, kw + (kvh + 1) * HEAD_DIM)
            qh = q_ref[rows, hq * HEAD_DIM:(hq + 1) * HEAD_DIM]
            s_loc = lax.dot_general(qh, kext[krows, kcol], nt, preferred_element_type=F32) * scale
            s_loc = jnp.where(mask, s_loc, NEG_INF)
            s_ctx = lax.dot_general(qh, kctx[:, kcol], nt, preferred_element_type=F32) * scale
            sink = sink_ref[0, hq]
            m = jnp.maximum(jnp.max(s_loc, axis=-1, keepdims=True),
                            jnp.max(s_ctx, axis=-1, keepdims=True))
            m = jnp.maximum(m, sink)
            p_loc = jnp.exp(s_loc - m)
            p_ctx = jnp.exp(s_ctx - m)
            den = (jnp.sum(p_loc, axis=-1, keepdims=True) + jnp.sum(p_ctx, axis=-1, keepdims=True)
                   + jnp.exp(sink - m))
            o = (jnp.dot(p_loc.astype(BF16), kext[krows, vcol], preferred_element_type=F32)
                 + jnp.dot(p_ctx.astype(BF16), kctx[:, vcol], preferred_element_type=F32))
            attn_scr[rows, hq * HEAD_DIM:(hq + 1) * HEAD_DIM] = o / den

    y_attn = jnp.dot(attn_scr[...].astype(BF16), wa_ref[...], preferred_element_type=F32)

    cu = cu_ref[...]
    ri = lax.broadcasted_iota(jnp.int32, cu.shape, 0)
    prev_row = jnp.where(has_prev, cup_ref[7:8, :], 0.0)
    next_row = jnp.where(has_next, cun_ref[0:1, :], 0.0)
    cu_m1 = jnp.where(ri == 0, prev_row, pltpu.roll(cu, 1, 0))
    cu_p1 = jnp.where(ri == tq - 1, next_row, pltpu.roll(cu, tq - 1, 0))
    cw = convw_ref[...]
    conv = cu_m1 * cw[0:1, :] + cu * cw[1:2, :] + cu_p1 * cw[2:3, :]
    y_conv = jnp.dot((gb_ref[...] * conv).astype(BF16), wc_ref[...], preferred_element_type=F32)

    merged = jax.nn.sigmoid(ga_ref[...]) * y_attn + jax.nn.sigmoid(gv_ref[...]) * y_conv
    y = jnp.dot(merged.astype(BF16), wm_ref[...], preferred_element_type=F32)
    g1 = mod_ref[:, 2 * d:3 * d]
    o_ref[...] = x_ref[...] + g1 * y


def _mixer(x2, q, kv, kvc, cu, gb, ga, gv, sink, conv_w, wa, wc, wm, mod3, seq, tq, b0, batch):
    d = x2.shape[1]
    t, aw = q.shape
    kv2 = kv.shape[1]
    cw = cu.shape[1]
    c = kvc.shape[1]
    nq = seq // tq
    nb = seq // WINDOW
    sub = tq // WINDOW
    kv3 = kv.reshape(t // WINDOW, WINDOW, kv2)
    cu3 = cu.reshape(t // 8, 8, cw)
    tile = lambda b, n: (b * nq + n, 0)
    const = lambda b, n: (0, 0)
    return pl.pallas_call(
        _mixer_kernel,
        out_shape=jax.ShapeDtypeStruct((t, d), F32),
        grid=(batch, nq),
        in_specs=[
            pl.BlockSpec(memory_space=pltpu.SMEM),
            pl.BlockSpec((tq, d), lambda b, n: ((b0 + b) * nq + n, 0)),
            pl.BlockSpec((tq, aw), tile),
            pl.BlockSpec((tq, kv2), tile),
            pl.BlockSpec((None, WINDOW, kv2), lambda b, n: (b * nb + jnp.maximum(n * sub - 1, 0), 0, 0)),
            pl.BlockSpec((None, WINDOW, kv2), lambda b, n: (b * nb + jnp.minimum((n + 1) * sub, nb - 1), 0, 0)),
            pl.BlockSpec((None, c, kv2), lambda b, n: (b0 + b, 0, 0)),
            pl.BlockSpec((tq, cw), tile),
            pl.BlockSpec((None, 8, cw), lambda b, n: (jnp.maximum((b * seq + n * tq) // 8 - 1, 0), 0, 0)),
            pl.BlockSpec((None, 8, cw), lambda b, n: (jnp.minimum((b * seq + (n + 1) * tq) // 8, t // 8 - 1), 0, 0)),
            pl.BlockSpec((tq, cw), tile),
            pl.BlockSpec((tq, d), tile),
            pl.BlockSpec((tq, d), tile),
            pl.BlockSpec(conv_w.shape, const),
            pl.BlockSpec(wa.shape, const),
            pl.BlockSpec(wc.shape, const),
            pl.BlockSpec(wm.shape, const),
            pl.BlockSpec((None, 1, N_MOD * d), lambda b, n: (b0 + b, 0, 0)),
        ],
        out_specs=pl.BlockSpec((tq, d), tile),
        scratch_shapes=[pltpu.VMEM((tq, aw), F32)],
        compiler_params=_cparams("parallel", "parallel"),
        name="mixer",
    )(sink, x2, q, kv, kv3, kv3, kvc, cu, cu3, cu3, gb, ga, gv, conv_w, wa, wc, wm, mod3)


def _topk_rows(s, k, payload=None):
    n = s.shape[0]
    iota = lax.broadcasted_iota(jnp.int32, s.shape, 0)
    vals, ids = [], []
    for _ in range(k):
        m = jnp.max(s, axis=0, keepdims=True)
        am = jnp.min(jnp.where(s == m, iota, n), axis=0, keepdims=True)
        hit = iota == am
        vals.append(m)
        if payload is None:
            ids.append(am)
        else:
            ids.append(jnp.max(jnp.where(hit, payload, -1), axis=0, keepdims=True))
        s = jnp.where(hit, -jnp.inf, s)
    return jnp.concatenate(vals, axis=0), jnp.concatenate(ids, axis=0)


def _route_kernel(x1_ref, mod_ref, n2_ref, wq_ref, keys_ref, h2_ref, idx_ref, gate_ref, h2b_scr):
    d = x1_ref.shape[-1]
    nt = (((1,), (1,)), ((), ()))

    @pl.when(pl.program_id(1) == 0)
    def _():
        mod = mod_ref[...]
        h2 = _rmsnorm(x1_ref[...], n2_ref[...]) * (1.0 + mod[:, 4 * d:5 * d]) + mod[:, 3 * d:4 * d]
        h2_ref[...] = h2
        h2b_scr[...] = h2.astype(BF16)

    qp = jnp.dot(h2b_scr[...], wq_ref[...], preferred_element_type=F32)
    half = qp.shape[1] // 2
    tops = []
    for p in range(2):
        qh = qp[:, p * half:(p + 1) * half].astype(BF16)
        st = lax.dot_general(keys_ref[p], qh, nt, preferred_element_type=F32)
        tops.append(_topk_rows(st, PEER_TOPK))
    (a, ia), (b, ib) = tops
    cand = jnp.concatenate([a[i:i + 1, :] + b for i in range(PEER_TOPK)], axis=0)
    cidx = jnp.concatenate([ia[i:i + 1, :] * PEER_N_KEYS + ib for i in range(PEER_TOPK)], axis=0)
    best, idx = _topk_rows(cand, PEER_TOPK, payload=cidx)
    e = jnp.exp(best - best[0:1, :])
    gate_ref[...] = e / jnp.sum(e, axis=0, keepdims=True)
    idx_ref[...] = idx


def _route(x1, mod3, n2, wq, keys, seq, tm, b0):
    t, d = x1.shape
    hw = wq.shape[1] // PEER_HEADS
    per_seq = seq // tm
    return pl.pallas_call(
        _route_kernel,
        out_shape=(jax.ShapeDtypeStruct((t, d), F32),
                   jax.ShapeDtypeStruct((PEER_HEADS, PEER_TOPK, t), jnp.int32),
                   jax.ShapeDtypeStruct((PEER_HEADS, PEER_TOPK, t), F32)),
        grid=(t // tm, PEER_HEADS),
        in_specs=[pl.BlockSpec((tm, d), lambda i, h: (i, 0)),
                  pl.BlockSpec((None, 1, N_MOD * d), lambda i, h: (b0 + i // per_seq, 0, 0)),
                  pl.BlockSpec((1, d), lambda i, h: (0, 0)),
                  pl.BlockSpec((d, hw), lambda i, h: (0, h)),
                  pl.BlockSpec((None, 2, PEER_N_KEYS, hw // 2), lambda i, h: (h, 0, 0, 0))],
        out_specs=(pl.BlockSpec((tm, d), lambda i, h: (i, 0)),
                   pl.BlockSpec((None, PEER_TOPK, tm), lambda i, h: (h, 0, i)),
                   pl.BlockSpec((None, PEER_TOPK, tm), lambda i, h: (h, 0, i))),
        scratch_shapes=[pltpu.VMEM((tm, d), BF16)],
        compiler_params=_cparams("parallel", "arbitrary"),
        name="route",
    )(x1, mod3, n2, wq, keys)


SC_CORES = 2
SC_SUBCORES = 16
SC_LANES = 16
SC_WORKERS = SC_CORES * SC_SUBCORES
SC_ROWS_U = 32
SC_ROWS_V = 64
SC_ROWS_PER_ITER = 8
SC_TOKENS = 8


def _sc_mesh():
    return plsc.VectorSubcoreMesh(core_axis_name="c", subcore_axis_name="s")


def _sc_params():
    return dataclasses.replace(pltpu.CompilerParams(), needs_layout_passes=False)


def _sc_worker_base(tokens_per_worker):
    return (lax.axis_index("s") * SC_CORES + lax.axis_index("c")) * tokens_per_worker


def _sc_chunk_pipeline(tab_hbm, idx_v, bufs, n_chunks, compute):
    rows = bufs[0][0].shape[0]

    def gather(g, b):
        buf, sem = bufs[b]
        return pltpu.make_async_copy(tab_hbm.at[idx_v.at[pl.ds(g * rows, rows)]], buf, sem)

    gather(0, 0).start()

    @pl.loop(0, n_chunks, step=2)
    def _(g):
        gather(g, 0).wait()
        gather(g + 1, 1).start()
        compute(g, bufs[0][0])
        gather(g + 1, 1).wait()

        @pl.when(g + 2 < n_chunks)
        def _():
            gather(g + 2, 0).start()

        compute(g + 1, bufs[1][0])


def _pack_table(tab):
    half = tab.shape[1] // 2
    lo = lax.bitcast_convert_type(tab[:, :half].astype(BF16), jnp.uint16).astype(jnp.uint32)
    hi = lax.bitcast_convert_type(tab[:, half:].astype(BF16), jnp.uint16).astype(jnp.uint32)
    return lax.bitcast_convert_type((hi << 16) | lo, jnp.int32)


def _unpack_lo(x):
    return lax.bitcast_convert_type(x << 16, F32)


def _unpack_hi(x):
    return lax.bitcast_convert_type(x & jnp.int32(-65536), F32)


def _peer_u_sc(u_pk, idx_flat, h2, t, k):
    d = h2.shape[1]
    words = u_pk.shape[1]
    tpw = t // SC_WORKERS
    rows = SC_ROWS_U
    cpt = k // rows
    cpb = SC_TOKENS * cpt
    nj = words // SC_LANES

    def body(u_hbm, idx_hbm, h_hbm, o_hbm, idx_v, h_v, buf0, buf1, acc_v, pre_v, sem0, sem1):
        base = _sc_worker_base(tpw)
        lanes = lax.iota(jnp.int32, SC_LANES)

        def compute(g, buf):
            tl = g // cpt
            c = g % cpt

            @plsc.parallel_loop(0, rows // SC_ROWS_PER_ITER)
            def _(it):
                r0 = it * SC_ROWS_PER_ITER
                accs = [[None, None] for _ in range(SC_ROWS_PER_ITER)]
                for j in range(nj):
                    h_lo = h_v[tl, pl.ds(j * SC_LANES, SC_LANES)]
                    h_hi = h_v[tl, pl.ds(words + j * SC_LANES, SC_LANES)]
                    for a in range(SC_ROWS_PER_ITER):
                        x = buf[r0 + a, pl.ds(j * SC_LANES, SC_LANES)]
                        term = _unpack_lo(x) * h_lo + _unpack_hi(x) * h_hi
                        accs[a][j % 2] = term if accs[a][j % 2] is None else accs[a][j % 2] + term
                for a in range(SC_ROWS_PER_ITER):
                    acc_v[r0 + a, :] = accs[a][0] + accs[a][1]

            for q in range(rows // SC_LANES):
                s = plsc.load_gather(acc_v, [lanes + q * SC_LANES, jnp.zeros((SC_LANES,), jnp.int32)])
                for l in range(1, SC_LANES):
                    s = s + plsc.load_gather(acc_v, [lanes + q * SC_LANES,
                                                     jnp.full((SC_LANES,), l, jnp.int32)])
                pre_v[tl, pl.ds(c * rows + q * SC_LANES, SC_LANES)] = s

        @pl.loop(0, tpw // SC_TOKENS)
        def _(blk):
            tok0 = base + blk * SC_TOKENS
            pltpu.sync_copy(idx_hbm.at[pl.ds(tok0 * k, SC_TOKENS * k)], idx_v)
            pltpu.sync_copy(h_hbm.at[pl.ds(tok0, SC_TOKENS)], h_v)
            _sc_chunk_pipeline(u_hbm, idx_v, ((buf0, sem0), (buf1, sem1)), cpb, compute)
            pltpu.sync_copy(pre_v, o_hbm.at[pl.ds(tok0, SC_TOKENS)])

    return pl.kernel(
        body,
        out_type=jax.ShapeDtypeStruct((t, k), F32),
        mesh=_sc_mesh(),
        scratch_types=[pltpu.VMEM((SC_TOKENS * k,), jnp.int32),
                       pltpu.VMEM((SC_TOKENS, d), F32),
                       pltpu.VMEM((rows, words), jnp.int32),
                       pltpu.VMEM((rows, words), jnp.int32),
                       pltpu.VMEM((rows, SC_LANES), F32),
                       pltpu.VMEM((SC_TOKENS, k), F32),
                       pltpu.SemaphoreType.DMA,
                       pltpu.SemaphoreType.DMA],
        compiler_params=_sc_params(),
        name="peer_u_sc",
    )(u_pk, idx_flat, h2)


def _peer_v_sc(v_pk, idx_flat, w_flat, t, k):
    words = v_pk.shape[1]
    d = 2 * words
    tpw = t // SC_WORKERS
    rows = SC_ROWS_V
    cpt = k // rows
    cpb = SC_TOKENS * cpt
    pw = words // 2
    nj = pw // SC_LANES

    def body(v_hbm, idx_hbm, w_hbm, o_hbm, idx_v, w_v, buf0, buf1, out_v, sem0, sem1):
        base = _sc_worker_base(tpw)

        def compute(g, buf):
            tl = g // cpt
            c = g % cpt
            for p in range(2):
                lo_cols = [pl.ds(p * pw + j * SC_LANES, SC_LANES) for j in range(nj)]
                hi_cols = [pl.ds(words + p * pw + j * SC_LANES, SC_LANES) for j in range(nj)]

                def row(r, accs):
                    wb = plsc.load_gather(w_v, [jnp.full((SC_LANES,), g * rows + r, jnp.int32)])
                    new = []
                    for j in range(nj):
                        x = buf[r, lo_cols[j]]
                        new.append(accs[2 * j] + _unpack_lo(x) * wb)
                        new.append(accs[2 * j + 1] + _unpack_hi(x) * wb)
                    return tuple(new)

                init = []
                for j in range(nj):
                    init.append(jnp.where(c == 0, 0.0, out_v[tl, lo_cols[j]]))
                    init.append(jnp.where(c == 0, 0.0, out_v[tl, hi_cols[j]]))
                accs = lax.fori_loop(0, rows, row, tuple(init))
                for j in range(nj):
                    out_v[tl, lo_cols[j]] = accs[2 * j]
                    out_v[tl, hi_cols[j]] = accs[2 * j + 1]

        @pl.loop(0, tpw // SC_TOKENS)
        def _(blk):
            tok0 = base + blk * SC_TOKENS
            pltpu.sync_copy(idx_hbm.at[pl.ds(tok0 * k, SC_TOKENS * k)], idx_v)
            pltpu.sync_copy(w_hbm.at[pl.ds(tok0 * k, SC_TOKENS * k)], w_v)
            _sc_chunk_pipeline(v_hbm, idx_v, ((buf0, sem0), (buf1, sem1)), cpb, compute)
            pltpu.sync_copy(out_v, o_hbm.at[pl.ds(tok0, SC_TOKENS)])

    return pl.kernel(
        body,
        out_type=jax.ShapeDtypeStruct((t, d), F32),
        mesh=_sc_mesh(),
        scratch_types=[pltpu.VMEM((SC_TOKENS * k,), jnp.int32),
                       pltpu.VMEM((SC_TOKENS * k,), F32),
                       pltpu.VMEM((rows, words), jnp.int32),
                       pltpu.VMEM((rows, words), jnp.int32),
                       pltpu.VMEM((SC_TOKENS, d), F32),
                       pltpu.SemaphoreType.DMA,
                       pltpu.SemaphoreType.DMA],
        compiler_params=_sc_params(),
        name="peer_v_sc",
    )(v_pk, idx_flat, w_flat)


def _act_kernel(gate_ref, pre_ref, o_ref):
    o_ref[...] = gate_ref[...] * _gelu_exact(pre_ref[...])


def _act(gate, pre, tile):
    t, k = pre.shape
    spec = pl.BlockSpec((tile, k), lambda i: (i, 0))
    return pl.pallas_call(
        _act_kernel,
        out_shape=jax.ShapeDtypeStruct((t, k), F32),
        grid=(t // tile,),
        in_specs=[spec, spec],
        out_specs=spec,
        compiler_params=_cparams("parallel"),
        name="expert_act",
    )(gate, pre)


def _final_kernel(x1_ref, y_ref, mod_ref, fg_ref, out_so_far_ref, o_ref):
    del out_so_far_ref
    d = x1_ref.shape[-1]
    x2 = x1_ref[...] + mod_ref[:, 5 * d:6 * d] * y_ref[...]
    o_ref[...] = _rmsnorm(x2, fg_ref[...])


def _final(x1, y, mod3, fg, out_so_far, seq, tile, b0):
    t, d = y.shape
    per_seq = seq // tile
    row = pl.BlockSpec((tile, d), lambda i: (i, 0))
    return pl.pallas_call(
        _final_kernel,
        out_shape=jax.ShapeDtypeStruct(out_so_far.shape, F32),
        grid=(t // tile,),
        in_specs=[row, row,
                  pl.BlockSpec((None, 1, N_MOD * d), lambda i: (b0 + i // per_seq, 0, 0)),
                  pl.BlockSpec((1, d), lambda i: (0, 0)),
                  pl.BlockSpec(memory_space=pl.ANY)],
        out_specs=pl.BlockSpec((tile, d), lambda i: (b0 * per_seq + i, 0)),
        input_output_aliases={4: 0},
        compiler_params=_cparams("parallel"),
        name="final",
    )(x1, y, mod3, fg, out_so_far)


def _peer_sc(idx, gate, h2, u_pk, v_pk, tile, t):
    k = idx.shape[1]
    assert t % (SC_WORKERS * SC_TOKENS) == 0 and k % (2 * SC_ROWS_U) == 0 and k % (2 * SC_ROWS_V) == 0
    idx_flat = idx.reshape(idx.shape[0] * k)
    pre = _peer_u_sc(u_pk, idx_flat, h2, t, k)
    w = _act(gate, pre, tile)
    return _peer_v_sc(v_pk, idx_flat, w.reshape(t * k), t, k)


TC_PEER_TOKENS = 8


def _peer_tc_kernel(idx_ref, idxn_ref, gate_ref, h2_ref, x1_ref, mod_ref, fg_ref, u_hbm, v_hbm,
                    out_so_far_ref, o_ref, ubuf, vbuf, sem):
    del out_so_far_ref
    i = pl.program_id(0)
    tt, k = gate_ref.shape
    d = x1_ref.shape[-1]
    words = d // 2
    slot = i % 2

    def row_copy(tab, buf, which, sl, src_row, dst_row):
        return pltpu.make_async_copy(tab.at[pl.ds(src_row, 1)], buf.at[sl, pl.ds(dst_row, 1)],
                                     sem.at[which, sl])

    def issue(iref, sl):
        def tok(t, carry):
            for j in range(k):
                row = iref[t, j]
                row_copy(u_hbm, ubuf, 0, sl, row, t * k + j).start(priority=j % 2)
                row_copy(v_hbm, vbuf, 1, sl, row, t * k + j).start(priority=(j + 1) % 2)
            return carry
        lax.fori_loop(0, tt, tok, 0)

    @pl.when(i == 0)
    def _():
        issue(idx_ref, 0)

    @pl.when(i + 1 < pl.num_programs(0))
    def _():
        issue(idxn_ref, 1 - slot)

    pltpu.make_async_copy(u_hbm.at[pl.ds(0, tt * k)], ubuf.at[slot], sem.at[0, slot]).wait()
    pltpu.make_async_copy(v_hbm.at[pl.ds(0, tt * k)], vbuf.at[slot], sem.at[1, slot]).wait()

    def unpack(x):
        return (lax.bitcast_convert_type(x << 16, F32),
                lax.bitcast_convert_type(x & jnp.int32(-65536), F32))

    cols = []
    for t in range(tt):
        lo, hi = unpack(ubuf[slot, t * k:(t + 1) * k, :])
        cols.append(jnp.sum(lo * h2_ref[t:t + 1, :words] + hi * h2_ref[t:t + 1, words:],
                            axis=1, keepdims=True))
    pre = jnp.concatenate(cols, axis=1)
    w = gate_ref[...].T * _gelu_exact(pre)
    outs = []
    for t in range(tt):
        lo, hi = unpack(vbuf[slot, t * k:(t + 1) * k, :])
        wt = w[:, t:t + 1]
        outs.append(jnp.concatenate([jnp.sum(wt * lo, axis=0, keepdims=True),
                                     jnp.sum(wt * hi, axis=0, keepdims=True)], axis=1))
    y = jnp.concatenate(outs, axis=0)
    x2 = x1_ref[...] + mod_ref[:, 5 * d:6 * d] * y
    o_ref[...] = _rmsnorm(x2, fg_ref[...])


def _peer_tc(idx, gate, h2, x1, mod3, fg, u_pk, v_pk, out_so_far, seq, b0, t0):
    tg, k = idx.shape
    d = x1.shape[1]
    tt = TC_PEER_TOKENS
    n = (tg - t0) // tt
    first = t0 // tt
    per_seq = seq // tt
    row = lambda i: (first + i, 0)
    return pl.pallas_call(
        _peer_tc_kernel,
        out_shape=jax.ShapeDtypeStruct(out_so_far.shape, F32),
        grid=(n,),
        in_specs=[pl.BlockSpec((tt, k), row, memory_space=pltpu.SMEM),
                  pl.BlockSpec((tt, k), lambda i: (first + jnp.minimum(i + 1, n - 1), 0),
                               memory_space=pltpu.SMEM),
                  pl.BlockSpec((tt, k), row),
                  pl.BlockSpec((tt, d), row),
                  pl.BlockSpec((tt, d), row),
                  pl.BlockSpec((None, 1, N_MOD * d), lambda i: (b0 + (first + i) // per_seq, 0, 0)),
                  pl.BlockSpec((1, d), lambda i: (0, 0)),
                  pl.BlockSpec(memory_space=pl.ANY),
                  pl.BlockSpec(memory_space=pl.ANY),
                  pl.BlockSpec(memory_space=pl.ANY)],
        out_specs=pl.BlockSpec((tt, d), lambda i: (b0 * per_seq + first + i, 0)),
        scratch_shapes=[pltpu.VMEM((2, tt * k, d // 2), jnp.int32),
                        pltpu.VMEM((2, tt * k, d // 2), jnp.int32),
                        pltpu.SemaphoreType.DMA((2, 2))],
        input_output_aliases={9: 0},
        compiler_params=_cparams("arbitrary"),
        name="peer_tc",
    )(idx, idx, gate, h2, x1, mod3, fg, u_pk, v_pk, out_so_far)


def _rope_tables(length):
    rows = length // GRID_W
    row = jnp.repeat(jnp.arange(rows, dtype=F32), GRID_W)
    col = jnp.tile(jnp.arange(GRID_W, dtype=F32), rows)
    inv_freq = ROPE_BASE ** (-jnp.arange(ROPE_PAIRS, dtype=F32) / ROPE_PAIRS)
    ang_r = row[:, None] * inv_freq
    ang_c = col[:, None] * inv_freq
    cos = jnp.concatenate([jnp.cos(ang_r)] * 2 + [jnp.cos(ang_c)] * 2, axis=-1)
    sin = jnp.concatenate([-jnp.sin(ang_r), jnp.sin(ang_r), -jnp.sin(ang_c), jnp.sin(ang_c)], axis=-1)
    reps = LANES // HEAD_DIM
    return jnp.tile(cos, (1, reps)), jnp.tile(sin, (1, reps))


def _layer(x, c, ctx, c_ctx, w_mod, b_mod, n1, n2, w_in, sink, conv_w, w_attn_out, w_conv_out,
           w_mix_out, pw_q, p_keys, p_u, p_v, final_g, tm, tq, tr, tt, groups):
    batch, seq, d = x.shape
    t = batch * seq
    aw = N_Q_HEADS * HEAD_DIM
    kw = N_KV_HEADS * HEAD_DIM

    rows = -(-(batch + 1) // 8) * 8
    cond = jnp.zeros((rows, d), F32).at[:batch].set(c).at[batch].set(c_ctx)
    mod3 = _adaln(cond, w_mod, b_mod).reshape(rows, 1, N_MOD * d)

    w_in_b = w_in.astype(BF16)
    kvc = _ctx_kv(ctx, mod3, batch, n1.reshape(1, d), w_in_b[:, aw:aw + 2 * kw])

    cos, sin = _rope_tables(seq)
    x2 = x.reshape(t, d)
    wa, wc, wm = w_attn_out.astype(BF16), w_conv_out.astype(BF16), w_mix_out.astype(BF16)
    wq, keys = pw_q.astype(BF16), p_keys.astype(BF16)
    u_pk, v_pk = _pack_table(p_u), _pack_table(p_v)
    nsel = PEER_HEADS * PEER_TOPK

    nb = batch // groups
    tg = nb * seq
    t_sc = _sparsecore_share(tg)
    fg = final_g.reshape(1, d)
    out = pl.empty((t, d), F32)
    for g in range(groups):
        b0 = g * nb
        q, kv, gb, cu, ga, gv = _proj(x2, mod3, n1.reshape(1, d), w_in_b, cos, sin, seq, tm, b0, nb)
        x1 = _mixer(x2, q, kv, kvc, cu, gb, ga, gv, sink.reshape(1, N_Q_HEADS), conv_w,
                    wa, wc, wm, mod3, seq, tq, b0, nb)
        h2, idx_t, gate_t = _route(x1, mod3, n2.reshape(1, d), wq, keys, seq, tr, b0)
        idx, gate = idx_t.reshape(nsel, tg).T, gate_t.reshape(nsel, tg).T
        y = _peer_sc(idx, gate, h2, u_pk, v_pk, tt, t_sc)
        if t_sc < tg:
            out = _peer_tc(idx, gate, h2, x1, mod3, fg, u_pk, v_pk, out, seq, b0, t_sc)
        out = _final(x1, y, mod3, fg, out, seq, tt, b0)
    return out.reshape(batch, seq, d)


MAX_TOKEN_GROUPS = 8
SC_SHARE_NUM, SC_SHARE_DEN = 13, 16


def _sparsecore_share(tokens):
    unit = SC_WORKERS * SC_TOKENS
    t_sc = tokens * SC_SHARE_NUM // SC_SHARE_DEN // unit * unit
    if t_sc == 0 or (tokens - t_sc) % TC_PEER_TOKENS:
        return tokens
    return t_sc


def _token_groups(batch, seq):
    for groups in range(min(MAX_TOKEN_GROUPS, batch), 0, -1):
        if batch % groups == 0 and (batch // groups * seq) % (SC_WORKERS * SC_TOKENS) == 0:
            return groups
    raise ValueError("token count must be a multiple of the SparseCore work split")


def kernel(x, c, ctx, c_ctx, w_mod, b_mod, norm1_g, norm2_g, w_in, attn_sink, conv_w, w_attn_out,
           w_conv_out, w_mix_out, peer_w_q, peer_sub_keys, peer_u, peer_v, final_g):
    assert w_mod.shape[0] == 1, "only the single-layer configuration is implemented"
    seq = x.shape[1]
    return _layer(x, c, ctx, c_ctx, w_mod[0], b_mod[0], norm1_g[0], norm2_g[0], w_in[0],
                  attn_sink[0], conv_w[0], w_attn_out[0], w_conv_out[0], w_mix_out[0],
                  peer_w_q[0], peer_sub_keys[0], peer_u[0], peer_v[0], final_g,
                  tm=min(512, seq), tq=min(256, seq), tr=min(256, seq), tt=min(256, seq),
                  groups=_token_groups(x.shape[0], seq))
```

```python
import dataclasses

import jax
import jax.numpy as jnp
from jax import lax
from jax.experimental import pallas as pl
from jax.experimental.pallas import tpu as pltpu
from jax.experimental.pallas import tpu_sc as plsc

HEAD_DIM = 64
N_Q_HEADS = 8
N_KV_HEADS = 2
Q_PER_KV = N_Q_HEADS // N_KV_HEADS
WINDOW = 128
GRID_W = 64
ROPE_BASE = 10000.0
ROPE_PAIRS = HEAD_DIM // 4
PEER_HEADS = 8
PEER_N_KEYS = 128
PEER_TOPK = 16
N_MOD = 6
EPS = 1e-6
NEG_INF = -1e30

LANES = 128
VMEM_LIMIT = 56 * 1024 * 1024

F32 = jnp.float32
BF16 = jnp.bfloat16


def _cparams(*sem):
    return pltpu.CompilerParams(dimension_semantics=sem, vmem_limit_bytes=VMEM_LIMIT)


def _rmsnorm(x, g):
    return x * lax.rsqrt(jnp.mean(x * x, axis=-1, keepdims=True) + EPS) * g


def _gelu_exact(x):
    return 0.5 * x * (1.0 + lax.erf(x * (2.0 ** -0.5)))


def _adaln_kernel(cond_ref, w_ref, b_ref, o_ref):
    act = jax.nn.silu(cond_ref[...])
    o_ref[...] = jnp.dot(act, w_ref[...], precision=lax.Precision.HIGHEST,
                         preferred_element_type=F32) + b_ref[...]


def _adaln(cond, w_mod, b_mod):
    rows, d = cond.shape
    n = w_mod.shape[1]
    tn = d
    return pl.pallas_call(
        _adaln_kernel,
        out_shape=jax.ShapeDtypeStruct((rows, n), F32),
        grid=(n // tn,),
        in_specs=[pl.BlockSpec((rows, d), lambda j: (0, 0)),
                  pl.BlockSpec((d, tn), lambda j: (0, j)),
                  pl.BlockSpec((1, tn), lambda j: (0, j))],
        out_specs=pl.BlockSpec((rows, tn), lambda j: (0, j)),
        compiler_params=_cparams("parallel"),
        name="adaln",
    )(cond, w_mod, b_mod.reshape(1, n))


def _ctx_kv_kernel(xc_ref, mod_ref, n1_ref, w_ref, o_ref):
    d = xc_ref.shape[-1]
    mod = mod_ref[...]
    hc = _rmsnorm(xc_ref[...], n1_ref[...]) * (1.0 + mod[:, d:2 * d]) + mod[:, 0:d]
    o_ref[...] = jnp.dot(hc.astype(BF16), w_ref[...], preferred_element_type=F32).astype(BF16)


def _ctx_kv(ctx, mod3, ctx_row, n1, w_kv):
    b, c, d = ctx.shape
    kvw = w_kv.shape[1]
    return pl.pallas_call(
        _ctx_kv_kernel,
        out_shape=jax.ShapeDtypeStruct((b, c, kvw), BF16),
        grid=(b,),
        in_specs=[pl.BlockSpec((None, c, d), lambda i: (i, 0, 0)),
                  pl.BlockSpec((None, 1, N_MOD * d), lambda i: (ctx_row, 0, 0)),
                  pl.BlockSpec((1, d), lambda i: (0, 0)),
                  pl.BlockSpec((d, kvw), lambda i: (0, 0))],
        out_specs=pl.BlockSpec((None, c, kvw), lambda i: (i, 0, 0)),
        compiler_params=_cparams("parallel"),
        name="ctx_kv",
    )(ctx, mod3, n1, w_kv)


def _proj_kernel(x_ref, mod_ref, n1_ref, w_ref, cos_ref, sin_ref,
                 q_ref, kv_ref, gb_ref, cu_ref, ga_ref, gv_ref):
    d = x_ref.shape[-1]
    aw = q_ref.shape[-1]
    kw = kv_ref.shape[-1] // 2
    cw = gb_ref.shape[-1]
    mod = mod_ref[...]
    h = (_rmsnorm(x_ref[...], n1_ref[...]) * (1.0 + mod[:, d:2 * d]) + mod[:, 0:d]).astype(BF16)
    cos = cos_ref[...]
    sin = sin_ref[...]
    lane = lax.broadcasted_iota(jnp.int32, cos.shape, 1)
    first_half = (lane % (2 * ROPE_PAIRS)) < ROPE_PAIRS

    def rope(z):
        partner = jnp.where(first_half, pltpu.roll(z, LANES - ROPE_PAIRS, 1),
                            pltpu.roll(z, ROPE_PAIRS, 1))
        return z * cos + partner * sin

    def proj(lo, width):
        return jnp.dot(h, w_ref[:, lo:lo + width], preferred_element_type=F32)

    off = 0
    zq = proj(off, aw)
    for g in range(aw // LANES):
        q_ref[:, g * LANES:(g + 1) * LANES] = rope(zq[:, g * LANES:(g + 1) * LANES]).astype(BF16)
    off += aw
    zkv = proj(off, 2 * kw)
    for g in range(kw // LANES):
        kv_ref[:, g * LANES:(g + 1) * LANES] = rope(zkv[:, g * LANES:(g + 1) * LANES]).astype(BF16)
    kv_ref[:, kw:] = zkv[:, kw:].astype(BF16)
    off += 2 * kw
    gb_ref[...] = proj(off, cw)
    off += cw
    zc = proj(off, cw)
    off += cw
    cu_ref[...] = zc * proj(off, cw)
    off += cw
    ga_ref[...] = proj(off, d)
    off += d
    gv_ref[...] = proj(off, d)


def _proj(x2, mod3, n1, w_in, cos, sin, seq, tm, b0, nb):
    d = x2.shape[1]
    t = nb * seq
    aw = N_Q_HEADS * HEAD_DIM
    kw = N_KV_HEADS * HEAD_DIM
    cw = d // 2
    per_seq = seq // tm
    row = lambda i: (i, 0)
    return pl.pallas_call(
        _proj_kernel,
        out_shape=(jax.ShapeDtypeStruct((t, aw), BF16),
                   jax.ShapeDtypeStruct((t, 2 * kw), BF16),
                   jax.ShapeDtypeStruct((t, cw), F32),
                   jax.ShapeDtypeStruct((t, cw), F32),
                   jax.ShapeDtypeStruct((t, d), F32),
                   jax.ShapeDtypeStruct((t, d), F32)),
        grid=(t // tm,),
        in_specs=[pl.BlockSpec((tm, d), lambda i: (b0 * per_seq + i, 0)),
                  pl.BlockSpec((None, 1, N_MOD * d), lambda i: (b0 + i // per_seq, 0, 0)),
                  pl.BlockSpec((1, d), lambda i: (0, 0)),
                  pl.BlockSpec(w_in.shape, lambda i: (0, 0)),
                  pl.BlockSpec((tm, LANES), lambda i: (i % per_seq, 0)),
                  pl.BlockSpec((tm, LANES), lambda i: (i % per_seq, 0))],
        out_specs=(pl.BlockSpec((tm, aw), row), pl.BlockSpec((tm, 2 * kw), row),
                   pl.BlockSpec((tm, cw), row), pl.BlockSpec((tm, cw), row),
                   pl.BlockSpec((tm, d), row), pl.BlockSpec((tm, d), row)),
        compiler_params=_cparams("parallel"),
        name="proj",
    )(x2, mod3, n1, w_in, cos, sin)


def _mixer_kernel(sink_ref, x_ref, q_ref, kv_ref, kvp_ref, kvn_ref, kvc_ref,
                  cu_ref, cup_ref, cun_ref, gb_ref, ga_ref, gv_ref, convw_ref,
                  wa_ref, wc_ref, wm_ref, mod_ref, o_ref, attn_scr):
    n = pl.program_id(1)
    has_prev = n > 0
    has_next = n < pl.num_programs(1) - 1
    tq, d = x_ref.shape
    kw = N_KV_HEADS * HEAD_DIM
    scale = HEAD_DIM ** -0.5
    nt = (((1,), (1,)), ((), ()))

    kext = jnp.concatenate([kvp_ref[...], kv_ref[...], kvn_ref[...]], axis=0)
    kctx = kvc_ref[...]
    qi = lax.broadcasted_iota(jnp.int32, (WINDOW, WINDOW), 0)
    ki = lax.broadcasted_iota(jnp.int32, (WINDOW, WINDOW), 1)
    nblk = tq // WINDOW
    for j in range(nblk):
        prev_ok = ki >= qi
        next_ok = ki <= qi
        if j == 0:
            prev_ok = jnp.logical_and(prev_ok, has_prev)
        if j == nblk - 1:
            next_ok = jnp.logical_and(next_ok, has_next)
        mask = jnp.concatenate([prev_ok, jnp.ones_like(prev_ok), next_ok], axis=1)
        rows = slice(j * WINDOW, (j + 1) * WINDOW)
        krows = slice(j * WINDOW, (j + 3) * WINDOW)
        for hq in range(N_Q_HEADS):
            kvh = hq // Q_PER_KV
            kcol = slice(kvh * HEAD_DIM, (kvh + 1) * HEAD_DIM)
            vcol = slice(kw + kvh * HEAD_DIM, kw + (kvh + 1) * HEAD_DIM)
            qh = q_ref[rows, hq * HEAD_DIM:(hq + 1) * HEAD_DIM]
            s_loc = lax.dot_general(qh, kext[krows, kcol], nt, preferred_element_type=F32) * scale
            s_loc = jnp.where(mask, s_loc, NEG_INF)
            s_ctx = lax.dot_general(qh, kctx[:, kcol], nt, preferred_element_type=F32) * scale
            sink = sink_ref[0, hq]
            m = jnp.maximum(jnp.max(s_loc, axis=-1, keepdims=True),
                            jnp.max(s_ctx, axis=-1, keepdims=True))
            m = jnp.maximum(m, sink)
            p_loc = jnp.exp(s_loc - m)
            p_ctx = jnp.exp(s_ctx - m)
            den = (jnp.sum(p_loc, axis=-1, keepdims=True) + jnp.sum(p_ctx, axis=-1, keepdims=True)
                   + jnp.exp(sink - m))
            o = (jnp.dot(p_loc.astype(BF16), kext[krows, vcol], preferred_element_type=F32)
                 + jnp.dot(p_ctx.astype(BF16), kctx[:, vcol], preferred_element_type=F32))
            attn_scr[rows, hq * HEAD_DIM:(hq + 1) * HEAD_DIM] = o / den

    y_attn = jnp.dot(attn_scr[...].astype(BF16), wa_ref[...], preferred_element_type=F32)

    cu = cu_ref[...]
    ri = lax.broadcasted_iota(jnp.int32, cu.shape, 0)
    prev_row = jnp.where(has_prev, cup_ref[7:8, :], 0.0)
    next_row = jnp.where(has_next, cun_ref[0:1, :], 0.0)
    cu_m1 = jnp.where(ri == 0, prev_row, pltpu.roll(cu, 1, 0))
    cu_p1 = jnp.where(ri == tq - 1, next_row, pltpu.roll(cu, tq - 1, 0))
    cw = convw_ref[...]
    conv = cu_m1 * cw[0:1, :] + cu * cw[1:2, :] + cu_p1 * cw[2:3, :]
    y_conv = jnp.dot((gb_ref[...] * conv).astype(BF16), wc_ref[...], preferred_element_type=F32)

    merged = jax.nn.sigmoid(ga_ref[...]) * y_attn + jax.nn.sigmoid(gv_ref[...]) * y_conv
    y = jnp.dot(merged.astype(BF16), wm_ref[...], preferred_element_type=F32)
    g1 = mod_ref[:, 2 * d:3 * d]
    o_ref[...] = x_ref[...] + g1 * y


def _mixer(x2, q, kv, kvc, cu, gb, ga, gv, sink, conv_w, wa, wc, wm, mod3, seq, tq, b0, batch):
    d = x2.shape[1]
    t, aw = q.shape
    kv2 = kv.shape[1]
    cw = cu.shape[1]
    c = kvc.shape[1]
    nq = seq // tq
    nb = seq // WINDOW
    sub = tq // WINDOW
    kv3 = kv.reshape(t // WINDOW, WINDOW, kv2)
    cu3 = cu.reshape(t // 8, 8, cw)
    tile = lambda b, n: (b * nq + n, 0)
    const = lambda b, n: (0, 0)
    return pl.pallas_call(
        _mixer_kernel,
        out_shape=jax.ShapeDtypeStruct((t, d), F32),
        grid=(batch, nq),
        in_specs=[
            pl.BlockSpec(memory_space=pltpu.SMEM),
            pl.BlockSpec((tq, d), lambda b, n: ((b0 + b) * nq + n, 0)),
            pl.BlockSpec((tq, aw), tile),
            pl.BlockSpec((tq, kv2), tile),
            pl.BlockSpec((None, WINDOW, kv2), lambda b, n: (b * nb + jnp.maximum(n * sub - 1, 0), 0, 0)),
            pl.BlockSpec((None, WINDOW, kv2), lambda b, n: (b * nb + jnp.minimum((n + 1) * sub, nb - 1), 0, 0)),
            pl.BlockSpec((None, c, kv2), lambda b, n: (b0 + b, 0, 0)),
            pl.BlockSpec((tq, cw), tile),
            pl.BlockSpec((None, 8, cw), lambda b, n: (jnp.maximum((b * seq + n * tq) // 8 - 1, 0), 0, 0)),
            pl.BlockSpec((None, 8, cw), lambda b, n: (jnp.minimum((b * seq + (n + 1) * tq) // 8, t // 8 - 1), 0, 0)),
            pl.BlockSpec((tq, cw), tile),
            pl.BlockSpec((tq, d), tile),
            pl.BlockSpec((tq, d), tile),
            pl.BlockSpec(conv_w.shape, const),
            pl.BlockSpec(wa.shape, const),
            pl.BlockSpec(wc.shape, const),
            pl.BlockSpec(wm.shape, const),
            pl.BlockSpec((None, 1, N_MOD * d), lambda b, n: (b0 + b, 0, 0)),
        ],
        out_specs=pl.BlockSpec((tq, d), tile),
        scratch_shapes=[pltpu.VMEM((tq, aw), F32)],
        compiler_params=_cparams("parallel", "parallel"),
        name="mixer",
    )(sink, x2, q, kv, kv3, kv3, kvc, cu, cu3, cu3, gb, ga, gv, conv_w, wa, wc, wm, mod3)


def _topk_rows(s, k, payload=None):
    n = s.shape[0]
    rows = lax.broadcasted_iota(jnp.int32, s.shape, 0).astype(F32)
    vals, picks = [], []
    for _ in range(k):
        m = jnp.max(s, axis=0, keepdims=True)
        am = jnp.min(jnp.where(s == m, rows, float(n)), axis=0, keepdims=True)
        hit = rows == am
        vals.append(m)
        if payload is None:
            picks.append(am)
        else:
            picks.append(jnp.max(jnp.where(hit, payload, -1.0), axis=0, keepdims=True))
        s = jnp.where(hit, -jnp.inf, s)
    return jnp.concatenate(vals, axis=0), jnp.concatenate(picks, axis=0)


def _route_kernel(x1_ref, mod_ref, n2_ref, wq_ref, keys_ref, h2_ref, idx_ref, gate_ref, h2b_scr):
    d = x1_ref.shape[-1]
    nt = (((1,), (1,)), ((), ()))

    @pl.when(pl.program_id(1) == 0)
    def _():
        mod = mod_ref[...]
        h2 = _rmsnorm(x1_ref[...], n2_ref[...]) * (1.0 + mod[:, 4 * d:5 * d]) + mod[:, 3 * d:4 * d]
        h2_ref[...] = h2
        h2b_scr[...] = h2.astype(BF16)

    qp = jnp.dot(h2b_scr[...], wq_ref[...], preferred_element_type=F32)
    half = qp.shape[1] // 2
    tops = []
    for p in range(2):
        qh = qp[:, p * half:(p + 1) * half].astype(BF16)
        st = lax.dot_general(keys_ref[p], qh, nt, preferred_element_type=F32)
        tops.append(_topk_rows(st, PEER_TOPK))
    (a, ia), (b, ib) = tops
    width = [PEER_TOPK // (i + 1) for i in range(PEER_TOPK)]
    pad = -sum(width) % 8
    tm = a.shape[1]
    cand = jnp.concatenate([a[i:i + 1, :] + b[:width[i], :] for i in range(PEER_TOPK)]
                           + [jnp.full((pad, tm), -jnp.inf, F32)], axis=0)
    cidx = jnp.concatenate([ia[i:i + 1, :] * PEER_N_KEYS + ib[:width[i], :] for i in range(PEER_TOPK)]
                           + [jnp.zeros((pad, tm), F32)], axis=0)
    best, idx = _topk_rows(cand, PEER_TOPK, payload=cidx)
    e = jnp.exp(best - best[0:1, :])
    gate_ref[...] = e / jnp.sum(e, axis=0, keepdims=True)
    idx_ref[...] = idx.astype(jnp.int32)


def _route(x1, mod3, n2, wq, keys, seq, tm, b0):
    t, d = x1.shape
    hw = wq.shape[1] // PEER_HEADS
    per_seq = seq // tm
    return pl.pallas_call(
        _route_kernel,
        out_shape=(jax.ShapeDtypeStruct((t, d), F32),
                   jax.ShapeDtypeStruct((PEER_HEADS, PEER_TOPK, t), jnp.int32),
                   jax.ShapeDtypeStruct((PEER_HEADS, PEER_TOPK, t), F32)),
        grid=(t // tm, PEER_HEADS),
        in_specs=[pl.BlockSpec((tm, d), lambda i, h: (i, 0)),
                  pl.BlockSpec((None, 1, N_MOD * d), lambda i, h: (b0 + i // per_seq, 0, 0)),
                  pl.BlockSpec((1, d), lambda i, h: (0, 0)),
                  pl.BlockSpec((d, hw), lambda i, h: (0, h)),
                  pl.BlockSpec((None, 2, PEER_N_KEYS, hw // 2), lambda i, h: (h, 0, 0, 0))],
        out_specs=(pl.BlockSpec((tm, d), lambda i, h: (i, 0)),
                   pl.BlockSpec((None, PEER_TOPK, tm), lambda i, h: (h, 0, i)),
                   pl.BlockSpec((None, PEER_TOPK, tm), lambda i, h: (h, 0, i))),
        scratch_shapes=[pltpu.VMEM((tm, d), BF16)],
        compiler_params=_cparams("parallel", "arbitrary"),
        name="route",
    )(x1, mod3, n2, wq, keys)


SC_CORES = 2
SC_SUBCORES = 16
SC_LANES = 16
SC_WORKERS = SC_CORES * SC_SUBCORES
SC_ROWS_U = 32
SC_ROWS_V = 64
SC_ROWS_PER_ITER = 8
SC_TOKENS = 8


def _sc_mesh():
    return plsc.VectorSubcoreMesh(core_axis_name="c", subcore_axis_name="s")


def _sc_params():
    return dataclasses.replace(pltpu.CompilerParams(), needs_layout_passes=False)


def _sc_worker_base(tokens_per_worker):
    return (lax.axis_index("s") * SC_CORES + lax.axis_index("c")) * tokens_per_worker


def _sc_chunk_pipeline(tab_hbm, idx_v, bufs, n_chunks, compute):
    rows = bufs[0][0].shape[0]

    def gather(g, b):
        buf, sem = bufs[b]
        return pltpu.make_async_copy(tab_hbm.at[idx_v.at[pl.ds(g * rows, rows)]], buf, sem)

    gather(0, 0).start()

    @pl.loop(0, n_chunks, step=2)
    def _(g):
        gather(g, 0).wait()
        gather(g + 1, 1).start()
        compute(g, bufs[0][0])
        gather(g + 1, 1).wait()

        @pl.when(g + 2 < n_chunks)
        def _():
            gather(g + 2, 0).start()

        compute(g + 1, bufs[1][0])


def _pack_table(tab):
    half = tab.shape[1] // 2
    lo = lax.bitcast_convert_type(tab[:, :half].astype(BF16), jnp.uint16).astype(jnp.uint32)
    hi = lax.bitcast_convert_type(tab[:, half:].astype(BF16), jnp.uint16).astype(jnp.uint32)
    return lax.bitcast_convert_type((hi << 16) | lo, jnp.int32)


def _unpack_lo(x):
    return lax.bitcast_convert_type(x << 16, F32)


def _unpack_hi(x):
    return lax.bitcast_convert_type(x & jnp.int32(-65536), F32)


def _peer_u_sc(u_pk, idx_flat, h2, t, k):
    d = h2.shape[1]
    words = u_pk.shape[1]
    tpw = t // SC_WORKERS
    rows = SC_ROWS_U
    cpt = k // rows
    cpb = SC_TOKENS * cpt
    nj = words // SC_LANES

    def body(u_hbm, idx_hbm, h_hbm, o_hbm, idx_v, h_v, buf0, buf1, acc_v, pre_v, sem0, sem1):
        base = _sc_worker_base(tpw)
        lanes = lax.iota(jnp.int32, SC_LANES)

        def compute(g, buf):
            tl = g // cpt
            c = g % cpt

            @plsc.parallel_loop(0, rows // SC_ROWS_PER_ITER)
            def _(it):
                r0 = it * SC_ROWS_PER_ITER
                accs = [[None, None] for _ in range(SC_ROWS_PER_ITER)]
                for j in range(nj):
                    h_lo = h_v[tl, pl.ds(j * SC_LANES, SC_LANES)]
                    h_hi = h_v[tl, pl.ds(words + j * SC_LANES, SC_LANES)]
                    for a in range(SC_ROWS_PER_ITER):
                        x = buf[r0 + a, pl.ds(j * SC_LANES, SC_LANES)]
                        term = _unpack_lo(x) * h_lo + _unpack_hi(x) * h_hi
                        accs[a][j % 2] = term if accs[a][j % 2] is None else accs[a][j % 2] + term
                for a in range(SC_ROWS_PER_ITER):
                    acc_v[r0 + a, :] = accs[a][0] + accs[a][1]

            for q in range(rows // SC_LANES):
                s = plsc.load_gather(acc_v, [lanes + q * SC_LANES, jnp.zeros((SC_LANES,), jnp.int32)])
                for l in range(1, SC_LANES):
                    s = s + plsc.load_gather(acc_v, [lanes + q * SC_LANES,
                                                     jnp.full((SC_LANES,), l, jnp.int32)])
                pre_v[tl, pl.ds(c * rows + q * SC_LANES, SC_LANES)] = s

        @pl.loop(0, tpw // SC_TOKENS)
        def _(blk):
            tok0 = base + blk * SC_TOKENS
            pltpu.sync_copy(idx_hbm.at[pl.ds(tok0 * k, SC_TOKENS * k)], idx_v)
            pltpu.sync_copy(h_hbm.at[pl.ds(tok0, SC_TOKENS)], h_v)
            _sc_chunk_pipeline(u_hbm, idx_v, ((buf0, sem0), (buf1, sem1)), cpb, compute)
            pltpu.sync_copy(pre_v, o_hbm.at[pl.ds(tok0, SC_TOKENS)])

    return pl.kernel(
        body,
        out_type=jax.ShapeDtypeStruct((t, k), F32),
        mesh=_sc_mesh(),
        scratch_types=[pltpu.VMEM((SC_TOKENS * k,), jnp.int32),
                       pltpu.VMEM((SC_TOKENS, d), F32),
                       pltpu.VMEM((rows, words), jnp.int32),
                       pltpu.VMEM((rows, words), jnp.int32),
                       pltpu.VMEM((rows, SC_LANES), F32),
                       pltpu.VMEM((SC_TOKENS, k), F32),
                       pltpu.SemaphoreType.DMA,
                       pltpu.SemaphoreType.DMA],
        compiler_params=_sc_params(),
        name="peer_u_sc",
    )(u_pk, idx_flat, h2)


def _peer_v_sc(v_pk, idx_flat, w_flat, t, k):
    words = v_pk.shape[1]
    d = 2 * words
    tpw = t // SC_WORKERS
    rows = SC_ROWS_V
    cpt = k // rows
    cpb = SC_TOKENS * cpt
    pw = words // 2
    nj = pw // SC_LANES

    def body(v_hbm, idx_hbm, w_hbm, o_hbm, idx_v, w_v, buf0, buf1, out_v, sem0, sem1):
        base = _sc_worker_base(tpw)

        def compute(g, buf):
            tl = g // cpt
            c = g % cpt
            for p in range(2):
                lo_cols = [pl.ds(p * pw + j * SC_LANES, SC_LANES) for j in range(nj)]
                hi_cols = [pl.ds(words + p * pw + j * SC_LANES, SC_LANES) for j in range(nj)]

                def row(r, accs):
                    wb = plsc.load_gather(w_v, [jnp.full((SC_LANES,), g * rows + r, jnp.int32)])
                    new = []
                    for j in range(nj):
                        x = buf[r, lo_cols[j]]
                        new.append(accs[2 * j] + _unpack_lo(x) * wb)
                        new.append(accs[2 * j + 1] + _unpack_hi(x) * wb)
                    return tuple(new)

                init = []
                for j in range(nj):
                    init.append(jnp.where(c == 0, 0.0, out_v[tl, lo_cols[j]]))
                    init.append(jnp.where(c == 0, 0.0, out_v[tl, hi_cols[j]]))
                accs = lax.fori_loop(0, rows, row, tuple(init))
                for j in range(nj):
                    out_v[tl, lo_cols[j]] = accs[2 * j]
                    out_v[tl, hi_cols[j]] = accs[2 * j + 1]

        @pl.loop(0, tpw // SC_TOKENS)
        def _(blk):
            tok0 = base + blk * SC_TOKENS
            pltpu.sync_copy(idx_hbm.at[pl.ds(tok0 * k, SC_TOKENS * k)], idx_v)
            pltpu.sync_copy(w_hbm.at[pl.ds(tok0 * k, SC_TOKENS * k)], w_v)
            _sc_chunk_pipeline(v_hbm, idx_v, ((buf0, sem0), (buf1, sem1)), cpb, compute)
            pltpu.sync_copy(out_v, o_hbm.at[pl.ds(tok0, SC_TOKENS)])

    return pl.kernel(
        body,
        out_type=jax.ShapeDtypeStruct((t, d), F32),
        mesh=_sc_mesh(),
        scratch_types=[pltpu.VMEM((SC_TOKENS * k,), jnp.int32),
                       pltpu.VMEM((SC_TOKENS * k,), F32),
                       pltpu.VMEM((rows, words), jnp.int32),
                       pltpu.VMEM((rows, words), jnp.int32),
                       pltpu.VMEM((SC_TOKENS, d), F32),
                       pltpu.SemaphoreType.DMA,
                       pltpu.SemaphoreType.DMA],
        compiler_params=_sc_params(),
        name="peer_v_sc",
    )(v_pk, idx_flat, w_flat)


def _act_kernel(gate_ref, pre_ref, o_ref):
    o_ref[...] = gate_ref[...] * _gelu_exact(pre_ref[...])


def _act(gate, pre, tile):
    t, k = pre.shape
    spec = pl.BlockSpec((tile, k), lambda i: (i, 0))
    return pl.pallas_call(
        _act_kernel,
        out_shape=jax.ShapeDtypeStruct((t, k), F32),
        grid=(t // tile,),
        in_specs=[spec, spec],
        out_specs=spec,
        compiler_params=_cparams("parallel"),
        name="expert_act",
    )(gate, pre)


def _final_kernel(x1_ref, y_ref, mod_ref, fg_ref, out_so_far_ref, o_ref):
    del out_so_far_ref
    d = x1_ref.shape[-1]
    x2 = x1_ref[...] + mod_ref[:, 5 * d:6 * d] * y_ref[...]
    o_ref[...] = _rmsnorm(x2, fg_ref[...])


def _final(x1, y, mod3, fg, out_so_far, seq, tile, b0):
    t, d = y.shape
    per_seq = seq // tile
    row = pl.BlockSpec((tile, d), lambda i: (i, 0))
    return pl.pallas_call(
        _final_kernel,
        out_shape=jax.ShapeDtypeStruct(out_so_far.shape, F32),
        grid=(t // tile,),
        in_specs=[row, row,
                  pl.BlockSpec((None, 1, N_MOD * d), lambda i: (b0 + i // per_seq, 0, 0)),
                  pl.BlockSpec((1, d), lambda i: (0, 0)),
                  pl.BlockSpec(memory_space=pl.ANY)],
        out_specs=pl.BlockSpec((tile, d), lambda i: (b0 * per_seq + i, 0)),
        input_output_aliases={4: 0},
        compiler_params=_cparams("parallel"),
        name="final",
    )(x1, y, mod3, fg, out_so_far)


def _peer_sc(idx, gate, h2, u_pk, v_pk, tile, t):
    k = idx.shape[1]
    assert t % (SC_WORKERS * SC_TOKENS) == 0 and k % (2 * SC_ROWS_U) == 0 and k % (2 * SC_ROWS_V) == 0
    idx_flat = idx.reshape(idx.shape[0] * k)
    pre = _peer_u_sc(u_pk, idx_flat, h2, t, k)
    w = _act(gate, pre, tile)
    return _peer_v_sc(v_pk, idx_flat, w.reshape(t * k), t, k)


TC_PEER_TOKENS = 8


def _peer_tc_kernel(idx_ref, idxn_ref, gate_ref, h2_ref, x1_ref, mod_ref, fg_ref, u_hbm, v_hbm,
                    out_so_far_ref, o_ref, ubuf, vbuf, sem):
    del out_so_far_ref
    i = pl.program_id(0)
    tt, k = gate_ref.shape
    d = x1_ref.shape[-1]
    words = d // 2
    slot = i % 2

    def row_copy(tab, buf, which, sl, src_row, dst_row):
        return pltpu.make_async_copy(tab.at[pl.ds(src_row, 1)], buf.at[sl, pl.ds(dst_row, 1)],
                                     sem.at[which, sl])

    def issue(iref, sl):
        def tok(t, carry):
            for j in range(k):
                row = iref[t, j]
                row_copy(u_hbm, ubuf, 0, sl, row, t * k + j).start(priority=j % 2)
                row_copy(v_hbm, vbuf, 1, sl, row, t * k + j).start(priority=(j + 1) % 2)
            return carry
        lax.fori_loop(0, tt, tok, 0)

    @pl.when(i == 0)
    def _():
        issue(idx_ref, 0)

    @pl.when(i + 1 < pl.num_programs(0))
    def _():
        issue(idxn_ref, 1 - slot)

    pltpu.make_async_copy(u_hbm.at[pl.ds(0, tt * k)], ubuf.at[slot], sem.at[0, slot]).wait()
    pltpu.make_async_copy(v_hbm.at[pl.ds(0, tt * k)], vbuf.at[slot], sem.at[1, slot]).wait()

    def unpack(x):
        return (lax.bitcast_convert_type(x << 16, F32),
                lax.bitcast_convert_type(x & jnp.int32(-65536), F32))

    cols = []
    for t in range(tt):
        lo, hi = unpack(ubuf[slot, t * k:(t + 1) * k, :])
        cols.append(jnp.sum(lo * h2_ref[t:t + 1, :words] + hi * h2_ref[t:t + 1, words:],
                            axis=1, keepdims=True))
    pre = jnp.concatenate(cols, axis=1)
    w = gate_ref[...].T * _gelu_exact(pre)
    outs = []
    for t in range(tt):
        lo, hi = unpack(vbuf[slot, t * k:(t + 1) * k, :])
        wt = w[:, t:t + 1]
        outs.append(jnp.concatenate([jnp.sum(wt * lo, axis=0, keepdims=True),
                                     jnp.sum(wt * hi, axis=0, keepdims=True)], axis=1))
    y = jnp.concatenate(outs, axis=0)
    x2 = x1_ref[...] + mod_ref[:, 5 * d:6 * d] * y
    o_ref[...] = _rmsnorm(x2, fg_ref[...])


def _peer_tc(idx, gate, h2, x1, mod3, fg, u_pk, v_pk, out_so_far, seq, b0, t0):
    tg, k = idx.shape
    d = x1.shape[1]
    tt = TC_PEER_TOKENS
    n = (tg - t0) // tt
    first = t0 // tt
    per_seq = seq // tt
    row = lambda i: (first + i, 0)
    return pl.pallas_call(
        _peer_tc_kernel,
        out_shape=jax.ShapeDtypeStruct(out_so_far.shape, F32),
        grid=(n,),
        in_specs=[pl.BlockSpec((tt, k), row, memory_space=pltpu.SMEM),
                  pl.BlockSpec((tt, k), lambda i: (first + jnp.minimum(i + 1, n - 1), 0),
                               memory_space=pltpu.SMEM),
                  pl.BlockSpec((tt, k), row),
                  pl.BlockSpec((tt, d), row),
                  pl.BlockSpec((tt, d), row),
                  pl.BlockSpec((None, 1, N_MOD * d), lambda i: (b0 + (first + i) // per_seq, 0, 0)),
                  pl.BlockSpec((1, d), lambda i: (0, 0)),
                  pl.BlockSpec(memory_space=pl.ANY),
                  pl.BlockSpec(memory_space=pl.ANY),
                  pl.BlockSpec(memory_space=pl.ANY)],
        out_specs=pl.BlockSpec((tt, d), lambda i: (b0 * per_seq + first + i, 0)),
        scratch_shapes=[pltpu.VMEM((2, tt * k, d // 2), jnp.int32),
                        pltpu.VMEM((2, tt * k, d // 2), jnp.int32),
                        pltpu.SemaphoreType.DMA((2, 2))],
        input_output_aliases={9: 0},
        compiler_params=_cparams("arbitrary"),
        name="peer_tc",
    )(idx, idx, gate, h2, x1, mod3, fg, u_pk, v_pk, out_so_far)


def _rope_tables(length):
    rows = length // GRID_W
    row = jnp.repeat(jnp.arange(rows, dtype=F32), GRID_W)
    col = jnp.tile(jnp.arange(GRID_W, dtype=F32), rows)
    inv_freq = ROPE_BASE ** (-jnp.arange(ROPE_PAIRS, dtype=F32) / ROPE_PAIRS)
    ang_r = row[:, None] * inv_freq
    ang_c = col[:, None] * inv_freq
    cos = jnp.concatenate([jnp.cos(ang_r)] * 2 + [jnp.cos(ang_c)] * 2, axis=-1)
    sin = jnp.concatenate([-jnp.sin(ang_r), jnp.sin(ang_r), -jnp.sin(ang_c), jnp.sin(ang_c)], axis=-1)
    reps = LANES // HEAD_DIM
    return jnp.tile(cos, (1, reps)), jnp.tile(sin, (1, reps))


def _layer(x, c, ctx, c_ctx, w_mod, b_mod, n1, n2, w_in, sink, conv_w, w_attn_out, w_conv_out,
           w_mix_out, pw_q, p_keys, p_u, p_v, final_g, tm, tq, tr, tt, groups):
    batch, seq, d = x.shape
    t = batch * seq
    aw = N_Q_HEADS * HEAD_DIM
    kw = N_KV_HEADS * HEAD_DIM

    rows = -(-(batch + 1) // 8) * 8
    cond = jnp.zeros((rows, d), F32).at[:batch].set(c).at[batch].set(c_ctx)
    mod3 = _adaln(cond, w_mod, b_mod).reshape(rows, 1, N_MOD * d)

    w_in_b = w_in.astype(BF16)
    kvc = _ctx_kv(ctx, mod3, batch, n1.reshape(1, d), w_in_b[:, aw:aw + 2 * kw])

    cos, sin = _rope_tables(seq)
    x2 = x.reshape(t, d)
    wa, wc, wm = w_attn_out.astype(BF16), w_conv_out.astype(BF16), w_mix_out.astype(BF16)
    wq, keys = pw_q.astype(BF16), p_keys.astype(BF16)
    u_pk, v_pk = _pack_table(p_u), _pack_table(p_v)
    nsel = PEER_HEADS * PEER_TOPK

    nb = batch // groups
    tg = nb * seq
    t_sc = _sparsecore_share(tg)
    fg = final_g.reshape(1, d)
    out = pl.empty((t, d), F32)
    for g in range(groups):
        b0 = g * nb
        q, kv, gb, cu, ga, gv = _proj(x2, mod3, n1.reshape(1, d), w_in_b, cos, sin, seq, tm, b0, nb)
        x1 = _mixer(x2, q, kv, kvc, cu, gb, ga, gv, sink.reshape(1, N_Q_HEADS), conv_w,
                    wa, wc, wm, mod3, seq, tq, b0, nb)
        h2, idx_t, gate_t = _route(x1, mod3, n2.reshape(1, d), wq, keys, seq, tr, b0)
        idx, gate = idx_t.reshape(nsel, tg).T, gate_t.reshape(nsel, tg).T
        y = _peer_sc(idx, gate, h2, u_pk, v_pk, tt, t_sc)
        if t_sc < tg:
            out = _peer_tc(idx, gate, h2, x1, mod3, fg, u_pk, v_pk, out, seq, b0, t_sc)
        out = _final(x1, y, mod3, fg, out, seq, tt, b0)
    return out.reshape(batch, seq, d)


MAX_TOKEN_GROUPS = 8
SC_SHARE_NUM, SC_SHARE_DEN = 13, 16


def _sparsecore_share(tokens):
    unit = SC_WORKERS * SC_TOKENS
    t_sc = tokens * SC_SHARE_NUM // SC_SHARE_DEN // unit * unit
    if t_sc == 0 or (tokens - t_sc) % TC_PEER_TOKENS:
        return tokens
    return t_sc


def _token_groups(batch, seq):
    for groups in range(min(MAX_TOKEN_GROUPS, batch), 0, -1):
        if batch % groups == 0 and (batch // groups * seq) % (SC_WORKERS * SC_TOKENS) == 0:
            return groups
    raise ValueError("token count must be a multiple of the SparseCore work split")


def kernel(x, c, ctx, c_ctx, w_mod, b_mod, norm1_g, norm2_g, w_in, attn_sink, conv_w, w_attn_out,
           w_conv_out, w_mix_out, peer_w_q, peer_sub_keys, peer_u, peer_v, final_g):
    assert w_mod.shape[0] == 1, "only the single-layer configuration is implemented"
    seq = x.shape[1]
    return _layer(x, c, ctx, c_ctx, w_mod[0], b_mod[0], norm1_g[0], norm2_g[0], w_in[0],
                  attn_sink[0], conv_w[0], w_attn_out[0], w_conv_out[0], w_mix_out[0],
                  peer_w_q[0], peer_sub_keys[0], peer_u[0], peer_v[0], final_g,
                  tm=min(512, seq), tq=min(256, seq), tr=min(256, seq), tt=min(256, seq),
                  groups=_token_groups(x.shape[0], seq))
```

```python
import dataclasses

import jax
import jax.numpy as jnp
from jax import lax
from jax.experimental import pallas as pl
from jax.experimental.pallas import tpu as pltpu
from jax.experimental.pallas import tpu_sc as plsc

HEAD_DIM = 64
N_Q_HEADS = 8
N_KV_HEADS = 2
Q_PER_KV = N_Q_HEADS // N_KV_HEADS
WINDOW = 128
GRID_W = 64
ROPE_BASE = 10000.0
ROPE_PAIRS = HEAD_DIM // 4
PEER_HEADS = 8
PEER_N_KEYS = 128
PEER_TOPK = 16
N_MOD = 6
EPS = 1e-6
NEG_INF = -1e30

LANES = 128
VMEM_LIMIT = 56 * 1024 * 1024

F32 = jnp.float32
BF16 = jnp.bfloat16


def _cparams(*sem):
    return pltpu.CompilerParams(dimension_semantics=sem, vmem_limit_bytes=VMEM_LIMIT)


def _rmsnorm(x, g):
    return x * lax.rsqrt(jnp.mean(x * x, axis=-1, keepdims=True) + EPS) * g


def _gelu_exact(x):
    return 0.5 * x * (1.0 + lax.erf(x * (2.0 ** -0.5)))


def _adaln_kernel(cond_ref, w_ref, b_ref, o_ref):
    act = jax.nn.silu(cond_ref[...])
    o_ref[...] = jnp.dot(act, w_ref[...], precision=lax.Precision.HIGHEST,
                         preferred_element_type=F32) + b_ref[...]


def _adaln(cond, w_mod, b_mod):
    rows, d = cond.shape
    n = w_mod.shape[1]
    tn = d
    return pl.pallas_call(
        _adaln_kernel,
        out_shape=jax.ShapeDtypeStruct((rows, n), F32),
        grid=(n // tn,),
        in_specs=[pl.BlockSpec((rows, d), lambda j: (0, 0)),
                  pl.BlockSpec((d, tn), lambda j: (0, j)),
                  pl.BlockSpec((1, tn), lambda j: (0, j))],
        out_specs=pl.BlockSpec((rows, tn), lambda j: (0, j)),
        compiler_params=_cparams("parallel"),
        name="adaln",
    )(cond, w_mod, b_mod.reshape(1, n))


def _ctx_kv_kernel(xc_ref, mod_ref, n1_ref, w_ref, o_ref):
    d = xc_ref.shape[-1]
    mod = mod_ref[...]
    hc = _rmsnorm(xc_ref[...], n1_ref[...]) * (1.0 + mod[:, d:2 * d]) + mod[:, 0:d]
    o_ref[...] = jnp.dot(hc.astype(BF16), w_ref[...], preferred_element_type=F32).astype(BF16)


def _ctx_kv(ctx, mod3, ctx_row, n1, w_kv):
    b, c, d = ctx.shape
    kvw = w_kv.shape[1]
    return pl.pallas_call(
        _ctx_kv_kernel,
        out_shape=jax.ShapeDtypeStruct((b, c, kvw), BF16),
        grid=(b,),
        in_specs=[pl.BlockSpec((None, c, d), lambda i: (i, 0, 0)),
                  pl.BlockSpec((None, 1, N_MOD * d), lambda i: (ctx_row, 0, 0)),
                  pl.BlockSpec((1, d), lambda i: (0, 0)),
                  pl.BlockSpec((d, kvw), lambda i: (0, 0))],
        out_specs=pl.BlockSpec((None, c, kvw), lambda i: (i, 0, 0)),
        compiler_params=_cparams("parallel"),
        name="ctx_kv",
    )(ctx, mod3, n1, w_kv)


def _proj_kernel(x_ref, mod_ref, n1_ref, w_ref, cos_ref, sin_ref,
                 q_ref, kv_ref, gb_ref, cu_ref, ga_ref, gv_ref):
    d = x_ref.shape[-1]
    aw = q_ref.shape[-1]
    kw = kv_ref.shape[-1] // 2
    cw = gb_ref.shape[-1]
    mod = mod_ref[...]
    h = (_rmsnorm(x_ref[...], n1_ref[...]) * (1.0 + mod[:, d:2 * d]) + mod[:, 0:d]).astype(BF16)
    cos = cos_ref[...]
    sin = sin_ref[...]
    lane = lax.broadcasted_iota(jnp.int32, cos.shape, 1)
    first_half = (lane % (2 * ROPE_PAIRS)) < ROPE_PAIRS

    def rope(z):
        partner = jnp.where(first_half, pltpu.roll(z, LANES - ROPE_PAIRS, 1),
                            pltpu.roll(z, ROPE_PAIRS, 1))
        return z * cos + partner * sin

    def proj(lo, width):
        return jnp.dot(h, w_ref[:, lo:lo + width], preferred_element_type=F32)

    off = 0
    zq = proj(off, aw)
    for g in range(aw // LANES):
        q_ref[:, g * LANES:(g + 1) * LANES] = rope(zq[:, g * LANES:(g + 1) * LANES]).astype(BF16)
    off += aw
    zkv = proj(off, 2 * kw)
    for g in range(kw // LANES):
        kv_ref[:, g * LANES:(g + 1) * LANES] = rope(zkv[:, g * LANES:(g + 1) * LANES]).astype(BF16)
    kv_ref[:, kw:] = zkv[:, kw:].astype(BF16)
    off += 2 * kw
    gb_ref[...] = proj(off, cw)
    off += cw
    zc = proj(off, cw)
    off += cw
    cu_ref[...] = zc * proj(off, cw)
    off += cw
    ga_ref[...] = proj(off, d)
    off += d
    gv_ref[...] = proj(off, d)


def _proj(x2, mod3, n1, w_in, cos, sin, seq, tm, b0, nb):
    d = x2.shape[1]
    t = nb * seq
    aw = N_Q_HEADS * HEAD_DIM
    kw = N_KV_HEADS * HEAD_DIM
    cw = d // 2
    per_seq = seq // tm
    row = lambda i: (i, 0)
    return pl.pallas_call(
        _proj_kernel,
        out_shape=(jax.ShapeDtypeStruct((t, aw), BF16),
                   jax.ShapeDtypeStruct((t, 2 * kw), BF16),
                   jax.ShapeDtypeStruct((t, cw), F32),
                   jax.ShapeDtypeStruct((t, cw), F32),
                   jax.ShapeDtypeStruct((t, d), F32),
                   jax.ShapeDtypeStruct((t, d), F32)),
        grid=(t // tm,),
        in_specs=[pl.BlockSpec((tm, d), lambda i: (b0 * per_seq + i, 0)),
                  pl.BlockSpec((None, 1, N_MOD * d), lambda i: (b0 + i // per_seq, 0, 0)),
                  pl.BlockSpec((1, d), lambda i: (0, 0)),
                  pl.BlockSpec(w_in.shape, lambda i: (0, 0)),
                  pl.BlockSpec((tm, LANES), lambda i: (i % per_seq, 0)),
                  pl.BlockSpec((tm, LANES), lambda i: (i % per_seq, 0))],
        out_specs=(pl.BlockSpec((tm, aw), row), pl.BlockSpec((tm, 2 * kw), row),
                   pl.BlockSpec((tm, cw), row), pl.BlockSpec((tm, cw), row),
                   pl.BlockSpec((tm, d), row), pl.BlockSpec((tm, d), row)),
        compiler_params=_cparams("parallel"),
        name="proj",
    )(x2, mod3, n1, w_in, cos, sin)


def _mixer_kernel(sink_ref, x_ref, q_ref, kv_ref, kvp_ref, kvn_ref, kvc_ref,
                  cu_ref, cup_ref, cun_ref, gb_ref, ga_ref, gv_ref, convw_ref,
                  wa_ref, wc_ref, wm_ref, mod_ref, o_ref, attn_scr):
    n = pl.program_id(1)
    has_prev = n > 0
    has_next = n < pl.num_programs(1) - 1
    tq, d = x_ref.shape
    kw = N_KV_HEADS * HEAD_DIM
    scale = HEAD_DIM ** -0.5
    nt = (((1,), (1,)), ((), ()))

    kext = jnp.concatenate([kvp_ref[...], kv_ref[...], kvn_ref[...]], axis=0)
    kctx = kvc_ref[...]
    qi = lax.broadcasted_iota(jnp.int32, (WINDOW, WINDOW), 0)
    ki = lax.broadcasted_iota(jnp.int32, (WINDOW, WINDOW), 1)
    nblk = tq // WINDOW
    for j in range(nblk):
        prev_ok = ki >= qi
        next_ok = ki <= qi
        if j == 0:
            prev_ok = jnp.logical_and(prev_ok, has_prev)
        if j == nblk - 1:
            next_ok = jnp.logical_and(next_ok, has_next)
        mask = jnp.concatenate([prev_ok, jnp.ones_like(prev_ok), next_ok], axis=1)
        rows = slice(j * WINDOW, (j + 1) * WINDOW)
        krows = slice(j * WINDOW, (j + 3) * WINDOW)
        for hq in range(N_Q_HEADS):
            kvh = hq // Q_PER_KV
            kcol = slice(kvh * HEAD_DIM, (kvh + 1) * HEAD_DIM)
            vcol = slice(kw + kvh * HEAD_DIM, kw + (kvh + 1) * HEAD_DIM)
            qh = q_ref[rows, hq * HEAD_DIM:(hq + 1) * HEAD_DIM]
            s_loc = lax.dot_general(qh, kext[krows, kcol], nt, preferred_element_type=F32) * scale
            s_loc = jnp.where(mask, s_loc, NEG_INF)
            s_ctx = lax.dot_general(qh, kctx[:, kcol], nt, preferred_element_type=F32) * scale
            sink = sink_ref[0, hq]
            m = jnp.maximum(jnp.max(s_loc, axis=-1, keepdims=True),
                            jnp.max(s_ctx, axis=-1, keepdims=True))
            m = jnp.maximum(m, sink)
            p_loc = jnp.exp(s_loc - m)
            p_ctx = jnp.exp(s_ctx - m)
            den = (jnp.sum(p_loc, axis=-1, keepdims=True) + jnp.sum(p_ctx, axis=-1, keepdims=True)
                   + jnp.exp(sink - m))
            o = (jnp.dot(p_loc.astype(BF16), kext[krows, vcol], preferred_element_type=F32)
                 + jnp.dot(p_ctx.astype(BF16), kctx[:, vcol], preferred_element_type=F32))
            attn_scr[rows, hq * HEAD_DIM:(hq + 1) * HEAD_DIM] = o / den

    y_attn = jnp.dot(attn_scr[...].astype(BF16), wa_ref[...], preferred_element_type=F32)

    cu = cu_ref[...]
    ri = lax.broadcasted_iota(jnp.int32, cu.shape, 0)
    prev_row = jnp.where(has_prev, cup_ref[7:8, :], 0.0)
    next_row = jnp.where(has_next, cun_ref[0:1, :], 0.0)
    cu_m1 = jnp.where(ri == 0, prev_row, pltpu.roll(cu, 1, 0))
    cu_p1 = jnp.where(ri == tq - 1, next_row, pltpu.roll(cu, tq - 1, 0))
    cw = convw_ref[...]
    conv = cu_m1 * cw[0:1, :] + cu * cw[1:2, :] + cu_p1 * cw[2:3, :]
    y_conv = jnp.dot((gb_ref[...] * conv).astype(BF16), wc_ref[...], preferred_element_type=F32)

    merged = jax.nn.sigmoid(ga_ref[...]) * y_attn + jax.nn.sigmoid(gv_ref[...]) * y_conv
    y = jnp.dot(merged.astype(BF16), wm_ref[...], preferred_element_type=F32)
    g1 = mod_ref[:, 2 * d:3 * d]
    o_ref[...] = x_ref[...] + g1 * y


def _mixer(x2, q, kv, kvc, cu, gb, ga, gv, sink, conv_w, wa, wc, wm, mod3, seq, tq, b0, batch):
    d = x2.shape[1]
    t, aw = q.shape
    kv2 = kv.shape[1]
    cw = cu.shape[1]
    c = kvc.shape[1]
    nq = seq // tq
    nb = seq // WINDOW
    sub = tq // WINDOW
    kv3 = kv.reshape(t // WINDOW, WINDOW, kv2)
    cu3 = cu.reshape(t // 8, 8, cw)
    tile = lambda b, n: (b * nq + n, 0)
    const = lambda b, n: (0, 0)
    return pl.pallas_call(
        _mixer_kernel,
        out_shape=jax.ShapeDtypeStruct((t, d), F32),
        grid=(batch, nq),
        in_specs=[
            pl.BlockSpec(memory_space=pltpu.SMEM),
            pl.BlockSpec((tq, d), lambda b, n: ((b0 + b) * nq + n, 0)),
            pl.BlockSpec((tq, aw), tile),
            pl.BlockSpec((tq, kv2), tile),
            pl.BlockSpec((None, WINDOW, kv2), lambda b, n: (b * nb + jnp.maximum(n * sub - 1, 0), 0, 0)),
            pl.BlockSpec((None, WINDOW, kv2), lambda b, n: (b * nb + jnp.minimum((n + 1) * sub, nb - 1), 0, 0)),
            pl.BlockSpec((None, c, kv2), lambda b, n: (b0 + b, 0, 0)),
            pl.BlockSpec((tq, cw), tile),
            pl.BlockSpec((None, 8, cw), lambda b, n: (jnp.maximum((b * seq + n * tq) // 8 - 1, 0), 0, 0)),
            pl.BlockSpec((None, 8, cw), lambda b, n: (jnp.minimum((b * seq + (n + 1) * tq) // 8, t // 8 - 1), 0, 0)),
            pl.BlockSpec((tq, cw), tile),
            pl.BlockSpec((tq, d), tile),
            pl.BlockSpec((tq, d), tile),
            pl.BlockSpec(conv_w.shape, const),
            pl.BlockSpec(wa.shape, const),
            pl.BlockSpec(wc.shape, const),
            pl.BlockSpec(wm.shape, const),
            pl.BlockSpec((None, 1, N_MOD * d), lambda b, n: (b0 + b, 0, 0)),
        ],
        out_specs=pl.BlockSpec((tq, d), tile),
        scratch_shapes=[pltpu.VMEM((tq, aw), F32)],
        compiler_params=_cparams("parallel", "parallel"),
        name="mixer",
    )(sink, x2, q, kv, kv3, kv3, kvc, cu, cu3, cu3, gb, ga, gv, conv_w, wa, wc, wm, mod3)


def _topk_rows(s, k, payload=None):
    n = s.shape[0]
    rows = lax.broadcasted_iota(jnp.int32, s.shape, 0).astype(F32)
    vals, picks = [], []
    for _ in range(k):
        m = jnp.max(s, axis=0, keepdims=True)
        am = jnp.min(jnp.where(s == m, rows, float(n)), axis=0, keepdims=True)
        hit = rows == am
        vals.append(m)
        if payload is None:
            picks.append(am)
        else:
            picks.append(jnp.max(jnp.where(hit, payload, -1.0), axis=0, keepdims=True))
        s = jnp.where(hit, -jnp.inf, s)
    return jnp.concatenate(vals, axis=0), jnp.concatenate(picks, axis=0)


def _route_kernel(x1_ref, mod_ref, n2_ref, wq_ref, keys_ref, h2_ref, idx_ref, gate_ref, h2b_scr):
    d = x1_ref.shape[-1]
    nt = (((1,), (1,)), ((), ()))

    @pl.when(pl.program_id(1) == 0)
    def _():
        mod = mod_ref[...]
        h2 = _rmsnorm(x1_ref[...], n2_ref[...]) * (1.0 + mod[:, 4 * d:5 * d]) + mod[:, 3 * d:4 * d]
        h2_ref[...] = h2
        h2b_scr[...] = h2.astype(BF16)

    qp = jnp.dot(h2b_scr[...], wq_ref[...], preferred_element_type=F32)
    half = qp.shape[1] // 2
    tops = []
    for p in range(2):
        qh = qp[:, p * half:(p + 1) * half].astype(BF16)
        st = lax.dot_general(keys_ref[p], qh, nt, preferred_element_type=F32)
        tops.append(_topk_rows(st, PEER_TOPK))
    (a, ia), (b, ib) = tops
    width = [PEER_TOPK // (i + 1) for i in range(PEER_TOPK)]
    pad = -sum(width) % 8
    tm = a.shape[1]
    cand = jnp.concatenate([a[i:i + 1, :] + b[:width[i], :] for i in range(PEER_TOPK)]
                           + [jnp.full((pad, tm), -jnp.inf, F32)], axis=0)
    cidx = jnp.concatenate([ia[i:i + 1, :] * PEER_N_KEYS + ib[:width[i], :] for i in range(PEER_TOPK)]
                           + [jnp.zeros((pad, tm), F32)], axis=0)
    best, idx = _topk_rows(cand, PEER_TOPK, payload=cidx)
    e = jnp.exp(best - best[0:1, :])
    gate_ref[...] = e / jnp.sum(e, axis=0, keepdims=True)
    idx_ref[...] = idx.astype(jnp.int32)


def _route(x1, mod3, n2, wq, keys, seq, tm, b0):
    t, d = x1.shape
    hw = wq.shape[1] // PEER_HEADS
    per_seq = seq // tm
    return pl.pallas_call(
        _route_kernel,
        out_shape=(jax.ShapeDtypeStruct((t, d), F32),
                   jax.ShapeDtypeStruct((PEER_HEADS, PEER_TOPK, t), jnp.int32),
                   jax.ShapeDtypeStruct((PEER_HEADS, PEER_TOPK, t), F32)),
        grid=(t // tm, PEER_HEADS),
        in_specs=[pl.BlockSpec((tm, d), lambda i, h: (i, 0)),
                  pl.BlockSpec((None, 1, N_MOD * d), lambda i, h: (b0 + i // per_seq, 0, 0)),
                  pl.BlockSpec((1, d), lambda i, h: (0, 0)),
                  pl.BlockSpec((d, hw), lambda i, h: (0, h)),
                  pl.BlockSpec((None, 2, PEER_N_KEYS, hw // 2), lambda i, h: (h, 0, 0, 0))],
        out_specs=(pl.BlockSpec((tm, d), lambda i, h: (i, 0)),
                   pl.BlockSpec((None, PEER_TOPK, tm), lambda i, h: (h, 0, i)),
                   pl.BlockSpec((None, PEER_TOPK, tm), lambda i, h: (h, 0, i))),
        scratch_shapes=[pltpu.VMEM((tm, d), BF16)],
        compiler_params=_cparams("parallel", "arbitrary"),
        name="route",
    )(x1, mod3, n2, wq, keys)


SC_CORES = 2
SC_SUBCORES = 16
SC_LANES = 16
SC_WORKERS = SC_CORES * SC_SUBCORES
SC_ROWS_U = 32
SC_ROWS_V = 64
SC_ROWS_PER_ITER = 8
SC_TOKENS = 8


def _sc_mesh():
    return plsc.VectorSubcoreMesh(core_axis_name="c", subcore_axis_name="s")


def _sc_params():
    return dataclasses.replace(pltpu.CompilerParams(), needs_layout_passes=False)


def _sc_worker_base(tokens_per_worker):
    return (lax.axis_index("s") * SC_CORES + lax.axis_index("c")) * tokens_per_worker


def _sc_chunk_pipeline(tab_hbm, idx_v, bufs, n_chunks, compute):
    rows = bufs[0][0].shape[0]

    def gather(g, b):
        buf, sem = bufs[b]
        return pltpu.make_async_copy(tab_hbm.at[idx_v.at[pl.ds(g * rows, rows)]], buf, sem)

    gather(0, 0).start()

    @pl.loop(0, n_chunks, step=2)
    def _(g):
        gather(g, 0).wait()
        gather(g + 1, 1).start()
        compute(g, bufs[0][0])
        gather(g + 1, 1).wait()

        @pl.when(g + 2 < n_chunks)
        def _():
            gather(g + 2, 0).start()

        compute(g + 1, bufs[1][0])


def _pack_table(tab):
    half = tab.shape[1] // 2
    lo = lax.bitcast_convert_type(tab[:, :half].astype(BF16), jnp.uint16).astype(jnp.int32)
    bits = lax.bitcast_convert_type(tab[:, half:], jnp.int32)
    sign = bits & jnp.int32(-2 ** 31)
    magnitude = bits & jnp.int32(2 ** 31 - 1)
    top = jnp.maximum((magnitude - lo + 0x8000) >> 16, 0)
    return sign | ((top << 16) + lo)


def _unpack_lo(x):
    return lax.bitcast_convert_type(x << 16, F32)


def _unpack_hi(x):
    return lax.bitcast_convert_type(x, F32)


def _peer_u_sc(u_pk, idx_flat, h2, t, k):
    d = h2.shape[1]
    words = u_pk.shape[1]
    tpw = t // SC_WORKERS
    rows = SC_ROWS_U
    cpt = k // rows
    cpb = SC_TOKENS * cpt
    nj = words // SC_LANES

    def body(u_hbm, idx_hbm, h_hbm, o_hbm, idx_v, h_v, buf0, buf1, acc_v, pre_v, sem0, sem1):
        base = _sc_worker_base(tpw)
        lanes = lax.iota(jnp.int32, SC_LANES)

        def compute(g, buf):
            tl = g // cpt
            c = g % cpt

            @plsc.parallel_loop(0, rows // SC_ROWS_PER_ITER)
            def _(it):
                r0 = it * SC_ROWS_PER_ITER
                accs = [[None, None] for _ in range(SC_ROWS_PER_ITER)]
                for j in range(nj):
                    h_lo = h_v[tl, pl.ds(j * SC_LANES, SC_LANES)]
                    h_hi = h_v[tl, pl.ds(words + j * SC_LANES, SC_LANES)]
                    for a in range(SC_ROWS_PER_ITER):
                        x = buf[r0 + a, pl.ds(j * SC_LANES, SC_LANES)]
                        term = _unpack_lo(x) * h_lo + _unpack_hi(x) * h_hi
                        accs[a][j % 2] = term if accs[a][j % 2] is None else accs[a][j % 2] + term
                for a in range(SC_ROWS_PER_ITER):
                    acc_v[r0 + a, :] = accs[a][0] + accs[a][1]

            for q in range(rows // SC_LANES):
                s = plsc.load_gather(acc_v, [lanes + q * SC_LANES, jnp.zeros((SC_LANES,), jnp.int32)])
                for l in range(1, SC_LANES):
                    s = s + plsc.load_gather(acc_v, [lanes + q * SC_LANES,
                                                     jnp.full((SC_LANES,), l, jnp.int32)])
                pre_v[tl, pl.ds(c * rows + q * SC_LANES, SC_LANES)] = s

        @pl.loop(0, tpw // SC_TOKENS)
        def _(blk):
            tok0 = base + blk * SC_TOKENS
            pltpu.sync_copy(idx_hbm.at[pl.ds(tok0 * k, SC_TOKENS * k)], idx_v)
            pltpu.sync_copy(h_hbm.at[pl.ds(tok0, SC_TOKENS)], h_v)
            _sc_chunk_pipeline(u_hbm, idx_v, ((buf0, sem0), (buf1, sem1)), cpb, compute)
            pltpu.sync_copy(pre_v, o_hbm.at[pl.ds(tok0, SC_TOKENS)])

    return pl.kernel(
        body,
        out_type=jax.ShapeDtypeStruct((t, k), F32),
        mesh=_sc_mesh(),
        scratch_types=[pltpu.VMEM((SC_TOKENS * k,), jnp.int32),
                       pltpu.VMEM((SC_TOKENS, d), F32),
                       pltpu.VMEM((rows, words), jnp.int32),
                       pltpu.VMEM((rows, words), jnp.int32),
                       pltpu.VMEM((rows, SC_LANES), F32),
                       pltpu.VMEM((SC_TOKENS, k), F32),
                       pltpu.SemaphoreType.DMA,
                       pltpu.SemaphoreType.DMA],
        compiler_params=_sc_params(),
        name="peer_u_sc",
    )(u_pk, idx_flat, h2)


def _peer_v_sc(v_pk, idx_flat, w_flat, t, k):
    words = v_pk.shape[1]
    d = 2 * words
    tpw = t // SC_WORKERS
    rows = SC_ROWS_V
    cpt = k // rows
    cpb = SC_TOKENS * cpt
    pw = words // 2
    nj = pw // SC_LANES

    def body(v_hbm, idx_hbm, w_hbm, o_hbm, idx_v, w_v, buf0, buf1, out_v, sem0, sem1):
        base = _sc_worker_base(tpw)

        def compute(g, buf):
            tl = g // cpt
            c = g % cpt
            for p in range(2):
                lo_cols = [pl.ds(p * pw + j * SC_LANES, SC_LANES) for j in range(nj)]
                hi_cols = [pl.ds(words + p * pw + j * SC_LANES, SC_LANES) for j in range(nj)]

                def row(r, accs):
                    wb = plsc.load_gather(w_v, [jnp.full((SC_LANES,), g * rows + r, jnp.int32)])
                    new = []
                    for j in range(nj):
                        x = buf[r, lo_cols[j]]
                        new.append(accs[2 * j] + _unpack_lo(x) * wb)
                        new.append(accs[2 * j + 1] + _unpack_hi(x) * wb)
                    return tuple(new)

                init = []
                for j in range(nj):
                    init.append(jnp.where(c == 0, 0.0, out_v[tl, lo_cols[j]]))
                    init.append(jnp.where(c == 0, 0.0, out_v[tl, hi_cols[j]]))
                accs = lax.fori_loop(0, rows, row, tuple(init))
                for j in range(nj):
                    out_v[tl, lo_cols[j]] = accs[2 * j]
                    out_v[tl, hi_cols[j]] = accs[2 * j + 1]

        @pl.loop(0, tpw // SC_TOKENS)
        def _(blk):
            tok0 = base + blk * SC_TOKENS
            pltpu.sync_copy(idx_hbm.at[pl.ds(tok0 * k, SC_TOKENS * k)], idx_v)
            pltpu.sync_copy(w_hbm.at[pl.ds(tok0 * k, SC_TOKENS * k)], w_v)
            _sc_chunk_pipeline(v_hbm, idx_v, ((buf0, sem0), (buf1, sem1)), cpb, compute)
            pltpu.sync_copy(out_v, o_hbm.at[pl.ds(tok0, SC_TOKENS)])

    return pl.kernel(
        body,
        out_type=jax.ShapeDtypeStruct((t, d), F32),
        mesh=_sc_mesh(),
        scratch_types=[pltpu.VMEM((SC_TOKENS * k,), jnp.int32),
                       pltpu.VMEM((SC_TOKENS * k,), F32),
                       pltpu.VMEM((rows, words), jnp.int32),
                       pltpu.VMEM((rows, words), jnp.int32),
                       pltpu.VMEM((SC_TOKENS, d), F32),
                       pltpu.SemaphoreType.DMA,
                       pltpu.SemaphoreType.DMA],
        compiler_params=_sc_params(),
        name="peer_v_sc",
    )(v_pk, idx_flat, w_flat)


def _act_kernel(gate_ref, pre_ref, o_ref):
    o_ref[...] = gate_ref[...] * _gelu_exact(pre_ref[...])


def _act(gate, pre, tile):
    t, k = pre.shape
    spec = pl.BlockSpec((tile, k), lambda i: (i, 0))
    return pl.pallas_call(
        _act_kernel,
        out_shape=jax.ShapeDtypeStruct((t, k), F32),
        grid=(t // tile,),
        in_specs=[spec, spec],
        out_specs=spec,
        compiler_params=_cparams("parallel"),
        name="expert_act",
    )(gate, pre)


def _final_kernel(x1_ref, y_ref, mod_ref, fg_ref, out_so_far_ref, o_ref):
    del out_so_far_ref
    d = x1_ref.shape[-1]
    x2 = x1_ref[...] + mod_ref[:, 5 * d:6 * d] * y_ref[...]
    o_ref[...] = _rmsnorm(x2, fg_ref[...])


def _final(x1, y, mod3, fg, out_so_far, seq, tile, b0):
    t, d = y.shape
    per_seq = seq // tile
    row = pl.BlockSpec((tile, d), lambda i: (i, 0))
    return pl.pallas_call(
        _final_kernel,
        out_shape=jax.ShapeDtypeStruct(out_so_far.shape, F32),
        grid=(t // tile,),
        in_specs=[row, row,
                  pl.BlockSpec((None, 1, N_MOD * d), lambda i: (b0 + i // per_seq, 0, 0)),
                  pl.BlockSpec((1, d), lambda i: (0, 0)),
                  pl.BlockSpec(memory_space=pl.ANY)],
        out_specs=pl.BlockSpec((tile, d), lambda i: (b0 * per_seq + i, 0)),
        input_output_aliases={4: 0},
        compiler_params=_cparams("parallel"),
        name="final",
    )(x1, y, mod3, fg, out_so_far)


def _peer_sc(idx, gate, h2, u_pk, v_pk, tile, t):
    k = idx.shape[1]
    assert t % (SC_WORKERS * SC_TOKENS) == 0 and k % (2 * SC_ROWS_U) == 0 and k % (2 * SC_ROWS_V) == 0
    idx_flat = idx.reshape(idx.shape[0] * k)
    pre = _peer_u_sc(u_pk, idx_flat, h2, t, k)
    w = _act(gate, pre, tile)
    return _peer_v_sc(v_pk, idx_flat, w.reshape(t * k), t, k)


TC_PEER_TOKENS = 8


def _peer_tc_kernel(idx_ref, idxn_ref, gate_ref, h2_ref, x1_ref, mod_ref, fg_ref, u_hbm, v_hbm,
                    out_so_far_ref, o_ref, ubuf, vbuf, sem):
    del out_so_far_ref
    i = pl.program_id(0)
    tt, k = gate_ref.shape
    d = x1_ref.shape[-1]
    words = d // 2
    slot = i % 2

    def row_copy(tab, buf, which, sl, src_row, dst_row):
        return pltpu.make_async_copy(tab.at[pl.ds(src_row, 1)], buf.at[sl, pl.ds(dst_row, 1)],
                                     sem.at[which, sl])

    def issue(iref, sl):
        def tok(t, carry):
            for j in range(k):
                row = iref[t, j]
                row_copy(u_hbm, ubuf, 0, sl, row, t * k + j).start(priority=j % 2)
                row_copy(v_hbm, vbuf, 1, sl, row, t * k + j).start(priority=(j + 1) % 2)
            return carry
        lax.fori_loop(0, tt, tok, 0)

    @pl.when(i == 0)
    def _():
        issue(idx_ref, 0)

    @pl.when(i + 1 < pl.num_programs(0))
    def _():
        issue(idxn_ref, 1 - slot)

    pltpu.make_async_copy(u_hbm.at[pl.ds(0, tt * k)], ubuf.at[slot], sem.at[0, slot]).wait()
    pltpu.make_async_copy(v_hbm.at[pl.ds(0, tt * k)], vbuf.at[slot], sem.at[1, slot]).wait()

    def unpack(x):
        return _unpack_lo(x), _unpack_hi(x)

    cols = []
    for t in range(tt):
        lo, hi = unpack(ubuf[slot, t * k:(t + 1) * k, :])
        cols.append(jnp.sum(lo * h2_ref[t:t + 1, :words] + hi * h2_ref[t:t + 1, words:],
                            axis=1, keepdims=True))
    pre = jnp.concatenate(cols, axis=1)
    w = gate_ref[...].T * _gelu_exact(pre)
    outs = []
    for t in range(tt):
        lo, hi = unpack(vbuf[slot, t * k:(t + 1) * k, :])
        wt = w[:, t:t + 1]
        outs.append(jnp.concatenate([jnp.sum(wt * lo, axis=0, keepdims=True),
                                     jnp.sum(wt * hi, axis=0, keepdims=True)], axis=1))
    y = jnp.concatenate(outs, axis=0)
    x2 = x1_ref[...] + mod_ref[:, 5 * d:6 * d] * y
    o_ref[...] = _rmsnorm(x2, fg_ref[...])


def _peer_tc(idx, gate, h2, x1, mod3, fg, u_pk, v_pk, out_so_far, seq, b0, t0):
    tg, k = idx.shape
    d = x1.shape[1]
    tt = TC_PEER_TOKENS
    n = (tg - t0) // tt
    first = t0 // tt
    per_seq = seq // tt
    row = lambda i: (first + i, 0)
    return pl.pallas_call(
        _peer_tc_kernel,
        out_shape=jax.ShapeDtypeStruct(out_so_far.shape, F32),
        grid=(n,),
        in_specs=[pl.BlockSpec((tt, k), row, memory_space=pltpu.SMEM),
                  pl.BlockSpec((tt, k), lambda i: (first + jnp.minimum(i + 1, n - 1), 0),
                               memory_space=pltpu.SMEM),
                  pl.BlockSpec((tt, k), row),
                  pl.BlockSpec((tt, d), row),
                  pl.BlockSpec((tt, d), row),
                  pl.BlockSpec((None, 1, N_MOD * d), lambda i: (b0 + (first + i) // per_seq, 0, 0)),
                  pl.BlockSpec((1, d), lambda i: (0, 0)),
                  pl.BlockSpec(memory_space=pl.ANY),
                  pl.BlockSpec(memory_space=pl.ANY),
                  pl.BlockSpec(memory_space=pl.ANY)],
        out_specs=pl.BlockSpec((tt, d), lambda i: (b0 * per_seq + first + i, 0)),
        scratch_shapes=[pltpu.VMEM((2, tt * k, d // 2), jnp.int32),
                        pltpu.VMEM((2, tt * k, d // 2), jnp.int32),
                        pltpu.SemaphoreType.DMA((2, 2))],
        input_output_aliases={9: 0},
        compiler_params=_cparams("arbitrary"),
        name="peer_tc",
    )(idx, idx, gate, h2, x1, mod3, fg, u_pk, v_pk, out_so_far)


def _rope_tables(length):
    rows = length // GRID_W
    row = jnp.repeat(jnp.arange(rows, dtype=F32), GRID_W)
    col = jnp.tile(jnp.arange(GRID_W, dtype=F32), rows)
    inv_freq = ROPE_BASE ** (-jnp.arange(ROPE_PAIRS, dtype=F32) / ROPE_PAIRS)
    ang_r = row[:, None] * inv_freq
    ang_c = col[:, None] * inv_freq
    cos = jnp.concatenate([jnp.cos(ang_r)] * 2 + [jnp.cos(ang_c)] * 2, axis=-1)
    sin = jnp.concatenate([-jnp.sin(ang_r), jnp.sin(ang_r), -jnp.sin(ang_c), jnp.sin(ang_c)], axis=-1)
    reps = LANES // HEAD_DIM
    return jnp.tile(cos, (1, reps)), jnp.tile(sin, (1, reps))


def _layer(x, c, ctx, c_ctx, w_mod, b_mod, n1, n2, w_in, sink, conv_w, w_attn_out, w_conv_out,
           w_mix_out, pw_q, p_keys, p_u, p_v, final_g, tm, tq, tr, tt, groups):
    batch, seq, d = x.shape
    t = batch * seq
    aw = N_Q_HEADS * HEAD_DIM
    kw = N_KV_HEADS * HEAD_DIM

    rows = -(-(batch + 1) // 8) * 8
    cond = jnp.zeros((rows, d), F32).at[:batch].set(c).at[batch].set(c_ctx)
    mod3 = _adaln(cond, w_mod, b_mod).reshape(rows, 1, N_MOD * d)

    w_in_b = w_in.astype(BF16)
    kvc = _ctx_kv(ctx, mod3, batch, n1.reshape(1, d), w_in_b[:, aw:aw + 2 * kw])

    cos, sin = _rope_tables(seq)
    x2 = x.reshape(t, d)
    wa, wc, wm = w_attn_out.astype(BF16), w_conv_out.astype(BF16), w_mix_out.astype(BF16)
    wq, keys = pw_q.astype(BF16), p_keys.astype(BF16)
    u_pk, v_pk = _pack_table(p_u), _pack_table(p_v)
    nsel = PEER_HEADS * PEER_TOPK

    nb = batch // groups
    tg = nb * seq
    fg = final_g.reshape(1, d)
    out = pl.empty((t, d), F32)
    for g in range(groups):
        b0 = g * nb
        t_sc = _sparsecore_share(tg, g)
        q, kv, gb, cu, ga, gv = _proj(x2, mod3, n1.reshape(1, d), w_in_b, cos, sin, seq, tm, b0, nb)
        x1 = _mixer(x2, q, kv, kvc, cu, gb, ga, gv, sink.reshape(1, N_Q_HEADS), conv_w,
                    wa, wc, wm, mod3, seq, tq, b0, nb)
        h2, idx_t, gate_t = _route(x1, mod3, n2.reshape(1, d), wq, keys, seq, tr, b0)
        idx, gate = idx_t.reshape(nsel, tg).T, gate_t.reshape(nsel, tg).T
        y = _peer_sc(idx, gate, h2, u_pk, v_pk, tt, t_sc)
        if t_sc < tg:
            out = _peer_tc(idx, gate, h2, x1, mod3, fg, u_pk, v_pk, out, seq, b0, t_sc)
        out = _final(x1, y, mod3, fg, out, seq, tt, b0)
    return out.reshape(batch, seq, d)


MAX_TOKEN_GROUPS = 8
SC_SHARE_NUMS, SC_SHARE_DEN = (13, 14), 16


def _sparsecore_share(tokens, group):
    unit = SC_WORKERS * SC_TOKENS
    t_sc = tokens * SC_SHARE_NUMS[group % len(SC_SHARE_NUMS)] // SC_SHARE_DEN // unit * unit
    if t_sc == 0 or (tokens - t_sc) % TC_PEER_TOKENS:
        return tokens
    return t_sc


def _token_groups(batch, seq):
    for groups in range(min(MAX_TOKEN_GROUPS, batch), 0, -1):
        if batch % groups == 0 and (batch // groups * seq) % (SC_WORKERS * SC_TOKENS) == 0:
            return groups
    raise ValueError("token count must be a multiple of the SparseCore work split")


def kernel(x, c, ctx, c_ctx, w_mod, b_mod, norm1_g, norm2_g, w_in, attn_sink, conv_w, w_attn_out,
           w_conv_out, w_mix_out, peer_w_q, peer_sub_keys, peer_u, peer_v, final_g):
    assert w_mod.shape[0] == 1, "only the single-layer configuration is implemented"
    seq = x.shape[1]
    return _layer(x, c, ctx, c_ctx, w_mod[0], b_mod[0], norm1_g[0], norm2_g[0], w_in[0],
                  attn_sink[0], conv_w[0], w_attn_out[0], w_conv_out[0], w_mix_out[0],
                  peer_w_q[0], peer_sub_keys[0], peer_u[0], peer_v[0], final_g,
                  tm=min(512, seq), tq=min(256, seq), tr=min(256, seq), tt=min(256, seq),
                  groups=_token_groups(x.shape[0], seq))
```

```python
import dataclasses

import jax
import jax.numpy as jnp
from jax import lax
from jax.experimental import pallas as pl
from jax.experimental.pallas import tpu as pltpu
from jax.experimental.pallas import tpu_sc as plsc

HEAD_DIM = 64
N_Q_HEADS = 8
N_KV_HEADS = 2
Q_PER_KV = N_Q_HEADS // N_KV_HEADS
WINDOW = 128
GRID_W = 64
ROPE_BASE = 10000.0
ROPE_PAIRS = HEAD_DIM // 4
PEER_HEADS = 8
PEER_N_KEYS = 128
PEER_TOPK = 16
N_MOD = 6
EPS = 1e-6
NEG_INF = -1e30

LANES = 128
VMEM_LIMIT = 56 * 1024 * 1024

F32 = jnp.float32
BF16 = jnp.bfloat16


def _cparams(*sem):
    return pltpu.CompilerParams(dimension_semantics=sem, vmem_limit_bytes=VMEM_LIMIT)


def _rmsnorm(x, g):
    return x * lax.rsqrt(jnp.mean(x * x, axis=-1, keepdims=True) + EPS) * g


def _gelu_exact(x):
    return 0.5 * x * (1.0 + lax.erf(x * (2.0 ** -0.5)))


def _adaln_kernel(cond_ref, w_ref, b_ref, o_ref):
    act = jax.nn.silu(cond_ref[...])
    o_ref[...] = jnp.dot(act, w_ref[...], precision=lax.Precision.HIGHEST,
                         preferred_element_type=F32) + b_ref[...]


def _adaln(cond, w_mod, b_mod):
    rows, d = cond.shape
    n = w_mod.shape[1]
    tn = d
    return pl.pallas_call(
        _adaln_kernel,
        out_shape=jax.ShapeDtypeStruct((rows, n), F32),
        grid=(n // tn,),
        in_specs=[pl.BlockSpec((rows, d), lambda j: (0, 0)),
                  pl.BlockSpec((d, tn), lambda j: (0, j)),
                  pl.BlockSpec((1, tn), lambda j: (0, j))],
        out_specs=pl.BlockSpec((rows, tn), lambda j: (0, j)),
        compiler_params=_cparams("parallel"),
        name="adaln",
    )(cond, w_mod, b_mod.reshape(1, n))


def _ctx_kv_kernel(xc_ref, mod_ref, n1_ref, w_ref, o_ref):
    d = xc_ref.shape[-1]
    mod = mod_ref[...]
    hc = _rmsnorm(xc_ref[...], n1_ref[...]) * (1.0 + mod[:, d:2 * d]) + mod[:, 0:d]
    o_ref[...] = jnp.dot(hc.astype(BF16), w_ref[...], preferred_element_type=F32).astype(BF16)


def _ctx_kv(ctx, mod3, ctx_row, n1, w_kv):
    b, c, d = ctx.shape
    kvw = w_kv.shape[1]
    return pl.pallas_call(
        _ctx_kv_kernel,
        out_shape=jax.ShapeDtypeStruct((b, c, kvw), BF16),
        grid=(b,),
        in_specs=[pl.BlockSpec((None, c, d), lambda i: (i, 0, 0)),
                  pl.BlockSpec((None, 1, N_MOD * d), lambda i: (ctx_row, 0, 0)),
                  pl.BlockSpec((1, d), lambda i: (0, 0)),
                  pl.BlockSpec((d, kvw), lambda i: (0, 0))],
        out_specs=pl.BlockSpec((None, c, kvw), lambda i: (i, 0, 0)),
        compiler_params=_cparams("parallel"),
        name="ctx_kv",
    )(ctx, mod3, n1, w_kv)


def _proj_kernel(x_ref, mod_ref, n1_ref, w_ref, cos_ref, sin_ref,
                 q_ref, kv_ref, gb_ref, cu_ref, ga_ref, gv_ref):
    d = x_ref.shape[-1]
    aw = q_ref.shape[-1]
    kw = kv_ref.shape[-1] // 2
    cw = gb_ref.shape[-1]
    mod = mod_ref[...]
    h = (_rmsnorm(x_ref[...], n1_ref[...]) * (1.0 + mod[:, d:2 * d]) + mod[:, 0:d]).astype(BF16)
    cos = cos_ref[...]
    sin = sin_ref[...]
    lane = lax.broadcasted_iota(jnp.int32, cos.shape, 1)
    first_half = (lane % (2 * ROPE_PAIRS)) < ROPE_PAIRS

    def rope(z):
        partner = jnp.where(first_half, pltpu.roll(z, LANES - ROPE_PAIRS, 1),
                            pltpu.roll(z, ROPE_PAIRS, 1))
        return z * cos + partner * sin

    def proj(lo, width):
        return jnp.dot(h, w_ref[:, lo:lo + width], preferred_element_type=F32)

    off = 0
    zq = proj(off, aw)
    for g in range(aw // LANES):
        q_ref[:, g * LANES:(g + 1) * LANES] = rope(zq[:, g * LANES:(g + 1) * LANES]).astype(BF16)
    off += aw
    zkv = proj(off, 2 * kw)
    for g in range(kw // LANES):
        kv_ref[:, g * LANES:(g + 1) * LANES] = rope(zkv[:, g * LANES:(g + 1) * LANES]).astype(BF16)
    kv_ref[:, kw:] = zkv[:, kw:].astype(BF16)
    off += 2 * kw
    gb_ref[...] = proj(off, cw)
    off += cw
    zc = proj(off, cw)
    off += cw
    cu_ref[...] = zc * proj(off, cw)
    off += cw
    ga_ref[...] = proj(off, d)
    off += d
    gv_ref[...] = proj(off, d)


def _proj(x2, mod3, n1, w_in, cos, sin, seq, tm, b0, nb):
    d = x2.shape[1]
    t = nb * seq
    aw = N_Q_HEADS * HEAD_DIM
    kw = N_KV_HEADS * HEAD_DIM
    cw = d // 2
    per_seq = seq // tm
    row = lambda i: (i, 0)
    return pl.pallas_call(
        _proj_kernel,
        out_shape=(jax.ShapeDtypeStruct((t, aw), BF16),
                   jax.ShapeDtypeStruct((t, 2 * kw), BF16),
                   jax.ShapeDtypeStruct((t, cw), F32),
                   jax.ShapeDtypeStruct((t, cw), F32),
                   jax.ShapeDtypeStruct((t, d), F32),
                   jax.ShapeDtypeStruct((t, d), F32)),
        grid=(t // tm,),
        in_specs=[pl.BlockSpec((tm, d), lambda i: (b0 * per_seq + i, 0)),
                  pl.BlockSpec((None, 1, N_MOD * d), lambda i: (b0 + i // per_seq, 0, 0)),
                  pl.BlockSpec((1, d), lambda i: (0, 0)),
                  pl.BlockSpec(w_in.shape, lambda i: (0, 0)),
                  pl.BlockSpec((tm, LANES), lambda i: (i % per_seq, 0)),
                  pl.BlockSpec((tm, LANES), lambda i: (i % per_seq, 0))],
        out_specs=(pl.BlockSpec((tm, aw), row), pl.BlockSpec((tm, 2 * kw), row),
                   pl.BlockSpec((tm, cw), row), pl.BlockSpec((tm, cw), row),
                   pl.BlockSpec((tm, d), row), pl.BlockSpec((tm, d), row)),
        compiler_params=_cparams("parallel"),
        name="proj",
    )(x2, mod3, n1, w_in, cos, sin)


def _mixer_kernel(sink_ref, x_ref, q_ref, kv_ref, kvp_ref, kvn_ref, kvc_ref,
                  cu_ref, cup_ref, cun_ref, gb_ref, ga_ref, gv_ref, convw_ref,
                  wa_ref, wc_ref, wm_ref, mod_ref, o_ref, attn_scr):
    n = pl.program_id(1)
    has_prev = n > 0
    has_next = n < pl.num_programs(1) - 1
    tq, d = x_ref.shape
    kw = N_KV_HEADS * HEAD_DIM
    scale = HEAD_DIM ** -0.5
    nt = (((1,), (1,)), ((), ()))

    kext = jnp.concatenate([kvp_ref[...], kv_ref[...], kvn_ref[...]], axis=0)
    kctx = kvc_ref[...]
    qi = lax.broadcasted_iota(jnp.int32, (WINDOW, WINDOW), 0)
    ki = lax.broadcasted_iota(jnp.int32, (WINDOW, WINDOW), 1)
    nblk = tq // WINDOW
    for j in range(nblk):
        prev_ok = ki >= qi
        next_ok = ki <= qi
        if j == 0:
            prev_ok = jnp.logical_and(prev_ok, has_prev)
        if j == nblk - 1:
            next_ok = jnp.logical_and(next_ok, has_next)
        mask = jnp.concatenate([prev_ok, jnp.ones_like(prev_ok), next_ok], axis=1)
        rows = slice(j * WINDOW, (j + 1) * WINDOW)
        krows = slice(j * WINDOW, (j + 3) * WINDOW)
        for hq in range(N_Q_HEADS):
            kvh = hq // Q_PER_KV
            kcol = slice(kvh * HEAD_DIM, (kvh + 1) * HEAD_DIM)
            vcol = slice(kw + kvh * HEAD_DIM, kw + (kvh + 1) * HEAD_DIM)
            qh = q_ref[rows, hq * HEAD_DIM:(hq + 1) * HEAD_DIM]
            s_loc = lax.dot_general(qh, kext[krows, kcol], nt, preferred_element_type=F32) * scale
            s_loc = jnp.where(mask, s_loc, NEG_INF)
            s_ctx = lax.dot_general(qh, kctx[:, kcol], nt, preferred_element_type=F32) * scale
            sink = sink_ref[0, hq]
            m = jnp.maximum(jnp.max(s_loc, axis=-1, keepdims=True),
                            jnp.max(s_ctx, axis=-1, keepdims=True))
            m = jnp.maximum(m, sink)
            p_loc = jnp.exp(s_loc - m)
            p_ctx = jnp.exp(s_ctx - m)
            den = (jnp.sum(p_loc, axis=-1, keepdims=True) + jnp.sum(p_ctx, axis=-1, keepdims=True)
                   + jnp.exp(sink - m))
            o = (jnp.dot(p_loc.astype(BF16), kext[krows, vcol], preferred_element_type=F32)
                 + jnp.dot(p_ctx.astype(BF16), kctx[:, vcol], preferred_element_type=F32))
            attn_scr[rows, hq * HEAD_DIM:(hq + 1) * HEAD_DIM] = o / den

    y_attn = jnp.dot(attn_scr[...].astype(BF16), wa_ref[...], preferred_element_type=F32)

    cu = cu_ref[...]
    ri = lax.broadcasted_iota(jnp.int32, cu.shape, 0)
    prev_row = jnp.where(has_prev, cup_ref[7:8, :], 0.0)
    next_row = jnp.where(has_next, cun_ref[0:1, :], 0.0)
    cu_m1 = jnp.where(ri == 0, prev_row, pltpu.roll(cu, 1, 0))
    cu_p1 = jnp.where(ri == tq - 1, next_row, pltpu.roll(cu, tq - 1, 0))
    cw = convw_ref[...]
    conv = cu_m1 * cw[0:1, :] + cu * cw[1:2, :] + cu_p1 * cw[2:3, :]
    y_conv = jnp.dot((gb_ref[...] * conv).astype(BF16), wc_ref[...], preferred_element_type=F32)

    merged = jax.nn.sigmoid(ga_ref[...]) * y_attn + jax.nn.sigmoid(gv_ref[...]) * y_conv
    y = jnp.dot(merged.astype(BF16), wm_ref[...], preferred_element_type=F32)
    g1 = mod_ref[:, 2 * d:3 * d]
    o_ref[...] = x_ref[...] + g1 * y


def _mixer(x2, q, kv, kvc, cu, gb, ga, gv, sink, conv_w, wa, wc, wm, mod3, seq, tq, b0, batch):
    d = x2.shape[1]
    t, aw = q.shape
    kv2 = kv.shape[1]
    cw = cu.shape[1]
    c = kvc.shape[1]
    nq = seq // tq
    nb = seq // WINDOW
    sub = tq // WINDOW
    kv3 = kv.reshape(t // WINDOW, WINDOW, kv2)
    cu3 = cu.reshape(t // 8, 8, cw)
    tile = lambda b, n: (b * nq + n, 0)
    const = lambda b, n: (0, 0)
    return pl.pallas_call(
        _mixer_kernel,
        out_shape=jax.ShapeDtypeStruct((t, d), F32),
        grid=(batch, nq),
        in_specs=[
            pl.BlockSpec(memory_space=pltpu.SMEM),
            pl.BlockSpec((tq, d), lambda b, n: ((b0 + b) * nq + n, 0)),
            pl.BlockSpec((tq, aw), tile),
            pl.BlockSpec((tq, kv2), tile),
            pl.BlockSpec((None, WINDOW, kv2), lambda b, n: (b * nb + jnp.maximum(n * sub - 1, 0), 0, 0)),
            pl.BlockSpec((None, WINDOW, kv2), lambda b, n: (b * nb + jnp.minimum((n + 1) * sub, nb - 1), 0, 0)),
            pl.BlockSpec((None, c, kv2), lambda b, n: (b0 + b, 0, 0)),
            pl.BlockSpec((tq, cw), tile),
            pl.BlockSpec((None, 8, cw), lambda b, n: (jnp.maximum((b * seq + n * tq) // 8 - 1, 0), 0, 0)),
            pl.BlockSpec((None, 8, cw), lambda b, n: (jnp.minimum((b * seq + (n + 1) * tq) // 8, t // 8 - 1), 0, 0)),
            pl.BlockSpec((tq, cw), tile),
            pl.BlockSpec((tq, d), tile),
            pl.BlockSpec((tq, d), tile),
            pl.BlockSpec(conv_w.shape, const),
            pl.BlockSpec(wa.shape, const),
            pl.BlockSpec(wc.shape, const),
            pl.BlockSpec(wm.shape, const),
            pl.BlockSpec((None, 1, N_MOD * d), lambda b, n: (b0 + b, 0, 0)),
        ],
        out_specs=pl.BlockSpec((tq, d), tile),
        scratch_shapes=[pltpu.VMEM((tq, aw), F32)],
        compiler_params=_cparams("parallel", "parallel"),
        name="mixer",
    )(sink, x2, q, kv, kv3, kv3, kvc, cu, cu3, cu3, gb, ga, gv, conv_w, wa, wc, wm, mod3)


def _topk_rows(s, k, payload=None):
    n = s.shape[0]
    rows = lax.broadcasted_iota(jnp.int32, s.shape, 0).astype(F32)
    vals, picks = [], []
    for _ in range(k):
        m = jnp.max(s, axis=0, keepdims=True)
        am = jnp.min(jnp.where(s == m, rows, float(n)), axis=0, keepdims=True)
        hit = rows == am
        vals.append(m)
        if payload is None:
            picks.append(am)
        else:
            picks.append(jnp.max(jnp.where(hit, payload, -1.0), axis=0, keepdims=True))
        s = jnp.where(hit, -jnp.inf, s)
    return jnp.concatenate(vals, axis=0), jnp.concatenate(picks, axis=0)


def _route_kernel(x1_ref, mod_ref, n2_ref, wq_ref, keys_ref, h2_ref, idx_ref, gate_ref, h2b_scr):
    d = x1_ref.shape[-1]
    nt = (((1,), (1,)), ((), ()))

    @pl.when(pl.program_id(1) == 0)
    def _():
        mod = mod_ref[...]
        h2 = _rmsnorm(x1_ref[...], n2_ref[...]) * (1.0 + mod[:, 4 * d:5 * d]) + mod[:, 3 * d:4 * d]
        h2_ref[...] = h2
        h2b_scr[...] = h2.astype(BF16)

    qp = jnp.dot(h2b_scr[...], wq_ref[...], preferred_element_type=F32)
    half = qp.shape[1] // 2
    tops = []
    for p in range(2):
        qh = qp[:, p * half:(p + 1) * half].astype(BF16)
        st = lax.dot_general(keys_ref[p], qh, nt, preferred_element_type=F32)
        tops.append(_topk_rows(st, PEER_TOPK))
    (a, ia), (b, ib) = tops
    width = [PEER_TOPK // (i + 1) for i in range(PEER_TOPK)]
    pad = -sum(width) % 8
    tm = a.shape[1]
    cand = jnp.concatenate([a[i:i + 1, :] + b[:width[i], :] for i in range(PEER_TOPK)]
                           + [jnp.full((pad, tm), -jnp.inf, F32)], axis=0)
    cidx = jnp.concatenate([ia[i:i + 1, :] * PEER_N_KEYS + ib[:width[i], :] for i in range(PEER_TOPK)]
                           + [jnp.zeros((pad, tm), F32)], axis=0)
    best, idx = _topk_rows(cand, PEER_TOPK, payload=cidx)
    e = jnp.exp(best - best[0:1, :])
    gate_ref[...] = e / jnp.sum(e, axis=0, keepdims=True)
    idx_ref[...] = idx.astype(jnp.int32)


def _route(x1, mod3, n2, wq, keys, seq, tm, b0):
    t, d = x1.shape
    hw = wq.shape[1] // PEER_HEADS
    per_seq = seq // tm
    return pl.pallas_call(
        _route_kernel,
        out_shape=(jax.ShapeDtypeStruct((t, d), F32),
                   jax.ShapeDtypeStruct((PEER_HEADS, PEER_TOPK, t), jnp.int32),
                   jax.ShapeDtypeStruct((PEER_HEADS, PEER_TOPK, t), F32)),
        grid=(t // tm, PEER_HEADS),
        in_specs=[pl.BlockSpec((tm, d), lambda i, h: (i, 0)),
                  pl.BlockSpec((None, 1, N_MOD * d), lambda i, h: (b0 + i // per_seq, 0, 0)),
                  pl.BlockSpec((1, d), lambda i, h: (0, 0)),
                  pl.BlockSpec((d, hw), lambda i, h: (0, h)),
                  pl.BlockSpec((None, 2, PEER_N_KEYS, hw // 2), lambda i, h: (h, 0, 0, 0))],
        out_specs=(pl.BlockSpec((tm, d), lambda i, h: (i, 0)),
                   pl.BlockSpec((None, PEER_TOPK, tm), lambda i, h: (h, 0, i)),
                   pl.BlockSpec((None, PEER_TOPK, tm), lambda i, h: (h, 0, i))),
        scratch_shapes=[pltpu.VMEM((tm, d), BF16)],
        compiler_params=_cparams("parallel", "arbitrary"),
        name="route",
    )(x1, mod3, n2, wq, keys)


SC_CORES = 2
SC_SUBCORES = 16
SC_LANES = 16
SC_WORKERS = SC_CORES * SC_SUBCORES
SC_ROWS_U = 64
SC_ROWS_V = 64
SC_ROWS_PER_ITER = 8
SC_TOKENS = 8


def _sc_mesh():
    return plsc.VectorSubcoreMesh(core_axis_name="c", subcore_axis_name="s")


def _sc_params():
    return dataclasses.replace(pltpu.CompilerParams(), needs_layout_passes=False)


def _sc_worker_base(tokens_per_worker):
    return (lax.axis_index("s") * SC_CORES + lax.axis_index("c")) * tokens_per_worker


def _sc_chunk_pipeline(tab_hbm, idx_v, bufs, n_chunks, compute):
    rows = bufs[0][0].shape[0]

    def gather(g, b):
        buf, sem = bufs[b]
        return pltpu.make_async_copy(tab_hbm.at[idx_v.at[pl.ds(g * rows, rows)]], buf, sem)

    gather(0, 0).start()

    @pl.loop(0, n_chunks, step=2)
    def _(g):
        gather(g, 0).wait()
        gather(g + 1, 1).start()
        compute(g, bufs[0][0])
        gather(g + 1, 1).wait()

        @pl.when(g + 2 < n_chunks)
        def _():
            gather(g + 2, 0).start()

        compute(g + 1, bufs[1][0])


def _pack_table(tab):
    half = tab.shape[1] // 2
    lo = lax.bitcast_convert_type(tab[:, :half].astype(BF16), jnp.uint16).astype(jnp.int32)
    bits = lax.bitcast_convert_type(tab[:, half:], jnp.int32)
    sign = bits & jnp.int32(-2 ** 31)
    magnitude = bits & jnp.int32(2 ** 31 - 1)
    top = jnp.maximum((magnitude - lo + 0x8000) >> 16, 0)
    return sign | ((top << 16) + lo)


def _unpack_lo(x):
    return lax.bitcast_convert_type(x << 16, F32)


def _unpack_hi(x):
    return lax.bitcast_convert_type(x, F32)


def _peer_u_sc(u_pk, idx_flat, h2, t, k):
    d = h2.shape[1]
    words = u_pk.shape[1]
    tpw = t // SC_WORKERS
    rows = SC_ROWS_U
    cpt = k // rows
    cpb = SC_TOKENS * cpt
    nj = words // SC_LANES

    def body(u_hbm, idx_hbm, h_hbm, o_hbm, idx_v, h_v, buf0, buf1, acc_v, pre_v, sem0, sem1):
        base = _sc_worker_base(tpw)
        lanes = lax.iota(jnp.int32, SC_LANES)

        def compute(g, buf):
            tl = g // cpt
            c = g % cpt

            @plsc.parallel_loop(0, rows // SC_ROWS_PER_ITER)
            def _(it):
                r0 = it * SC_ROWS_PER_ITER
                accs = [[None, None] for _ in range(SC_ROWS_PER_ITER)]
                for j in range(nj):
                    h_lo = h_v[tl, pl.ds(j * SC_LANES, SC_LANES)]
                    h_hi = h_v[tl, pl.ds(words + j * SC_LANES, SC_LANES)]
                    for a in range(SC_ROWS_PER_ITER):
                        x = buf[r0 + a, pl.ds(j * SC_LANES, SC_LANES)]
                        term = _unpack_lo(x) * h_lo + _unpack_hi(x) * h_hi
                        accs[a][j % 2] = term if accs[a][j % 2] is None else accs[a][j % 2] + term
                for a in range(SC_ROWS_PER_ITER):
                    acc_v[r0 + a, :] = accs[a][0] + accs[a][1]

            for q in range(rows // SC_LANES):
                s = plsc.load_gather(acc_v, [lanes + q * SC_LANES, jnp.zeros((SC_LANES,), jnp.int32)])
                for l in range(1, SC_LANES):
                    s = s + plsc.load_gather(acc_v, [lanes + q * SC_LANES,
                                                     jnp.full((SC_LANES,), l, jnp.int32)])
                pre_v[tl, pl.ds(c * rows + q * SC_LANES, SC_LANES)] = s

        @pl.loop(0, tpw // SC_TOKENS)
        def _(blk):
            tok0 = base + blk * SC_TOKENS
            pltpu.sync_copy(idx_hbm.at[pl.ds(tok0 * k, SC_TOKENS * k)], idx_v)
            pltpu.sync_copy(h_hbm.at[pl.ds(tok0, SC_TOKENS)], h_v)
            _sc_chunk_pipeline(u_hbm, idx_v, ((buf0, sem0), (buf1, sem1)), cpb, compute)
            pltpu.sync_copy(pre_v, o_hbm.at[pl.ds(tok0, SC_TOKENS)])

    return pl.kernel(
        body,
        out_type=jax.ShapeDtypeStruct((t, k), F32),
        mesh=_sc_mesh(),
        scratch_types=[pltpu.VMEM((SC_TOKENS * k,), jnp.int32),
                       pltpu.VMEM((SC_TOKENS, d), F32),
                       pltpu.VMEM((rows, words), jnp.int32),
                       pltpu.VMEM((rows, words), jnp.int32),
                       pltpu.VMEM((rows, SC_LANES), F32),
                       pltpu.VMEM((SC_TOKENS, k), F32),
                       pltpu.SemaphoreType.DMA,
                       pltpu.SemaphoreType.DMA],
        compiler_params=_sc_params(),
        name="peer_u_sc",
    )(u_pk, idx_flat, h2)


def _peer_v_sc(v_pk, idx_flat, w_flat, t, k):
    words = v_pk.shape[1]
    d = 2 * words
    tpw = t // SC_WORKERS
    rows = SC_ROWS_V
    cpt = k // rows
    cpb = SC_TOKENS * cpt
    pw = words // 2
    nj = pw // SC_LANES

    def body(v_hbm, idx_hbm, w_hbm, o_hbm, idx_v, w_v, buf0, buf1, out_v, sem0, sem1):
        base = _sc_worker_base(tpw)

        def compute(g, buf):
            tl = g // cpt
            c = g % cpt
            for p in range(2):
                lo_cols = [pl.ds(p * pw + j * SC_LANES, SC_LANES) for j in range(nj)]
                hi_cols = [pl.ds(words + p * pw + j * SC_LANES, SC_LANES) for j in range(nj)]

                def row(r, accs):
                    wb = plsc.load_gather(w_v, [jnp.full((SC_LANES,), g * rows + r, jnp.int32)])
                    new = []
                    for j in range(nj):
                        x = buf[r, lo_cols[j]]
                        new.append(accs[2 * j] + _unpack_lo(x) * wb)
                        new.append(accs[2 * j + 1] + _unpack_hi(x) * wb)
                    return tuple(new)

                init = []
                for j in range(nj):
                    init.append(jnp.where(c == 0, 0.0, out_v[tl, lo_cols[j]]))
                    init.append(jnp.where(c == 0, 0.0, out_v[tl, hi_cols[j]]))
                accs = lax.fori_loop(0, rows, row, tuple(init))
                for j in range(nj):
                    out_v[tl, lo_cols[j]] = accs[2 * j]
                    out_v[tl, hi_cols[j]] = accs[2 * j + 1]

        @pl.loop(0, tpw // SC_TOKENS)
        def _(blk):
            tok0 = base + blk * SC_TOKENS
            pltpu.sync_copy(idx_hbm.at[pl.ds(tok0 * k, SC_TOKENS * k)], idx_v)
            pltpu.sync_copy(w_hbm.at[pl.ds(tok0 * k, SC_TOKENS * k)], w_v)
            _sc_chunk_pipeline(v_hbm, idx_v, ((buf0, sem0), (buf1, sem1)), cpb, compute)
            pltpu.sync_copy(out_v, o_hbm.at[pl.ds(tok0, SC_TOKENS)])

    return pl.kernel(
        body,
        out_type=jax.ShapeDtypeStruct((t, d), F32),
        mesh=_sc_mesh(),
        scratch_types=[pltpu.VMEM((SC_TOKENS * k,), jnp.int32),
                       pltpu.VMEM((SC_TOKENS * k,), F32),
                       pltpu.VMEM((rows, words), jnp.int32),
                       pltpu.VMEM((rows, words), jnp.int32),
                       pltpu.VMEM((SC_TOKENS, d), F32),
                       pltpu.SemaphoreType.DMA,
                       pltpu.SemaphoreType.DMA],
        compiler_params=_sc_params(),
        name="peer_v_sc",
    )(v_pk, idx_flat, w_flat)


def _act_kernel(gate_ref, pre_ref, o_ref):
    o_ref[...] = gate_ref[...] * _gelu_exact(pre_ref[...])


def _act(gate, pre, tile):
    t, k = pre.shape
    spec = pl.BlockSpec((tile, k), lambda i: (i, 0))
    return pl.pallas_call(
        _act_kernel,
        out_shape=jax.ShapeDtypeStruct((t, k), F32),
        grid=(t // tile,),
        in_specs=[spec, spec],
        out_specs=spec,
        compiler_params=_cparams("parallel"),
        name="expert_act",
    )(gate, pre)


def _final_kernel(x1_ref, y_ref, mod_ref, fg_ref, out_so_far_ref, o_ref):
    del out_so_far_ref
    d = x1_ref.shape[-1]
    x2 = x1_ref[...] + mod_ref[:, 5 * d:6 * d] * y_ref[...]
    o_ref[...] = _rmsnorm(x2, fg_ref[...])


def _final(x1, y, mod3, fg, out_so_far, seq, tile, b0):
    t, d = y.shape
    per_seq = seq // tile
    row = pl.BlockSpec((tile, d), lambda i: (i, 0))
    return pl.pallas_call(
        _final_kernel,
        out_shape=jax.ShapeDtypeStruct(out_so_far.shape, F32),
        grid=(t // tile,),
        in_specs=[row, row,
                  pl.BlockSpec((None, 1, N_MOD * d), lambda i: (b0 + i // per_seq, 0, 0)),
                  pl.BlockSpec((1, d), lambda i: (0, 0)),
                  pl.BlockSpec(memory_space=pl.ANY)],
        out_specs=pl.BlockSpec((tile, d), lambda i: (b0 * per_seq + i, 0)),
        input_output_aliases={4: 0},
        compiler_params=_cparams("parallel"),
        name="final",
    )(x1, y, mod3, fg, out_so_far)


def _peer_sc(idx, gate, h2, u_pk, v_pk, tile, t):
    k = idx.shape[1]
    assert t % (SC_WORKERS * SC_TOKENS) == 0 and k % (2 * SC_ROWS_U) == 0 and k % (2 * SC_ROWS_V) == 0
    idx_flat = idx.reshape(idx.shape[0] * k)
    pre = _peer_u_sc(u_pk, idx_flat, h2, t, k)
    w = _act(gate, pre, tile)
    return _peer_v_sc(v_pk, idx_flat, w.reshape(t * k), t, k)


TC_PEER_TOKENS = 8


def _peer_tc_kernel(idx_ref, idxn_ref, gate_ref, h2_ref, x1_ref, mod_ref, fg_ref, u_hbm, v_hbm,
                    out_so_far_ref, o_ref, ubuf, vbuf, sem):
    del out_so_far_ref
    i = pl.program_id(0)
    tt, k = gate_ref.shape
    d = x1_ref.shape[-1]
    words = d // 2
    slot = i % 2

    def row_copy(tab, buf, which, sl, src_row, dst_row):
        return pltpu.make_async_copy(tab.at[pl.ds(src_row, 1)], buf.at[sl, pl.ds(dst_row, 1)],
                                     sem.at[which, sl])

    def issue_token(iref, sl, t):
        for j in range(k):
            row = iref[t, j]
            row_copy(u_hbm, ubuf, 0, sl, row, t * k + j).start(priority=j % 2)
            row_copy(v_hbm, vbuf, 1, sl, row, t * k + j).start(priority=(j + 1) % 2)

    @pl.when(i == 0)
    def _():
        def tok(t, carry):
            issue_token(idx_ref, 0, t)
            return carry
        lax.fori_loop(0, tt, tok, 0)

    @pl.when(i + 1 < pl.num_programs(0))
    def _():
        for t in range(tt):
            issue_token(idxn_ref, 1 - slot, t)

    pltpu.make_async_copy(u_hbm.at[pl.ds(0, tt * k)], ubuf.at[slot], sem.at[0, slot]).wait()
    pltpu.make_async_copy(v_hbm.at[pl.ds(0, tt * k)], vbuf.at[slot], sem.at[1, slot]).wait()

    def unpack(x):
        return _unpack_lo(x), _unpack_hi(x)

    cols = []
    for t in range(tt):
        lo, hi = unpack(ubuf[slot, t * k:(t + 1) * k, :])
        cols.append(jnp.sum(lo * h2_ref[t:t + 1, :words] + hi * h2_ref[t:t + 1, words:],
                            axis=1, keepdims=True))
    pre = jnp.concatenate(cols, axis=1)
    w = gate_ref[...].T * _gelu_exact(pre)
    outs = []
    for t in range(tt):
        lo, hi = unpack(vbuf[slot, t * k:(t + 1) * k, :])
        wt = w[:, t:t + 1]
        outs.append(jnp.concatenate([jnp.sum(wt * lo, axis=0, keepdims=True),
                                     jnp.sum(wt * hi, axis=0, keepdims=True)], axis=1))
    y = jnp.concatenate(outs, axis=0)
    x2 = x1_ref[...] + mod_ref[:, 5 * d:6 * d] * y
    o_ref[...] = _rmsnorm(x2, fg_ref[...])


def _peer_tc(idx, gate, h2, x1, mod3, fg, u_pk, v_pk, out_so_far, seq, b0, t0):
    tg, k = idx.shape
    d = x1.shape[1]
    tt = TC_PEER_TOKENS
    n = (tg - t0) // tt
    first = t0 // tt
    per_seq = seq // tt
    row = lambda i: (first + i, 0)
    return pl.pallas_call(
        _peer_tc_kernel,
        out_shape=jax.ShapeDtypeStruct(out_so_far.shape, F32),
        grid=(n,),
        in_specs=[pl.BlockSpec((tt, k), row, memory_space=pltpu.SMEM),
                  pl.BlockSpec((tt, k), lambda i: (first + jnp.minimum(i + 1, n - 1), 0),
                               memory_space=pltpu.SMEM),
                  pl.BlockSpec((tt, k), row),
                  pl.BlockSpec((tt, d), row),
                  pl.BlockSpec((tt, d), row),
                  pl.BlockSpec((None, 1, N_MOD * d), lambda i: (b0 + (first + i) // per_seq, 0, 0)),
                  pl.BlockSpec((1, d), lambda i: (0, 0)),
                  pl.BlockSpec(memory_space=pl.ANY),
                  pl.BlockSpec(memory_space=pl.ANY),
                  pl.BlockSpec(memory_space=pl.ANY)],
        out_specs=pl.BlockSpec((tt, d), lambda i: (b0 * per_seq + first + i, 0)),
        scratch_shapes=[pltpu.VMEM((2, tt * k, d // 2), jnp.int32),
                        pltpu.VMEM((2, tt * k, d // 2), jnp.int32),
                        pltpu.SemaphoreType.DMA((2, 2))],
        input_output_aliases={9: 0},
        compiler_params=_cparams("arbitrary"),
        name="peer_tc",
    )(idx, idx, gate, h2, x1, mod3, fg, u_pk, v_pk, out_so_far)


def _rope_tables(length):
    rows = length // GRID_W
    row = jnp.repeat(jnp.arange(rows, dtype=F32), GRID_W)
    col = jnp.tile(jnp.arange(GRID_W, dtype=F32), rows)
    inv_freq = ROPE_BASE ** (-jnp.arange(ROPE_PAIRS, dtype=F32) / ROPE_PAIRS)
    ang_r = row[:, None] * inv_freq
    ang_c = col[:, None] * inv_freq
    cos = jnp.concatenate([jnp.cos(ang_r)] * 2 + [jnp.cos(ang_c)] * 2, axis=-1)
    sin = jnp.concatenate([-jnp.sin(ang_r), jnp.sin(ang_r), -jnp.sin(ang_c), jnp.sin(ang_c)], axis=-1)
    reps = LANES // HEAD_DIM
    return jnp.tile(cos, (1, reps)), jnp.tile(sin, (1, reps))


def _layer(x, c, ctx, c_ctx, w_mod, b_mod, n1, n2, w_in, sink, conv_w, w_attn_out, w_conv_out,
           w_mix_out, pw_q, p_keys, p_u, p_v, final_g, tm, tq, tr, tt, groups):
    batch, seq, d = x.shape
    t = batch * seq
    aw = N_Q_HEADS * HEAD_DIM
    kw = N_KV_HEADS * HEAD_DIM

    rows = -(-(batch + 1) // 8) * 8
    cond = jnp.zeros((rows, d), F32).at[:batch].set(c).at[batch].set(c_ctx)
    mod3 = _adaln(cond, w_mod, b_mod).reshape(rows, 1, N_MOD * d)

    w_in_b = w_in.astype(BF16)
    kvc = _ctx_kv(ctx, mod3, batch, n1.reshape(1, d), w_in_b[:, aw:aw + 2 * kw])

    cos, sin = _rope_tables(seq)
    x2 = x.reshape(t, d)
    wa, wc, wm = w_attn_out.astype(BF16), w_conv_out.astype(BF16), w_mix_out.astype(BF16)
    wq, keys = pw_q.astype(BF16), p_keys.astype(BF16)
    u_pk, v_pk = _pack_table(p_u), _pack_table(p_v)
    nsel = PEER_HEADS * PEER_TOPK

    nb = batch // groups
    tg = nb * seq
    fg = final_g.reshape(1, d)
    out = pl.empty((t, d), F32)
    for g in range(groups):
        b0 = g * nb
        t_sc = _sparsecore_share(tg, g)
        q, kv, gb, cu, ga, gv = _proj(x2, mod3, n1.reshape(1, d), w_in_b, cos, sin, seq, tm, b0, nb)
        x1 = _mixer(x2, q, kv, kvc, cu, gb, ga, gv, sink.reshape(1, N_Q_HEADS), conv_w,
                    wa, wc, wm, mod3, seq, tq, b0, nb)
        h2, idx_t, gate_t = _route(x1, mod3, n2.reshape(1, d), wq, keys, seq, tr, b0)
        idx, gate = idx_t.reshape(nsel, tg).T, gate_t.reshape(nsel, tg).T
        y = _peer_sc(idx, gate, h2, u_pk, v_pk, tt, t_sc)
        if t_sc < tg:
            out = _peer_tc(idx, gate, h2, x1, mod3, fg, u_pk, v_pk, out, seq, b0, t_sc)
        out = _final(x1, y, mod3, fg, out, seq, tt, b0)
    return out.reshape(batch, seq, d)


MAX_TOKEN_GROUPS = 8
SC_SHARE_NUMS, SC_SHARE_DEN = (13, 14), 16


def _sparsecore_share(tokens, group):
    unit = SC_WORKERS * SC_TOKENS
    t_sc = tokens * SC_SHARE_NUMS[group % len(SC_SHARE_NUMS)] // SC_SHARE_DEN // unit * unit
    if t_sc == 0 or (tokens - t_sc) % TC_PEER_TOKENS:
        return tokens
    return t_sc


def _token_groups(batch, seq):
    for groups in range(min(MAX_TOKEN_GROUPS, batch), 0, -1):
        if batch % groups == 0 and (batch // groups * seq) % (SC_WORKERS * SC_TOKENS) == 0:
            return groups
    raise ValueError("token count must be a multiple of the SparseCore work split")


def kernel(x, c, ctx, c_ctx, w_mod, b_mod, norm1_g, norm2_g, w_in, attn_sink, conv_w, w_attn_out,
           w_conv_out, w_mix_out, peer_w_q, peer_sub_keys, peer_u, peer_v, final_g):
    assert w_mod.shape[0] == 1, "only the single-layer configuration is implemented"
    seq = x.shape[1]
    return _layer(x, c, ctx, c_ctx, w_mod[0], b_mod[0], norm1_g[0], norm2_g[0], w_in[0],
                  attn_sink[0], conv_w[0], w_attn_out[0], w_conv_out[0], w_mix_out[0],
                  peer_w_q[0], peer_sub_keys[0], peer_u[0], peer_v[0], final_g,
                  tm=min(512, seq), tq=min(256, seq), tr=min(256, seq), tt=min(256, seq),
                  groups=_token_groups(x.shape[0], seq))
```

```python
import dataclasses

import jax
import jax.numpy as jnp
from jax import lax
from jax.experimental import pallas as pl
from jax.experimental.pallas import tpu as pltpu
from jax.experimental.pallas import tpu_sc as plsc

HEAD_DIM = 64
N_Q_HEADS = 8
N_KV_HEADS = 2
Q_PER_KV = N_Q_HEADS // N_KV_HEADS
WINDOW = 128
GRID_W = 64
ROPE_BASE = 10000.0
ROPE_PAIRS = HEAD_DIM // 4
PEER_HEADS = 8
PEER_N_KEYS = 128
PEER_TOPK = 16
N_MOD = 6
EPS = 1e-6
NEG_INF = -1e30

LANES = 128
VMEM_LIMIT = 56 * 1024 * 1024

F32 = jnp.float32
BF16 = jnp.bfloat16


def _cparams(*sem):
    return pltpu.CompilerParams(dimension_semantics=sem, vmem_limit_bytes=VMEM_LIMIT)


def _rmsnorm(x, g):
    return x * lax.rsqrt(jnp.mean(x * x, axis=-1, keepdims=True) + EPS) * g


def _gelu_exact(x):
    return 0.5 * x * (1.0 + lax.erf(x * (2.0 ** -0.5)))


def _adaln_kernel(cond_ref, w_ref, b_ref, o_ref):
    act = jax.nn.silu(cond_ref[...])
    o_ref[...] = jnp.dot(act, w_ref[...], precision=lax.Precision.HIGHEST,
                         preferred_element_type=F32) + b_ref[...]


def _adaln(cond, w_mod, b_mod):
    rows, d = cond.shape
    n = w_mod.shape[1]
    tn = d
    return pl.pallas_call(
        _adaln_kernel,
        out_shape=jax.ShapeDtypeStruct((rows, n), F32),
        grid=(n // tn,),
        in_specs=[pl.BlockSpec((rows, d), lambda j: (0, 0)),
                  pl.BlockSpec((d, tn), lambda j: (0, j)),
                  pl.BlockSpec((1, tn), lambda j: (0, j))],
        out_specs=pl.BlockSpec((rows, tn), lambda j: (0, j)),
        compiler_params=_cparams("parallel"),
        name="adaln",
    )(cond, w_mod, b_mod.reshape(1, n))


def _ctx_kv_kernel(xc_ref, mod_ref, n1_ref, w_ref, o_ref):
    d = xc_ref.shape[-1]
    mod = mod_ref[...]
    hc = _rmsnorm(xc_ref[...], n1_ref[...]) * (1.0 + mod[:, d:2 * d]) + mod[:, 0:d]
    o_ref[...] = jnp.dot(hc.astype(BF16), w_ref[...], preferred_element_type=F32).astype(BF16)


def _ctx_kv(ctx, mod3, ctx_row, n1, w_kv):
    b, c, d = ctx.shape
    kvw = w_kv.shape[1]
    return pl.pallas_call(
        _ctx_kv_kernel,
        out_shape=jax.ShapeDtypeStruct((b, c, kvw), BF16),
        grid=(b,),
        in_specs=[pl.BlockSpec((None, c, d), lambda i: (i, 0, 0)),
                  pl.BlockSpec((None, 1, N_MOD * d), lambda i: (ctx_row, 0, 0)),
                  pl.BlockSpec((1, d), lambda i: (0, 0)),
                  pl.BlockSpec((d, kvw), lambda i: (0, 0))],
        out_specs=pl.BlockSpec((None, c, kvw), lambda i: (i, 0, 0)),
        compiler_params=_cparams("parallel"),
        name="ctx_kv",
    )(ctx, mod3, n1, w_kv)


def _proj_kernel(x_ref, mod_ref, n1_ref, w_ref, cos_ref, sin_ref,
                 q_ref, kv_ref, gb_ref, cu_ref, ga_ref, gv_ref):
    d = x_ref.shape[-1]
    aw = q_ref.shape[-1]
    kw = kv_ref.shape[-1] // 2
    cw = gb_ref.shape[-1]
    mod = mod_ref[...]
    h = (_rmsnorm(x_ref[...], n1_ref[...]) * (1.0 + mod[:, d:2 * d]) + mod[:, 0:d]).astype(BF16)
    cos = cos_ref[...]
    sin = sin_ref[...]
    lane = lax.broadcasted_iota(jnp.int32, cos.shape, 1)
    first_half = (lane % (2 * ROPE_PAIRS)) < ROPE_PAIRS

    def rope(z):
        partner = jnp.where(first_half, pltpu.roll(z, LANES - ROPE_PAIRS, 1),
                            pltpu.roll(z, ROPE_PAIRS, 1))
        return z * cos + partner * sin

    def proj(lo, width):
        return jnp.dot(h, w_ref[:, lo:lo + width], preferred_element_type=F32)

    off = 0
    zq = proj(off, aw)
    for g in range(aw // LANES):
        q_ref[:, g * LANES:(g + 1) * LANES] = rope(zq[:, g * LANES:(g + 1) * LANES]).astype(BF16)
    off += aw
    zkv = proj(off, 2 * kw)
    for g in range(kw // LANES):
        kv_ref[:, g * LANES:(g + 1) * LANES] = rope(zkv[:, g * LANES:(g + 1) * LANES]).astype(BF16)
    kv_ref[:, kw:] = zkv[:, kw:].astype(BF16)
    off += 2 * kw
    gb_ref[...] = proj(off, cw)
    off += cw
    zc = proj(off, cw)
    off += cw
    cu_ref[...] = zc * proj(off, cw)
    off += cw
    ga_ref[...] = proj(off, d)
    off += d
    gv_ref[...] = proj(off, d)


def _proj(x2, mod3, n1, w_in, cos, sin, seq, tm, b0, nb):
    d = x2.shape[1]
    t = nb * seq
    aw = N_Q_HEADS * HEAD_DIM
    kw = N_KV_HEADS * HEAD_DIM
    cw = d // 2
    per_seq = seq // tm
    row = lambda i: (i, 0)
    return pl.pallas_call(
        _proj_kernel,
        out_shape=(jax.ShapeDtypeStruct((t, aw), BF16),
                   jax.ShapeDtypeStruct((t, 2 * kw), BF16),
                   jax.ShapeDtypeStruct((t, cw), F32),
                   jax.ShapeDtypeStruct((t, cw), F32),
                   jax.ShapeDtypeStruct((t, d), F32),
                   jax.ShapeDtypeStruct((t, d), F32)),
        grid=(t // tm,),
        in_specs=[pl.BlockSpec((tm, d), lambda i: (b0 * per_seq + i, 0)),
                  pl.BlockSpec((None, 1, N_MOD * d), lambda i: (b0 + i // per_seq, 0, 0)),
                  pl.BlockSpec((1, d), lambda i: (0, 0)),
                  pl.BlockSpec(w_in.shape, lambda i: (0, 0)),
                  pl.BlockSpec((tm, LANES), lambda i: (i % per_seq, 0)),
                  pl.BlockSpec((tm, LANES), lambda i: (i % per_seq, 0))],
        out_specs=(pl.BlockSpec((tm, aw), row), pl.BlockSpec((tm, 2 * kw), row),
                   pl.BlockSpec((tm, cw), row), pl.BlockSpec((tm, cw), row),
                   pl.BlockSpec((tm, d), row), pl.BlockSpec((tm, d), row)),
        compiler_params=_cparams("parallel"),
        name="proj",
    )(x2, mod3, n1, w_in, cos, sin)


def _mixer_kernel(sink_ref, x_ref, q_ref, kv_ref, kvp_ref, kvn_ref, kvc_ref,
                  cu_ref, cup_ref, cun_ref, gb_ref, ga_ref, gv_ref, convw_ref,
                  wa_ref, wc_ref, wm_ref, mod_ref, o_ref, attn_scr):
    n = pl.program_id(1)
    has_prev = n > 0
    has_next = n < pl.num_programs(1) - 1
    tq, d = x_ref.shape
    kw = N_KV_HEADS * HEAD_DIM
    scale = HEAD_DIM ** -0.5
    nt = (((1,), (1,)), ((), ()))

    kext = jnp.concatenate([kvp_ref[...], kv_ref[...], kvn_ref[...]], axis=0)
    kctx = kvc_ref[...]
    qi = lax.broadcasted_iota(jnp.int32, (WINDOW, WINDOW), 0)
    ki = lax.broadcasted_iota(jnp.int32, (WINDOW, WINDOW), 1)
    nblk = tq // WINDOW
    for j in range(nblk):
        prev_ok = ki >= qi
        next_ok = ki <= qi
        if j == 0:
            prev_ok = jnp.logical_and(prev_ok, has_prev)
        if j == nblk - 1:
            next_ok = jnp.logical_and(next_ok, has_next)
        mask = jnp.concatenate([prev_ok, jnp.ones_like(prev_ok), next_ok], axis=1)
        rows = slice(j * WINDOW, (j + 1) * WINDOW)
        krows = slice(j * WINDOW, (j + 3) * WINDOW)
        for hq in range(N_Q_HEADS):
            kvh = hq // Q_PER_KV
            kcol = slice(kvh * HEAD_DIM, (kvh + 1) * HEAD_DIM)
            vcol = slice(kw + kvh * HEAD_DIM, kw + (kvh + 1) * HEAD_DIM)
            qh = q_ref[rows, hq * HEAD_DIM:(hq + 1) * HEAD_DIM]
            s_loc = lax.dot_general(qh, kext[krows, kcol], nt, preferred_element_type=F32) * scale
            s_loc = jnp.where(mask, s_loc, NEG_INF)
            s_ctx = lax.dot_general(qh, kctx[:, kcol], nt, preferred_element_type=F32) * scale
            sink = sink_ref[0, hq]
            m = jnp.maximum(jnp.max(s_loc, axis=-1, keepdims=True),
                            jnp.max(s_ctx, axis=-1, keepdims=True))
            m = jnp.maximum(m, sink)
            p_loc = jnp.exp(s_loc - m)
            p_ctx = jnp.exp(s_ctx - m)
            den = (jnp.sum(p_loc, axis=-1, keepdims=True) + jnp.sum(p_ctx, axis=-1, keepdims=True)
                   + jnp.exp(sink - m))
            o = (jnp.dot(p_loc.astype(BF16), kext[krows, vcol], preferred_element_type=F32)
                 + jnp.dot(p_ctx.astype(BF16), kctx[:, vcol], preferred_element_type=F32))
            attn_scr[rows, hq * HEAD_DIM:(hq + 1) * HEAD_DIM] = o / den

    y_attn = jnp.dot(attn_scr[...].astype(BF16), wa_ref[...], preferred_element_type=F32)

    cu = cu_ref[...]
    ri = lax.broadcasted_iota(jnp.int32, cu.shape, 0)
    prev_row = jnp.where(has_prev, cup_ref[7:8, :], 0.0)
    next_row = jnp.where(has_next, cun_ref[0:1, :], 0.0)
    cu_m1 = jnp.where(ri == 0, prev_row, pltpu.roll(cu, 1, 0))
    cu_p1 = jnp.where(ri == tq - 1, next_row, pltpu.roll(cu, tq - 1, 0))
    cw = convw_ref[...]
    conv = cu_m1 * cw[0:1, :] + cu * cw[1:2, :] + cu_p1 * cw[2:3, :]
    y_conv = jnp.dot((gb_ref[...] * conv).astype(BF16), wc_ref[...], preferred_element_type=F32)

    merged = jax.nn.sigmoid(ga_ref[...]) * y_attn + jax.nn.sigmoid(gv_ref[...]) * y_conv
    y = jnp.dot(merged.astype(BF16), wm_ref[...], preferred_element_type=F32)
    g1 = mod_ref[:, 2 * d:3 * d]
    o_ref[...] = x_ref[...] + g1 * y


def _mixer(x2, q, kv, kvc, cu, gb, ga, gv, sink, conv_w, wa, wc, wm, mod3, seq, tq, b0, batch):
    d = x2.shape[1]
    t, aw = q.shape
    kv2 = kv.shape[1]
    cw = cu.shape[1]
    c = kvc.shape[1]
    nq = seq // tq
    nb = seq // WINDOW
    sub = tq // WINDOW
    kv3 = kv.reshape(t // WINDOW, WINDOW, kv2)
    cu3 = cu.reshape(t // 8, 8, cw)
    tile = lambda b, n: (b * nq + n, 0)
    const = lambda b, n: (0, 0)
    return pl.pallas_call(
        _mixer_kernel,
        out_shape=jax.ShapeDtypeStruct((t, d), F32),
        grid=(batch, nq),
        in_specs=[
            pl.BlockSpec(memory_space=pltpu.SMEM),
            pl.BlockSpec((tq, d), lambda b, n: ((b0 + b) * nq + n, 0)),
            pl.BlockSpec((tq, aw), tile),
            pl.BlockSpec((tq, kv2), tile),
            pl.BlockSpec((None, WINDOW, kv2), lambda b, n: (b * nb + jnp.maximum(n * sub - 1, 0), 0, 0)),
            pl.BlockSpec((None, WINDOW, kv2), lambda b, n: (b * nb + jnp.minimum((n + 1) * sub, nb - 1), 0, 0)),
            pl.BlockSpec((None, c, kv2), lambda b, n: (b0 + b, 0, 0)),
            pl.BlockSpec((tq, cw), tile),
            pl.BlockSpec((None, 8, cw), lambda b, n: (jnp.maximum((b * seq + n * tq) // 8 - 1, 0), 0, 0)),
            pl.BlockSpec((None, 8, cw), lambda b, n: (jnp.minimum((b * seq + (n + 1) * tq) // 8, t // 8 - 1), 0, 0)),
            pl.BlockSpec((tq, cw), tile),
            pl.BlockSpec((tq, d), tile),
            pl.BlockSpec((tq, d), tile),
            pl.BlockSpec(conv_w.shape, const),
            pl.BlockSpec(wa.shape, const),
            pl.BlockSpec(wc.shape, const),
            pl.BlockSpec(wm.shape, const),
            pl.BlockSpec((None, 1, N_MOD * d), lambda b, n: (b0 + b, 0, 0)),
        ],
        out_specs=pl.BlockSpec((tq, d), tile),
        scratch_shapes=[pltpu.VMEM((tq, aw), F32)],
        compiler_params=_cparams("parallel", "parallel"),
        name="mixer",
    )(sink, x2, q, kv, kv3, kv3, kvc, cu, cu3, cu3, gb, ga, gv, conv_w, wa, wc, wm, mod3)


def _topk_rows(s, k, payload=None):
    n = s.shape[0]
    rows = lax.broadcasted_iota(jnp.int32, s.shape, 0).astype(F32)
    vals, picks = [], []
    for _ in range(k):
        m = jnp.max(s, axis=0, keepdims=True)
        am = jnp.min(jnp.where(s == m, rows, float(n)), axis=0, keepdims=True)
        hit = rows == am
        vals.append(m)
        if payload is None:
            picks.append(am)
        else:
            picks.append(jnp.max(jnp.where(hit, payload, -1.0), axis=0, keepdims=True))
        s = jnp.where(hit, -jnp.inf, s)
    return jnp.concatenate(vals, axis=0), jnp.concatenate(picks, axis=0)


def _route_kernel(x1_ref, mod_ref, n2_ref, wq_ref, keys_ref, h2_ref, idx_ref, gate_ref, h2b_scr):
    d = x1_ref.shape[-1]
    nt = (((1,), (1,)), ((), ()))

    @pl.when(pl.program_id(1) == 0)
    def _():
        mod = mod_ref[...]
        h2 = _rmsnorm(x1_ref[...], n2_ref[...]) * (1.0 + mod[:, 4 * d:5 * d]) + mod[:, 3 * d:4 * d]
        h2_ref[...] = h2
        h2b_scr[...] = h2.astype(BF16)

    qp = jnp.dot(h2b_scr[...], wq_ref[...], preferred_element_type=F32)
    half = qp.shape[1] // 2
    tops = []
    for p in range(2):
        qh = qp[:, p * half:(p + 1) * half].astype(BF16)
        st = lax.dot_general(keys_ref[p], qh, nt, preferred_element_type=F32)
        tops.append(_topk_rows(st, PEER_TOPK))
    (a, ia), (b, ib) = tops
    width = [PEER_TOPK // (i + 1) for i in range(PEER_TOPK)]
    pad = -sum(width) % 8
    tm = a.shape[1]
    cand = jnp.concatenate([a[i:i + 1, :] + b[:width[i], :] for i in range(PEER_TOPK)]
                           + [jnp.full((pad, tm), -jnp.inf, F32)], axis=0)
    cidx = jnp.concatenate([ia[i:i + 1, :] * PEER_N_KEYS + ib[:width[i], :] for i in range(PEER_TOPK)]
                           + [jnp.zeros((pad, tm), F32)], axis=0)
    best, idx = _topk_rows(cand, PEER_TOPK, payload=cidx)
    e = jnp.exp(best - best[0:1, :])
    gate_ref[...] = e / jnp.sum(e, axis=0, keepdims=True)
    idx_ref[...] = idx.astype(jnp.int32)


def _route(x1, mod3, n2, wq, keys, seq, tm, b0):
    t, d = x1.shape
    hw = wq.shape[1] // PEER_HEADS
    per_seq = seq // tm
    return pl.pallas_call(
        _route_kernel,
        out_shape=(jax.ShapeDtypeStruct((t, d), F32),
                   jax.ShapeDtypeStruct((PEER_HEADS, PEER_TOPK, t), jnp.int32),
                   jax.ShapeDtypeStruct((PEER_HEADS, PEER_TOPK, t), F32)),
        grid=(t // tm, PEER_HEADS),
        in_specs=[pl.BlockSpec((tm, d), lambda i, h: (i, 0)),
                  pl.BlockSpec((None, 1, N_MOD * d), lambda i, h: (b0 + i // per_seq, 0, 0)),
                  pl.BlockSpec((1, d), lambda i, h: (0, 0)),
                  pl.BlockSpec((d, hw), lambda i, h: (0, h)),
                  pl.BlockSpec((None, 2, PEER_N_KEYS, hw // 2), lambda i, h: (h, 0, 0, 0))],
        out_specs=(pl.BlockSpec((tm, d), lambda i, h: (i, 0)),
                   pl.BlockSpec((None, PEER_TOPK, tm), lambda i, h: (h, 0, i)),
                   pl.BlockSpec((None, PEER_TOPK, tm), lambda i, h: (h, 0, i))),
        scratch_shapes=[pltpu.VMEM((tm, d), BF16)],
        compiler_params=_cparams("parallel", "arbitrary"),
        name="route",
    )(x1, mod3, n2, wq, keys)


SC_CORES = 2
SC_SUBCORES = 16
SC_LANES = 16
SC_WORKERS = SC_CORES * SC_SUBCORES
SC_ROWS_U = 64
SC_ROWS_V = 64
SC_ROWS_PER_ITER = 8
SC_TOKENS = 8


def _sc_mesh():
    return plsc.VectorSubcoreMesh(core_axis_name="c", subcore_axis_name="s")


def _sc_params():
    return dataclasses.replace(pltpu.CompilerParams(), needs_layout_passes=False)


def _sc_worker_base(tokens_per_worker):
    return (lax.axis_index("s") * SC_CORES + lax.axis_index("c")) * tokens_per_worker


def _sc_chunk_pipeline(tab_hbm, idx_v, bufs, n_chunks, compute):
    rows = bufs[0][0].shape[0]

    def gather(g, b):
        buf, sem = bufs[b]
        return pltpu.make_async_copy(tab_hbm.at[idx_v.at[pl.ds(g * rows, rows)]], buf, sem)

    gather(0, 0).start()

    @pl.loop(0, n_chunks, step=2)
    def _(g):
        gather(g, 0).wait()
        gather(g + 1, 1).start()
        compute(g, bufs[0][0])
        gather(g + 1, 1).wait()

        @pl.when(g + 2 < n_chunks)
        def _():
            gather(g + 2, 0).start()

        compute(g + 1, bufs[1][0])


def _pack_table(tab):
    half = tab.shape[1] // 2
    lo = lax.bitcast_convert_type(tab[:, :half].astype(BF16), jnp.uint16).astype(jnp.int32)
    bits = lax.bitcast_convert_type(tab[:, half:], jnp.int32)
    sign = bits & jnp.int32(-2 ** 31)
    magnitude = bits & jnp.int32(2 ** 31 - 1)
    top = jnp.maximum((magnitude - lo + 0x8000) >> 16, 0)
    return sign | ((top << 16) + lo)


def _unpack_lo(x):
    return lax.bitcast_convert_type(x << 16, F32)


def _unpack_hi(x):
    return lax.bitcast_convert_type(x, F32)


def _peer_u_sc(u_pk, idx_flat, h2, t, k):
    d = h2.shape[1]
    words = u_pk.shape[1]
    tpw = t // SC_WORKERS
    rows = SC_ROWS_U
    cpt = k // rows
    cpb = SC_TOKENS * cpt
    nj = words // SC_LANES

    def body(u_hbm, idx_hbm, h_hbm, o_hbm, idx_v, h_v, buf0, buf1, acc_v, pre_v, sem0, sem1):
        base = _sc_worker_base(tpw)
        lanes = lax.iota(jnp.int32, SC_LANES)

        def compute(g, buf):
            tl = g // cpt
            c = g % cpt

            @plsc.parallel_loop(0, rows // SC_ROWS_PER_ITER)
            def _(it):
                r0 = it * SC_ROWS_PER_ITER
                accs = [[None, None] for _ in range(SC_ROWS_PER_ITER)]
                for j in range(nj):
                    h_lo = h_v[tl, pl.ds(j * SC_LANES, SC_LANES)]
                    h_hi = h_v[tl, pl.ds(words + j * SC_LANES, SC_LANES)]
                    for a in range(SC_ROWS_PER_ITER):
                        x = buf[r0 + a, pl.ds(j * SC_LANES, SC_LANES)]
                        term = _unpack_lo(x) * h_lo + _unpack_hi(x) * h_hi
                        accs[a][j % 2] = term if accs[a][j % 2] is None else accs[a][j % 2] + term
                for a in range(SC_ROWS_PER_ITER):
                    acc_v[r0 + a, :] = accs[a][0] + accs[a][1]

            for q in range(rows // SC_LANES):
                s = plsc.load_gather(acc_v, [lanes + q * SC_LANES, jnp.zeros((SC_LANES,), jnp.int32)])
                for l in range(1, SC_LANES):
                    s = s + plsc.load_gather(acc_v, [lanes + q * SC_LANES,
                                                     jnp.full((SC_LANES,), l, jnp.int32)])
                pre_v[tl, pl.ds(c * rows + q * SC_LANES, SC_LANES)] = s

        @pl.loop(0, tpw // SC_TOKENS)
        def _(blk):
            tok0 = base + blk * SC_TOKENS
            pltpu.sync_copy(idx_hbm.at[pl.ds(tok0 * k, SC_TOKENS * k)], idx_v)
            pltpu.sync_copy(h_hbm.at[pl.ds(tok0, SC_TOKENS)], h_v)
            _sc_chunk_pipeline(u_hbm, idx_v, ((buf0, sem0), (buf1, sem1)), cpb, compute)
            pltpu.sync_copy(pre_v, o_hbm.at[pl.ds(tok0, SC_TOKENS)])

    return pl.kernel(
        body,
        out_type=jax.ShapeDtypeStruct((t, k), F32),
        mesh=_sc_mesh(),
        scratch_types=[pltpu.VMEM((SC_TOKENS * k,), jnp.int32),
                       pltpu.VMEM((SC_TOKENS, d), F32),
                       pltpu.VMEM((rows, words), jnp.int32),
                       pltpu.VMEM((rows, words), jnp.int32),
                       pltpu.VMEM((rows, SC_LANES), F32),
                       pltpu.VMEM((SC_TOKENS, k), F32),
                       pltpu.SemaphoreType.DMA,
                       pltpu.SemaphoreType.DMA],
        compiler_params=_sc_params(),
        name="peer_u_sc",
    )(u_pk, idx_flat, h2)


def _peer_v_sc(v_pk, idx_flat, w_flat, t, k):
    words = v_pk.shape[1]
    d = 2 * words
    tpw = t // SC_WORKERS
    rows = SC_ROWS_V
    cpt = k // rows
    cpb = SC_TOKENS * cpt
    pw = words // 2
    nj = pw // SC_LANES

    def body(v_hbm, idx_hbm, w_hbm, o_hbm, idx_v, w_v, buf0, buf1, out_v, sem0, sem1):
        base = _sc_worker_base(tpw)

        def compute(g, buf):
            tl = g // cpt
            c = g % cpt
            for p in range(2):
                lo_cols = [pl.ds(p * pw + j * SC_LANES, SC_LANES) for j in range(nj)]
                hi_cols = [pl.ds(words + p * pw + j * SC_LANES, SC_LANES) for j in range(nj)]

                def row(r, accs):
                    wb = plsc.load_gather(w_v, [jnp.full((SC_LANES,), g * rows + r, jnp.int32)])
                    new = []
                    for j in range(nj):
                        x = buf[r, lo_cols[j]]
                        new.append(accs[2 * j] + _unpack_lo(x) * wb)
                        new.append(accs[2 * j + 1] + _unpack_hi(x) * wb)
                    return tuple(new)

                init = []
                for j in range(nj):
                    init.append(jnp.where(c == 0, 0.0, out_v[tl, lo_cols[j]]))
                    init.append(jnp.where(c == 0, 0.0, out_v[tl, hi_cols[j]]))
                accs = lax.fori_loop(0, rows, row, tuple(init))
                for j in range(nj):
                    out_v[tl, lo_cols[j]] = accs[2 * j]
                    out_v[tl, hi_cols[j]] = accs[2 * j + 1]

        @pl.loop(0, tpw // SC_TOKENS)
        def _(blk):
            tok0 = base + blk * SC_TOKENS
            pltpu.sync_copy(idx_hbm.at[pl.ds(tok0 * k, SC_TOKENS * k)], idx_v)
            pltpu.sync_copy(w_hbm.at[pl.ds(tok0 * k, SC_TOKENS * k)], w_v)
            _sc_chunk_pipeline(v_hbm, idx_v, ((buf0, sem0), (buf1, sem1)), cpb, compute)
            pltpu.sync_copy(out_v, o_hbm.at[pl.ds(tok0, SC_TOKENS)])

    return pl.kernel(
        body,
        out_type=jax.ShapeDtypeStruct((t, d), F32),
        mesh=_sc_mesh(),
        scratch_types=[pltpu.VMEM((SC_TOKENS * k,), jnp.int32),
                       pltpu.VMEM((SC_TOKENS * k,), F32),
                       pltpu.VMEM((rows, words), jnp.int32),
                       pltpu.VMEM((rows, words), jnp.int32),
                       pltpu.VMEM((SC_TOKENS, d), F32),
                       pltpu.SemaphoreType.DMA,
                       pltpu.SemaphoreType.DMA],
        compiler_params=_sc_params(),
        name="peer_v_sc",
    )(v_pk, idx_flat, w_flat)


def _act_kernel(gate_ref, pre_ref, o_ref):
    o_ref[...] = gate_ref[...] * _gelu_exact(pre_ref[...])


def _act(gate, pre, tile):
    t, k = pre.shape
    spec = pl.BlockSpec((tile, k), lambda i: (i, 0))
    return pl.pallas_call(
        _act_kernel,
        out_shape=jax.ShapeDtypeStruct((t, k), F32),
        grid=(t // tile,),
        in_specs=[spec, spec],
        out_specs=spec,
        compiler_params=_cparams("parallel"),
        name="expert_act",
    )(gate, pre)


def _final_kernel(x1_ref, y_ref, mod_ref, fg_ref, out_so_far_ref, o_ref):
    del out_so_far_ref
    d = x1_ref.shape[-1]
    x2 = x1_ref[...] + mod_ref[:, 5 * d:6 * d] * y_ref[...]
    o_ref[...] = _rmsnorm(x2, fg_ref[...])


def _final(x1, y, mod3, fg, out_so_far, seq, tile, b0):
    t, d = y.shape
    per_seq = seq // tile
    row = pl.BlockSpec((tile, d), lambda i: (i, 0))
    return pl.pallas_call(
        _final_kernel,
        out_shape=jax.ShapeDtypeStruct(out_so_far.shape, F32),
        grid=(t // tile,),
        in_specs=[row, row,
                  pl.BlockSpec((None, 1, N_MOD * d), lambda i: (b0 + i // per_seq, 0, 0)),
                  pl.BlockSpec((1, d), lambda i: (0, 0)),
                  pl.BlockSpec(memory_space=pl.ANY)],
        out_specs=pl.BlockSpec((tile, d), lambda i: (b0 * per_seq + i, 0)),
        input_output_aliases={4: 0},
        compiler_params=_cparams("parallel"),
        name="final",
    )(x1, y, mod3, fg, out_so_far)


def _peer_sc(idx, gate, h2, u_pk, v_pk, tile, t):
    k = idx.shape[1]
    assert t % (SC_WORKERS * SC_TOKENS) == 0 and k % (2 * SC_ROWS_U) == 0 and k % (2 * SC_ROWS_V) == 0
    idx_flat = idx.reshape(idx.shape[0] * k)
    pre = _peer_u_sc(u_pk, idx_flat, h2, t, k)
    w = _act(gate, pre, tile)
    return _peer_v_sc(v_pk, idx_flat, w.reshape(t * k), t, k)


TC_PEER_TOKENS = 8


def _peer_tc_kernel(idx_ref, idxn_ref, gate_ref, h2_ref, x1_ref, mod_ref, fg_ref, u_hbm, v_hbm,
                    out_so_far_ref, o_ref, ubuf, vbuf, sem):
    del out_so_far_ref
    i = pl.program_id(0)
    tt, k = gate_ref.shape
    d = x1_ref.shape[-1]
    words = d // 2
    slot = i % 2

    def row_copy(tab, buf, which, sl, src_row, dst_row):
        return pltpu.make_async_copy(tab.at[pl.ds(src_row, 1)], buf.at[sl, pl.ds(dst_row, 1)],
                                     sem.at[which, sl])

    def issue_token(iref, sl, t):
        for j in range(k):
            row = iref[t, j]
            row_copy(u_hbm, ubuf, 0, sl, row, t * k + j).start(priority=j % 2)
            row_copy(v_hbm, vbuf, 1, sl, row, t * k + j).start(priority=(j + 1) % 2)

    @pl.when(i == 0)
    def _():
        def tok(t, carry):
            issue_token(idx_ref, 0, t)
            return carry
        lax.fori_loop(0, tt, tok, 0)

    @pl.when(i + 1 < pl.num_programs(0))
    def _():
        for t in range(tt):
            issue_token(idxn_ref, 1 - slot, t)

    pltpu.make_async_copy(u_hbm.at[pl.ds(0, tt * k)], ubuf.at[slot], sem.at[0, slot]).wait()
    pltpu.make_async_copy(v_hbm.at[pl.ds(0, tt * k)], vbuf.at[slot], sem.at[1, slot]).wait()

    def unpack(x):
        return _unpack_lo(x), _unpack_hi(x)

    cols = []
    for t in range(tt):
        lo, hi = unpack(ubuf[slot, t * k:(t + 1) * k, :])
        cols.append(jnp.sum(lo * h2_ref[t:t + 1, :words] + hi * h2_ref[t:t + 1, words:],
                            axis=1, keepdims=True))
    pre = jnp.concatenate(cols, axis=1)
    w = gate_ref[...].T * _gelu_exact(pre)
    outs = []
    for t in range(tt):
        lo, hi = unpack(vbuf[slot, t * k:(t + 1) * k, :])
        wt = w[:, t:t + 1]
        outs.append(jnp.concatenate([jnp.sum(wt * lo, axis=0, keepdims=True),
                                     jnp.sum(wt * hi, axis=0, keepdims=True)], axis=1))
    y = jnp.concatenate(outs, axis=0)
    x2 = x1_ref[...] + mod_ref[:, 5 * d:6 * d] * y
    o_ref[...] = _rmsnorm(x2, fg_ref[...])


def _peer_tc(idx, gate, h2, x1, mod3, fg, u_pk, v_pk, out_so_far, seq, b0, t0):
    tg, k = idx.shape
    d = x1.shape[1]
    tt = TC_PEER_TOKENS
    n = (tg - t0) // tt
    first = t0 // tt
    per_seq = seq // tt
    row = lambda i: (first + i, 0)
    return pl.pallas_call(
        _peer_tc_kernel,
        out_shape=jax.ShapeDtypeStruct(out_so_far.shape, F32),
        grid=(n,),
        in_specs=[pl.BlockSpec((tt, k), row, memory_space=pltpu.SMEM),
                  pl.BlockSpec((tt, k), lambda i: (first + jnp.minimum(i + 1, n - 1), 0),
                               memory_space=pltpu.SMEM),
                  pl.BlockSpec((tt, k), row),
                  pl.BlockSpec((tt, d), row),
                  pl.BlockSpec((tt, d), row),
                  pl.BlockSpec((None, 1, N_MOD * d), lambda i: (b0 + (first + i) // per_seq, 0, 0)),
                  pl.BlockSpec((1, d), lambda i: (0, 0)),
                  pl.BlockSpec(memory_space=pl.ANY),
                  pl.BlockSpec(memory_space=pl.ANY),
                  pl.BlockSpec(memory_space=pl.ANY)],
        out_specs=pl.BlockSpec((tt, d), lambda i: (b0 * per_seq + first + i, 0)),
        scratch_shapes=[pltpu.VMEM((2, tt * k, d // 2), jnp.int32),
                        pltpu.VMEM((2, tt * k, d // 2), jnp.int32),
                        pltpu.SemaphoreType.DMA((2, 2))],
        input_output_aliases={9: 0},
        compiler_params=_cparams("arbitrary"),
        name="peer_tc",
    )(idx, idx, gate, h2, x1, mod3, fg, u_pk, v_pk, out_so_far)


def _rope_tables(length):
    rows = length // GRID_W
    row = jnp.repeat(jnp.arange(rows, dtype=F32), GRID_W)
    col = jnp.tile(jnp.arange(GRID_W, dtype=F32), rows)
    inv_freq = ROPE_BASE ** (-jnp.arange(ROPE_PAIRS, dtype=F32) / ROPE_PAIRS)
    ang_r = row[:, None] * inv_freq
    ang_c = col[:, None] * inv_freq
    cos = jnp.concatenate([jnp.cos(ang_r)] * 2 + [jnp.cos(ang_c)] * 2, axis=-1)
    sin = jnp.concatenate([-jnp.sin(ang_r), jnp.sin(ang_r), -jnp.sin(ang_c), jnp.sin(ang_c)], axis=-1)
    reps = LANES // HEAD_DIM
    return jnp.tile(cos, (1, reps)), jnp.tile(sin, (1, reps))


def _layer(x, c, ctx, c_ctx, w_mod, b_mod, n1, n2, w_in, sink, conv_w, w_attn_out, w_conv_out,
           w_mix_out, pw_q, p_keys, p_u, p_v, final_g, tm, tq, tr, tt, groups):
    batch, seq, d = x.shape
    t = batch * seq
    aw = N_Q_HEADS * HEAD_DIM
    kw = N_KV_HEADS * HEAD_DIM

    rows = -(-(batch + 1) // 8) * 8
    cond = jnp.zeros((rows, d), F32).at[:batch].set(c).at[batch].set(c_ctx)
    mod3 = _adaln(cond, w_mod, b_mod).reshape(rows, 1, N_MOD * d)

    w_in_b = w_in.astype(BF16)
    kvc = _ctx_kv(ctx, mod3, batch, n1.reshape(1, d), w_in_b[:, aw:aw + 2 * kw])

    cos, sin = _rope_tables(seq)
    x2 = x.reshape(t, d)
    wa, wc, wm = w_attn_out.astype(BF16), w_conv_out.astype(BF16), w_mix_out.astype(BF16)
    wq, keys = pw_q.astype(BF16), p_keys.astype(BF16)
    u_pk, v_pk = _pack_table(p_u), _pack_table(p_v)
    nsel = PEER_HEADS * PEER_TOPK

    fg = final_g.reshape(1, d)
    out = pl.empty((t, d), F32)
    b0 = 0
    for g, nb in enumerate(groups):
        tg = nb * seq
        t_sc = _sparsecore_share(tg, g)
        q, kv, gb, cu, ga, gv = _proj(x2, mod3, n1.reshape(1, d), w_in_b, cos, sin, seq, tm, b0, nb)
        x1 = _mixer(x2, q, kv, kvc, cu, gb, ga, gv, sink.reshape(1, N_Q_HEADS), conv_w,
                    wa, wc, wm, mod3, seq, tq, b0, nb)
        h2, idx_t, gate_t = _route(x1, mod3, n2.reshape(1, d), wq, keys, seq, tr, b0)
        idx, gate = idx_t.reshape(nsel, tg).T, gate_t.reshape(nsel, tg).T
        y = _peer_sc(idx, gate, h2, u_pk, v_pk, tt, t_sc)
        if t_sc < tg:
            out = _peer_tc(idx, gate, h2, x1, mod3, fg, u_pk, v_pk, out, seq, b0, t_sc)
        out = _final(x1, y, mod3, fg, out, seq, tt, b0)
        b0 += nb
    return out.reshape(batch, seq, d)


MAX_TOKEN_GROUPS = 8
SC_SHARE_NUMS, SC_SHARE_DEN = (13,), 16


def _sparsecore_share(tokens, group):
    unit = SC_WORKERS * SC_TOKENS
    t_sc = tokens * SC_SHARE_NUMS[group % len(SC_SHARE_NUMS)] // SC_SHARE_DEN // unit * unit
    if t_sc == 0 or (tokens - t_sc) % TC_PEER_TOKENS:
        return tokens
    return t_sc


def _token_groups(batch, seq):
    unit = SC_WORKERS * SC_TOKENS
    for groups in range(min(MAX_TOKEN_GROUPS, batch), 0, -1):
        nb = batch // groups
        if batch % groups == 0 and (nb * seq) % unit == 0:
            if nb % 2 == 0 and (nb // 2 * seq) % unit == 0:
                return [nb // 2] + [nb] * (groups - 1) + [nb // 2]
            return [nb] * groups
    raise ValueError("token count must be a multiple of the SparseCore work split")


def kernel(x, c, ctx, c_ctx, w_mod, b_mod, norm1_g, norm2_g, w_in, attn_sink, conv_w, w_attn_out,
           w_conv_out, w_mix_out, peer_w_q, peer_sub_keys, peer_u, peer_v, final_g):
    assert w_mod.shape[0] == 1, "only the single-layer configuration is implemented"
    seq = x.shape[1]
    return _layer(x, c, ctx, c_ctx, w_mod[0], b_mod[0], norm1_g[0], norm2_g[0], w_in[0],
                  attn_sink[0], conv_w[0], w_attn_out[0], w_conv_out[0], w_mix_out[0],
                  peer_w_q[0], peer_sub_keys[0], peer_u[0], peer_v[0], final_g,
                  tm=min(512, seq), tq=min(256, seq), tr=min(256, seq), tt=min(256, seq),
                  groups=_token_groups(x.shape[0], seq))
```

```python
import dataclasses

import jax
import jax.numpy as jnp
from jax import lax
from jax.experimental import pallas as pl
from jax.experimental.pallas import tpu as pltpu
from jax.experimental.pallas import tpu_sc as plsc

HEAD_DIM = 64
N_Q_HEADS = 8
N_KV_HEADS = 2
Q_PER_KV = N_Q_HEADS // N_KV_HEADS
WINDOW = 128
GRID_W = 64
ROPE_BASE = 10000.0
ROPE_PAIRS = HEAD_DIM // 4
PEER_HEADS = 8
PEER_N_KEYS = 128
PEER_TOPK = 16
N_MOD = 6
EPS = 1e-6
NEG_INF = -1e30

LANES = 128
VMEM_LIMIT = 56 * 1024 * 1024

F32 = jnp.float32
BF16 = jnp.bfloat16


def _cparams(*sem):
    return pltpu.CompilerParams(dimension_semantics=sem, vmem_limit_bytes=VMEM_LIMIT)


def _rmsnorm(x, g):
    return x * lax.rsqrt(jnp.mean(x * x, axis=-1, keepdims=True) + EPS) * g


def _gelu_exact(x):
    return 0.5 * x * (1.0 + lax.erf(x * (2.0 ** -0.5)))


def _adaln_kernel(cond_ref, w_ref, b_ref, o_ref):
    act = jax.nn.silu(cond_ref[...])
    o_ref[...] = jnp.dot(act, w_ref[...], precision=lax.Precision.HIGHEST,
                         preferred_element_type=F32) + b_ref[...]


def _adaln(cond, w_mod, b_mod):
    rows, d = cond.shape
    n = w_mod.shape[1]
    tn = d
    return pl.pallas_call(
        _adaln_kernel,
        out_shape=jax.ShapeDtypeStruct((rows, n), F32),
        grid=(n // tn,),
        in_specs=[pl.BlockSpec((rows, d), lambda j: (0, 0)),
                  pl.BlockSpec((d, tn), lambda j: (0, j)),
                  pl.BlockSpec((1, tn), lambda j: (0, j))],
        out_specs=pl.BlockSpec((rows, tn), lambda j: (0, j)),
        compiler_params=_cparams("parallel"),
        name="adaln",
    )(cond, w_mod, b_mod.reshape(1, n))


def _ctx_kv_kernel(xc_ref, mod_ref, n1_ref, w_ref, o_ref):
    d = xc_ref.shape[-1]
    mod = mod_ref[...]
    hc = _rmsnorm(xc_ref[...], n1_ref[...]) * (1.0 + mod[:, d:2 * d]) + mod[:, 0:d]
    o_ref[...] = jnp.dot(hc.astype(BF16), w_ref[...], preferred_element_type=F32).astype(BF16)


def _ctx_kv(ctx, mod3, ctx_row, n1, w_kv):
    b, c, d = ctx.shape
    kvw = w_kv.shape[1]
    return pl.pallas_call(
        _ctx_kv_kernel,
        out_shape=jax.ShapeDtypeStruct((b, c, kvw), BF16),
        grid=(b,),
        in_specs=[pl.BlockSpec((None, c, d), lambda i: (i, 0, 0)),
                  pl.BlockSpec((None, 1, N_MOD * d), lambda i: (ctx_row, 0, 0)),
                  pl.BlockSpec((1, d), lambda i: (0, 0)),
                  pl.BlockSpec((d, kvw), lambda i: (0, 0))],
        out_specs=pl.BlockSpec((None, c, kvw), lambda i: (i, 0, 0)),
        compiler_params=_cparams("parallel"),
        name="ctx_kv",
    )(ctx, mod3, n1, w_kv)


def _proj_kernel(x_ref, mod_ref, n1_ref, w_ref, cos_ref, sin_ref,
                 q_ref, kv_ref, gb_ref, cu_ref, ga_ref, gv_ref):
    d = x_ref.shape[-1]
    aw = q_ref.shape[-1]
    kw = kv_ref.shape[-1] // 2
    cw = gb_ref.shape[-1]
    mod = mod_ref[...]
    h = (_rmsnorm(x_ref[...], n1_ref[...]) * (1.0 + mod[:, d:2 * d]) + mod[:, 0:d]).astype(BF16)
    cos = cos_ref[...]
    sin = sin_ref[...]
    lane = lax.broadcasted_iota(jnp.int32, cos.shape, 1)
    first_half = (lane % (2 * ROPE_PAIRS)) < ROPE_PAIRS

    def rope(z):
        partner = jnp.where(first_half, pltpu.roll(z, LANES - ROPE_PAIRS, 1),
                            pltpu.roll(z, ROPE_PAIRS, 1))
        return z * cos + partner * sin

    def proj(lo, width):
        return jnp.dot(h, w_ref[:, lo:lo + width], preferred_element_type=F32)

    off = 0
    zq = proj(off, aw)
    for g in range(aw // LANES):
        q_ref[:, g * LANES:(g + 1) * LANES] = rope(zq[:, g * LANES:(g + 1) * LANES]).astype(BF16)
    off += aw
    zkv = proj(off, 2 * kw)
    for g in range(kw // LANES):
        kv_ref[:, g * LANES:(g + 1) * LANES] = rope(zkv[:, g * LANES:(g + 1) * LANES]).astype(BF16)
    kv_ref[:, kw:] = zkv[:, kw:].astype(BF16)
    off += 2 * kw
    gb_ref[...] = proj(off, cw)
    off += cw
    zc = proj(off, cw)
    off += cw
    cu_ref[...] = zc * proj(off, cw)
    off += cw
    ga_ref[...] = proj(off, d)
    off += d
    gv_ref[...] = proj(off, d)


def _proj(x2, mod3, n1, w_in, cos, sin, seq, tm, b0, nb):
    d = x2.shape[1]
    t = nb * seq
    aw = N_Q_HEADS * HEAD_DIM
    kw = N_KV_HEADS * HEAD_DIM
    cw = d // 2
    per_seq = seq // tm
    row = lambda i: (i, 0)
    return pl.pallas_call(
        _proj_kernel,
        out_shape=(jax.ShapeDtypeStruct((t, aw), BF16),
                   jax.ShapeDtypeStruct((t, 2 * kw), BF16),
                   jax.ShapeDtypeStruct((t, cw), F32),
                   jax.ShapeDtypeStruct((t, cw), F32),
                   jax.ShapeDtypeStruct((t, d), F32),
                   jax.ShapeDtypeStruct((t, d), F32)),
        grid=(t // tm,),
        in_specs=[pl.BlockSpec((tm, d), lambda i: (b0 * per_seq + i, 0)),
                  pl.BlockSpec((None, 1, N_MOD * d), lambda i: (b0 + i // per_seq, 0, 0)),
                  pl.BlockSpec((1, d), lambda i: (0, 0)),
                  pl.BlockSpec(w_in.shape, lambda i: (0, 0)),
                  pl.BlockSpec((tm, LANES), lambda i: (i % per_seq, 0)),
                  pl.BlockSpec((tm, LANES), lambda i: (i % per_seq, 0))],
        out_specs=(pl.BlockSpec((tm, aw), row), pl.BlockSpec((tm, 2 * kw), row),
                   pl.BlockSpec((tm, cw), row), pl.BlockSpec((tm, cw), row),
                   pl.BlockSpec((tm, d), row), pl.BlockSpec((tm, d), row)),
        compiler_params=_cparams("parallel"),
        name="proj",
    )(x2, mod3, n1, w_in, cos, sin)


def _mixer_kernel(sink_ref, x_ref, q_ref, kv_ref, kvp_ref, kvn_ref, kvc_ref,
                  cu_ref, cup_ref, cun_ref, gb_ref, ga_ref, gv_ref, convw_ref,
                  wa_ref, wc_ref, wm_ref, mod_ref, o_ref, attn_scr):
    n = pl.program_id(1)
    has_prev = n > 0
    has_next = n < pl.num_programs(1) - 1
    tq, d = x_ref.shape
    kw = N_KV_HEADS * HEAD_DIM
    scale = HEAD_DIM ** -0.5
    nt = (((1,), (1,)), ((), ()))

    kext = jnp.concatenate([kvp_ref[...], kv_ref[...], kvn_ref[...]], axis=0)
    kctx = kvc_ref[...]
    qi = lax.broadcasted_iota(jnp.int32, (WINDOW, WINDOW), 0)
    ki = lax.broadcasted_iota(jnp.int32, (WINDOW, WINDOW), 1)
    nblk = tq // WINDOW
    for j in range(nblk):
        prev_ok = ki >= qi
        next_ok = ki <= qi
        if j == 0:
            prev_ok = jnp.logical_and(prev_ok, has_prev)
        if j == nblk - 1:
            next_ok = jnp.logical_and(next_ok, has_next)
        mask = jnp.concatenate([prev_ok, jnp.ones_like(prev_ok), next_ok], axis=1)
        rows = slice(j * WINDOW, (j + 1) * WINDOW)
        krows = slice(j * WINDOW, (j + 3) * WINDOW)
        for hq in range(N_Q_HEADS):
            kvh = hq // Q_PER_KV
            kcol = slice(kvh * HEAD_DIM, (kvh + 1) * HEAD_DIM)
            vcol = slice(kw + kvh * HEAD_DIM, kw + (kvh + 1) * HEAD_DIM)
            qh = q_ref[rows, hq * HEAD_DIM:(hq + 1) * HEAD_DIM]
            s_loc = lax.dot_general(qh, kext[krows, kcol], nt, preferred_element_type=F32) * scale
            s_loc = jnp.where(mask, s_loc, NEG_INF)
            s_ctx = lax.dot_general(qh, kctx[:, kcol], nt, preferred_element_type=F32) * scale
            sink = sink_ref[0, hq]
            m = jnp.maximum(jnp.max(s_loc, axis=-1, keepdims=True),
                            jnp.max(s_ctx, axis=-1, keepdims=True))
            m = jnp.maximum(m, sink)
            p_loc = jnp.exp(s_loc - m)
            p_ctx = jnp.exp(s_ctx - m)
            den = (jnp.sum(p_loc, axis=-1, keepdims=True) + jnp.sum(p_ctx, axis=-1, keepdims=True)
                   + jnp.exp(sink - m))
            o = (jnp.dot(p_loc.astype(BF16), kext[krows, vcol], preferred_element_type=F32)
                 + jnp.dot(p_ctx.astype(BF16), kctx[:, vcol], preferred_element_type=F32))
            attn_scr[rows, hq * HEAD_DIM:(hq + 1) * HEAD_DIM] = o / den

    y_attn = jnp.dot(attn_scr[...].astype(BF16), wa_ref[...], preferred_element_type=F32)

    cu = cu_ref[...]
    ri = lax.broadcasted_iota(jnp.int32, cu.shape, 0)
    prev_row = jnp.where(has_prev, cup_ref[7:8, :], 0.0)
    next_row = jnp.where(has_next, cun_ref[0:1, :], 0.0)
    cu_m1 = jnp.where(ri == 0, prev_row, pltpu.roll(cu, 1, 0))
    cu_p1 = jnp.where(ri == tq - 1, next_row, pltpu.roll(cu, tq - 1, 0))
    cw = convw_ref[...]
    conv = cu_m1 * cw[0:1, :] + cu * cw[1:2, :] + cu_p1 * cw[2:3, :]
    y_conv = jnp.dot((gb_ref[...] * conv).astype(BF16), wc_ref[...], preferred_element_type=F32)

    merged = jax.nn.sigmoid(ga_ref[...]) * y_attn + jax.nn.sigmoid(gv_ref[...]) * y_conv
    y = jnp.dot(merged.astype(BF16), wm_ref[...], preferred_element_type=F32)
    g1 = mod_ref[:, 2 * d:3 * d]
    o_ref[...] = x_ref[...] + g1 * y


def _mixer(x2, q, kv, kvc, cu, gb, ga, gv, sink, conv_w, wa, wc, wm, mod3, seq, tq, b0, batch):
    d = x2.shape[1]
    t, aw = q.shape
    kv2 = kv.shape[1]
    cw = cu.shape[1]
    c = kvc.shape[1]
    nq = seq // tq
    nb = seq // WINDOW
    sub = tq // WINDOW
    kv3 = kv.reshape(t // WINDOW, WINDOW, kv2)
    cu3 = cu.reshape(t // 8, 8, cw)
    tile = lambda b, n: (b * nq + n, 0)
    const = lambda b, n: (0, 0)
    return pl.pallas_call(
        _mixer_kernel,
        out_shape=jax.ShapeDtypeStruct((t, d), F32),
        grid=(batch, nq),
        in_specs=[
            pl.BlockSpec(memory_space=pltpu.SMEM),
            pl.BlockSpec((tq, d), lambda b, n: ((b0 + b) * nq + n, 0)),
            pl.BlockSpec((tq, aw), tile),
            pl.BlockSpec((tq, kv2), tile),
            pl.BlockSpec((None, WINDOW, kv2), lambda b, n: (b * nb + jnp.maximum(n * sub - 1, 0), 0, 0)),
            pl.BlockSpec((None, WINDOW, kv2), lambda b, n: (b * nb + jnp.minimum((n + 1) * sub, nb - 1), 0, 0)),
            pl.BlockSpec((None, c, kv2), lambda b, n: (b0 + b, 0, 0)),
            pl.BlockSpec((tq, cw), tile),
            pl.BlockSpec((None, 8, cw), lambda b, n: (jnp.maximum((b * seq + n * tq) // 8 - 1, 0), 0, 0)),
            pl.BlockSpec((None, 8, cw), lambda b, n: (jnp.minimum((b * seq + (n + 1) * tq) // 8, t // 8 - 1), 0, 0)),
            pl.BlockSpec((tq, cw), tile),
            pl.BlockSpec((tq, d), tile),
            pl.BlockSpec((tq, d), tile),
            pl.BlockSpec(conv_w.shape, const),
            pl.BlockSpec(wa.shape, const),
            pl.BlockSpec(wc.shape, const),
            pl.BlockSpec(wm.shape, const),
            pl.BlockSpec((None, 1, N_MOD * d), lambda b, n: (b0 + b, 0, 0)),
        ],
        out_specs=pl.BlockSpec((tq, d), tile),
        scratch_shapes=[pltpu.VMEM((tq, aw), F32)],
        compiler_params=_cparams("parallel", "parallel"),
        name="mixer",
    )(sink, x2, q, kv, kv3, kv3, kvc, cu, cu3, cu3, gb, ga, gv, conv_w, wa, wc, wm, mod3)


def _topk_rows(s, k, payload=None):
    n = s.shape[0]
    rows = lax.broadcasted_iota(jnp.int32, s.shape, 0).astype(F32)
    vals, picks = [], []
    for _ in range(k):
        m = jnp.max(s, axis=0, keepdims=True)
        am = jnp.min(jnp.where(s == m, rows, float(n)), axis=0, keepdims=True)
        hit = rows == am
        vals.append(m)
        if payload is None:
            picks.append(am)
        else:
            picks.append(jnp.max(jnp.where(hit, payload, -1.0), axis=0, keepdims=True))
        s = jnp.where(hit, -jnp.inf, s)
    return jnp.concatenate(vals, axis=0), jnp.concatenate(picks, axis=0)


def _route_kernel(x1_ref, mod_ref, n2_ref, wq_ref, keys_ref, h2_ref, idx_ref, gate_ref, h2b_scr):
    d = x1_ref.shape[-1]
    nt = (((1,), (1,)), ((), ()))

    @pl.when(pl.program_id(1) == 0)
    def _():
        mod = mod_ref[...]
        h2 = _rmsnorm(x1_ref[...], n2_ref[...]) * (1.0 + mod[:, 4 * d:5 * d]) + mod[:, 3 * d:4 * d]
        h2_ref[...] = h2
        h2b_scr[...] = h2.astype(BF16)

    qp = jnp.dot(h2b_scr[...], wq_ref[...], preferred_element_type=F32)
    half = qp.shape[1] // 2
    tops = []
    for p in range(2):
        qh = qp[:, p * half:(p + 1) * half].astype(BF16)
        st = lax.dot_general(keys_ref[p], qh, nt, preferred_element_type=F32)
        tops.append(_topk_rows(st, PEER_TOPK))
    (a, ia), (b, ib) = tops
    width = [PEER_TOPK // (i + 1) for i in range(PEER_TOPK)]
    pad = -sum(width) % 8
    tm = a.shape[1]
    cand = jnp.concatenate([a[i:i + 1, :] + b[:width[i], :] for i in range(PEER_TOPK)]
                           + [jnp.full((pad, tm), -jnp.inf, F32)], axis=0)
    cidx = jnp.concatenate([ia[i:i + 1, :] * PEER_N_KEYS + ib[:width[i], :] for i in range(PEER_TOPK)]
                           + [jnp.zeros((pad, tm), F32)], axis=0)
    best, idx = _topk_rows(cand, PEER_TOPK, payload=cidx)
    e = jnp.exp(best - best[0:1, :])
    gate_ref[...] = e / jnp.sum(e, axis=0, keepdims=True)
    idx_ref[...] = idx.astype(jnp.int32)


def _route(x1, mod3, n2, wq, keys, seq, tm, b0):
    t, d = x1.shape
    hw = wq.shape[1] // PEER_HEADS
    per_seq = seq // tm
    return pl.pallas_call(
        _route_kernel,
        out_shape=(jax.ShapeDtypeStruct((t, d), F32),
                   jax.ShapeDtypeStruct((PEER_HEADS, PEER_TOPK, t), jnp.int32),
                   jax.ShapeDtypeStruct((PEER_HEADS, PEER_TOPK, t), F32)),
        grid=(t // tm, PEER_HEADS),
        in_specs=[pl.BlockSpec((tm, d), lambda i, h: (i, 0)),
                  pl.BlockSpec((None, 1, N_MOD * d), lambda i, h: (b0 + i // per_seq, 0, 0)),
                  pl.BlockSpec((1, d), lambda i, h: (0, 0)),
                  pl.BlockSpec((d, hw), lambda i, h: (0, h)),
                  pl.BlockSpec((None, 2, PEER_N_KEYS, hw // 2), lambda i, h: (h, 0, 0, 0))],
        out_specs=(pl.BlockSpec((tm, d), lambda i, h: (i, 0)),
                   pl.BlockSpec((None, PEER_TOPK, tm), lambda i, h: (h, 0, i)),
                   pl.BlockSpec((None, PEER_TOPK, tm), lambda i, h: (h, 0, i))),
        scratch_shapes=[pltpu.VMEM((tm, d), BF16)],
        compiler_params=_cparams("parallel", "arbitrary"),
        name="route",
    )(x1, mod3, n2, wq, keys)


SC_CORES = 2
SC_SUBCORES = 16
SC_LANES = 16
SC_WORKERS = SC_CORES * SC_SUBCORES
SC_ROWS_U = 64
SC_ROWS_V = 64
SC_ROWS_PER_ITER = 8
SC_TOKENS = 8


def _sc_mesh():
    return plsc.VectorSubcoreMesh(core_axis_name="c", subcore_axis_name="s")


def _sc_params():
    return dataclasses.replace(pltpu.CompilerParams(), needs_layout_passes=False)


def _sc_worker_base(tokens_per_worker):
    return (lax.axis_index("s") * SC_CORES + lax.axis_index("c")) * tokens_per_worker


def _sc_chunk_pipeline(tab_hbm, idx_v, bufs, n_chunks, compute):
    rows = bufs[0][0].shape[0]

    def gather(g, b):
        buf, sem = bufs[b]
        return pltpu.make_async_copy(tab_hbm.at[idx_v.at[pl.ds(g * rows, rows)]], buf, sem)

    gather(0, 0).start()

    @pl.loop(0, n_chunks, step=2)
    def _(g):
        gather(g, 0).wait()
        gather(g + 1, 1).start()
        compute(g, bufs[0][0])
        gather(g + 1, 1).wait()

        @pl.when(g + 2 < n_chunks)
        def _():
            gather(g + 2, 0).start()

        compute(g + 1, bufs[1][0])


def _pack_table(tab):
    half = tab.shape[1] // 2
    lo = lax.bitcast_convert_type(tab[:, :half].astype(BF16), jnp.uint16).astype(jnp.int32)
    bits = lax.bitcast_convert_type(tab[:, half:], jnp.int32)
    sign = bits & jnp.int32(-2 ** 31)
    magnitude = bits & jnp.int32(2 ** 31 - 1)
    top = jnp.maximum((magnitude - lo + 0x8000) >> 16, 0)
    return sign | ((top << 16) + lo)


def _unpack_lo(x):
    return lax.bitcast_convert_type(x << 16, F32)


def _unpack_hi(x):
    return lax.bitcast_convert_type(x, F32)


def _peer_u_sc(u_pk, idx_flat, h2, t, k):
    d = h2.shape[1]
    words = u_pk.shape[1]
    tpw = t // SC_WORKERS
    rows = SC_ROWS_U
    cpt = k // rows
    cpb = SC_TOKENS * cpt
    nj = words // SC_LANES

    def body(u_hbm, idx_hbm, h_hbm, o_hbm, idx_v, h_v, buf0, buf1, acc_v, pre_v, sem0, sem1):
        base = _sc_worker_base(tpw)
        lanes = lax.iota(jnp.int32, SC_LANES)

        def compute(g, buf):
            tl = g // cpt
            c = g % cpt

            @plsc.parallel_loop(0, rows // SC_ROWS_PER_ITER)
            def _(it):
                r0 = it * SC_ROWS_PER_ITER
                accs = [[None, None] for _ in range(SC_ROWS_PER_ITER)]
                for j in range(nj):
                    h_lo = h_v[tl, pl.ds(j * SC_LANES, SC_LANES)]
                    h_hi = h_v[tl, pl.ds(words + j * SC_LANES, SC_LANES)]
                    for a in range(SC_ROWS_PER_ITER):
                        x = buf[r0 + a, pl.ds(j * SC_LANES, SC_LANES)]
                        term = _unpack_lo(x) * h_lo + _unpack_hi(x) * h_hi
                        accs[a][j % 2] = term if accs[a][j % 2] is None else accs[a][j % 2] + term
                for a in range(SC_ROWS_PER_ITER):
                    acc_v[r0 + a, :] = accs[a][0] + accs[a][1]

            for q in range(rows // SC_LANES):
                s = plsc.load_gather(acc_v, [lanes + q * SC_LANES, jnp.zeros((SC_LANES,), jnp.int32)])
                for l in range(1, SC_LANES):
                    s = s + plsc.load_gather(acc_v, [lanes + q * SC_LANES,
                                                     jnp.full((SC_LANES,), l, jnp.int32)])
                pre_v[tl, pl.ds(c * rows + q * SC_LANES, SC_LANES)] = s

        @pl.loop(0, tpw // SC_TOKENS)
        def _(blk):
            tok0 = base + blk * SC_TOKENS
            pltpu.sync_copy(idx_hbm.at[pl.ds(tok0 * k, SC_TOKENS * k)], idx_v)
            pltpu.sync_copy(h_hbm.at[pl.ds(tok0, SC_TOKENS)], h_v)
            _sc_chunk_pipeline(u_hbm, idx_v, ((buf0, sem0), (buf1, sem1)), cpb, compute)
            pltpu.sync_copy(pre_v, o_hbm.at[pl.ds(tok0, SC_TOKENS)])

    return pl.kernel(
        body,
        out_type=jax.ShapeDtypeStruct((t, k), F32),
        mesh=_sc_mesh(),
        scratch_types=[pltpu.VMEM((SC_TOKENS * k,), jnp.int32),
                       pltpu.VMEM((SC_TOKENS, d), F32),
                       pltpu.VMEM((rows, words), jnp.int32),
                       pltpu.VMEM((rows, words), jnp.int32),
                       pltpu.VMEM((rows, SC_LANES), F32),
                       pltpu.VMEM((SC_TOKENS, k), F32),
                       pltpu.SemaphoreType.DMA,
                       pltpu.SemaphoreType.DMA],
        compiler_params=_sc_params(),
        name="peer_u_sc",
    )(u_pk, idx_flat, h2)


def _peer_v_sc(v_pk, idx_flat, w_flat, t, k, after):
    words = v_pk.shape[1]
    d = 2 * words
    tpw = t // SC_WORKERS
    rows = SC_ROWS_V
    cpt = k // rows
    cpb = SC_TOKENS * cpt
    pw = words // 2
    nj = pw // SC_LANES

    def body(v_hbm, idx_hbm, w_hbm, after_hbm, o_hbm, idx_v, w_v, buf0, buf1, out_v, sem0, sem1):
        del after_hbm
        base = _sc_worker_base(tpw)

        def compute(g, buf):
            tl = g // cpt
            c = g % cpt
            for p in range(2):
                lo_cols = [pl.ds(p * pw + j * SC_LANES, SC_LANES) for j in range(nj)]
                hi_cols = [pl.ds(words + p * pw + j * SC_LANES, SC_LANES) for j in range(nj)]

                def row(r, accs):
                    wb = plsc.load_gather(w_v, [jnp.full((SC_LANES,), g * rows + r, jnp.int32)])
                    new = []
                    for j in range(nj):
                        x = buf[r, lo_cols[j]]
                        new.append(accs[2 * j] + _unpack_lo(x) * wb)
                        new.append(accs[2 * j + 1] + _unpack_hi(x) * wb)
                    return tuple(new)

                init = []
                for j in range(nj):
                    init.append(jnp.where(c == 0, 0.0, out_v[tl, lo_cols[j]]))
                    init.append(jnp.where(c == 0, 0.0, out_v[tl, hi_cols[j]]))
                accs = lax.fori_loop(0, rows, row, tuple(init))
                for j in range(nj):
                    out_v[tl, lo_cols[j]] = accs[2 * j]
                    out_v[tl, hi_cols[j]] = accs[2 * j + 1]

        @pl.loop(0, tpw // SC_TOKENS)
        def _(blk):
            tok0 = base + blk * SC_TOKENS
            pltpu.sync_copy(idx_hbm.at[pl.ds(tok0 * k, SC_TOKENS * k)], idx_v)
            pltpu.sync_copy(w_hbm.at[pl.ds(tok0 * k, SC_TOKENS * k)], w_v)
            _sc_chunk_pipeline(v_hbm, idx_v, ((buf0, sem0), (buf1, sem1)), cpb, compute)
            pltpu.sync_copy(out_v, o_hbm.at[pl.ds(tok0, SC_TOKENS)])

    return pl.kernel(
        body,
        out_type=jax.ShapeDtypeStruct((t, d), F32),
        mesh=_sc_mesh(),
        scratch_types=[pltpu.VMEM((SC_TOKENS * k,), jnp.int32),
                       pltpu.VMEM((SC_TOKENS * k,), F32),
                       pltpu.VMEM((rows, words), jnp.int32),
                       pltpu.VMEM((rows, words), jnp.int32),
                       pltpu.VMEM((SC_TOKENS, d), F32),
                       pltpu.SemaphoreType.DMA,
                       pltpu.SemaphoreType.DMA],
        compiler_params=_sc_params(),
        name="peer_v_sc",
    )(v_pk, idx_flat, w_flat, after)


def _act_kernel(gate_ref, pre_ref, o_ref):
    o_ref[...] = gate_ref[...] * _gelu_exact(pre_ref[...])


def _act(gate, pre, tile):
    t, k = pre.shape
    spec = pl.BlockSpec((tile, k), lambda i: (i, 0))
    return pl.pallas_call(
        _act_kernel,
        out_shape=jax.ShapeDtypeStruct((t, k), F32),
        grid=(t // tile,),
        in_specs=[spec, spec],
        out_specs=spec,
        compiler_params=_cparams("parallel"),
        name="expert_act",
    )(gate, pre)


def _final_kernel(x1_ref, y_ref, mod_ref, fg_ref, out_so_far_ref, o_ref):
    del out_so_far_ref
    d = x1_ref.shape[-1]
    x2 = x1_ref[...] + mod_ref[:, 5 * d:6 * d] * y_ref[...]
    o_ref[...] = _rmsnorm(x2, fg_ref[...])


def _final(x1, y, mod3, fg, out_so_far, seq, tile, b0):
    t, d = y.shape
    per_seq = seq // tile
    row = pl.BlockSpec((tile, d), lambda i: (i, 0))
    return pl.pallas_call(
        _final_kernel,
        out_shape=jax.ShapeDtypeStruct(out_so_far.shape, F32),
        grid=(t // tile,),
        in_specs=[row, row,
                  pl.BlockSpec((None, 1, N_MOD * d), lambda i: (b0 + i // per_seq, 0, 0)),
                  pl.BlockSpec((1, d), lambda i: (0, 0)),
                  pl.BlockSpec(memory_space=pl.ANY)],
        out_specs=pl.BlockSpec((tile, d), lambda i: (b0 * per_seq + i, 0)),
        input_output_aliases={4: 0},
        compiler_params=_cparams("parallel"),
        name="final",
    )(x1, y, mod3, fg, out_so_far)


def _peer_sc_u(idx, h2, u_pk, t):
    k = idx.shape[1]
    assert t % (SC_WORKERS * SC_TOKENS) == 0 and k % (2 * SC_ROWS_U) == 0 and k % (2 * SC_ROWS_V) == 0
    return _peer_u_sc(u_pk, idx.reshape(idx.shape[0] * k), h2, t, k)


def _peer_sc_v(idx, gate, pre, v_pk, tile, after):
    t, k = pre.shape
    w = _act(gate, pre, tile)
    return _peer_v_sc(v_pk, idx.reshape(idx.shape[0] * k), w.reshape(t * k), t, k, after)


TC_PEER_TOKENS = 8


def _peer_tc_kernel(idx_ref, idxn_ref, gate_ref, h2_ref, x1_ref, mod_ref, fg_ref, u_hbm, v_hbm,
                    out_so_far_ref, o_ref, ubuf, vbuf, sem):
    del out_so_far_ref
    i = pl.program_id(0)
    tt, k = gate_ref.shape
    d = x1_ref.shape[-1]
    words = d // 2
    slot = i % 2

    def row_copy(tab, buf, which, sl, src_row, dst_row):
        return pltpu.make_async_copy(tab.at[pl.ds(src_row, 1)], buf.at[sl, pl.ds(dst_row, 1)],
                                     sem.at[which, sl])

    def issue_token(iref, sl, t):
        for j in range(k):
            row = iref[t, j]
            row_copy(u_hbm, ubuf, 0, sl, row, t * k + j).start(priority=j % 2)
            row_copy(v_hbm, vbuf, 1, sl, row, t * k + j).start(priority=(j + 1) % 2)

    @pl.when(i == 0)
    def _():
        def tok(t, carry):
            issue_token(idx_ref, 0, t)
            return carry
        lax.fori_loop(0, tt, tok, 0)

    @pl.when(i + 1 < pl.num_programs(0))
    def _():
        for t in range(tt):
            issue_token(idxn_ref, 1 - slot, t)

    pltpu.make_async_copy(u_hbm.at[pl.ds(0, tt * k)], ubuf.at[slot], sem.at[0, slot]).wait()
    pltpu.make_async_copy(v_hbm.at[pl.ds(0, tt * k)], vbuf.at[slot], sem.at[1, slot]).wait()

    def unpack(x):
        return _unpack_lo(x), _unpack_hi(x)

    cols = []
    for t in range(tt):
        lo, hi = unpack(ubuf[slot, t * k:(t + 1) * k, :])
        cols.append(jnp.sum(lo * h2_ref[t:t + 1, :words] + hi * h2_ref[t:t + 1, words:],
                            axis=1, keepdims=True))
    pre = jnp.concatenate(cols, axis=1)
    w = gate_ref[...].T * _gelu_exact(pre)
    outs = []
    for t in range(tt):
        lo, hi = unpack(vbuf[slot, t * k:(t + 1) * k, :])
        wt = w[:, t:t + 1]
        outs.append(jnp.concatenate([jnp.sum(wt * lo, axis=0, keepdims=True),
                                     jnp.sum(wt * hi, axis=0, keepdims=True)], axis=1))
    y = jnp.concatenate(outs, axis=0)
    x2 = x1_ref[...] + mod_ref[:, 5 * d:6 * d] * y
    o_ref[...] = _rmsnorm(x2, fg_ref[...])


def _peer_tc(idx, gate, h2, x1, mod3, fg, u_pk, v_pk, out_so_far, seq, b0, t0):
    tg, k = idx.shape
    d = x1.shape[1]
    tt = TC_PEER_TOKENS
    n = (tg - t0) // tt
    first = t0 // tt
    per_seq = seq // tt
    row = lambda i: (first + i, 0)
    return pl.pallas_call(
        _peer_tc_kernel,
        out_shape=jax.ShapeDtypeStruct(out_so_far.shape, F32),
        grid=(n,),
        in_specs=[pl.BlockSpec((tt, k), row, memory_space=pltpu.SMEM),
                  pl.BlockSpec((tt, k), lambda i: (first + jnp.minimum(i + 1, n - 1), 0),
                               memory_space=pltpu.SMEM),
                  pl.BlockSpec((tt, k), row),
                  pl.BlockSpec((tt, d), row),
                  pl.BlockSpec((tt, d), row),
                  pl.BlockSpec((None, 1, N_MOD * d), lambda i: (b0 + (first + i) // per_seq, 0, 0)),
                  pl.BlockSpec((1, d), lambda i: (0, 0)),
                  pl.BlockSpec(memory_space=pl.ANY),
                  pl.BlockSpec(memory_space=pl.ANY),
                  pl.BlockSpec(memory_space=pl.ANY)],
        out_specs=pl.BlockSpec((tt, d), lambda i: (b0 * per_seq + first + i, 0)),
        scratch_shapes=[pltpu.VMEM((2, tt * k, d // 2), jnp.int32),
                        pltpu.VMEM((2, tt * k, d // 2), jnp.int32),
                        pltpu.SemaphoreType.DMA((2, 2))],
        input_output_aliases={9: 0},
        compiler_params=_cparams("arbitrary"),
        name="peer_tc",
    )(idx, idx, gate, h2, x1, mod3, fg, u_pk, v_pk, out_so_far)


def _rope_tables(length):
    rows = length // GRID_W
    row = jnp.repeat(jnp.arange(rows, dtype=F32), GRID_W)
    col = jnp.tile(jnp.arange(GRID_W, dtype=F32), rows)
    inv_freq = ROPE_BASE ** (-jnp.arange(ROPE_PAIRS, dtype=F32) / ROPE_PAIRS)
    ang_r = row[:, None] * inv_freq
    ang_c = col[:, None] * inv_freq
    cos = jnp.concatenate([jnp.cos(ang_r)] * 2 + [jnp.cos(ang_c)] * 2, axis=-1)
    sin = jnp.concatenate([-jnp.sin(ang_r), jnp.sin(ang_r), -jnp.sin(ang_c), jnp.sin(ang_c)], axis=-1)
    reps = LANES // HEAD_DIM
    return jnp.tile(cos, (1, reps)), jnp.tile(sin, (1, reps))


def _layer(x, c, ctx, c_ctx, w_mod, b_mod, n1, n2, w_in, sink, conv_w, w_attn_out, w_conv_out,
           w_mix_out, pw_q, p_keys, p_u, p_v, final_g, tm, tq, tr, tt, groups):
    batch, seq, d = x.shape
    t = batch * seq
    aw = N_Q_HEADS * HEAD_DIM
    kw = N_KV_HEADS * HEAD_DIM

    rows = -(-(batch + 1) // 8) * 8
    cond = jnp.zeros((rows, d), F32).at[:batch].set(c).at[batch].set(c_ctx)
    mod3 = _adaln(cond, w_mod, b_mod).reshape(rows, 1, N_MOD * d)

    w_in_b = w_in.astype(BF16)
    kvc = _ctx_kv(ctx, mod3, batch, n1.reshape(1, d), w_in_b[:, aw:aw + 2 * kw])

    cos, sin = _rope_tables(seq)
    x2 = x.reshape(t, d)
    wa, wc, wm = w_attn_out.astype(BF16), w_conv_out.astype(BF16), w_mix_out.astype(BF16)
    wq, keys = pw_q.astype(BF16), p_keys.astype(BF16)
    u_pk, v_pk = _pack_table(p_u), _pack_table(p_v)
    nsel = PEER_HEADS * PEER_TOPK

    fg = final_g.reshape(1, d)
    out = pl.empty((t, d), F32)
    b0 = 0
    pending = None

    def finish(p, after, out):
        idx, gate, pre, x1, pb0 = p
        y = _peer_sc_v(idx, gate, pre, v_pk, tt, after)
        return _final(x1, y, mod3, fg, out, seq, tt, pb0)

    for g, nb in enumerate(groups):
        tg = nb * seq
        t_sc = _sparsecore_share(tg, g)
        q, kv, gb, cu, ga, gv = _proj(x2, mod3, n1.reshape(1, d), w_in_b, cos, sin, seq, tm, b0, nb)
        x1 = _mixer(x2, q, kv, kvc, cu, gb, ga, gv, sink.reshape(1, N_Q_HEADS), conv_w,
                    wa, wc, wm, mod3, seq, tq, b0, nb)
        h2, idx_t, gate_t = _route(x1, mod3, n2.reshape(1, d), wq, keys, seq, tr, b0)
        idx, gate = idx_t.reshape(nsel, tg).T, gate_t.reshape(nsel, tg).T
        pre = _peer_sc_u(idx, h2, u_pk, t_sc)
        if pending is not None:
            out = finish(pending, pre, out)
        pending = (idx, gate, pre, x1, b0)
        if t_sc < tg:
            out = _peer_tc(idx, gate, h2, x1, mod3, fg, u_pk, v_pk, out, seq, b0, t_sc)
        b0 += nb
    out = finish(pending, pending[2], out)
    return out.reshape(batch, seq, d)


MAX_TOKEN_GROUPS = 8
SC_SHARE_NUMS, SC_SHARE_DEN = (13, 14), 16


def _sparsecore_share(tokens, group):
    unit = SC_WORKERS * SC_TOKENS
    t_sc = tokens * SC_SHARE_NUMS[group % len(SC_SHARE_NUMS)] // SC_SHARE_DEN // unit * unit
    if t_sc == 0 or (tokens - t_sc) % TC_PEER_TOKENS:
        return tokens
    return t_sc


def _token_groups(batch, seq):
    unit = SC_WORKERS * SC_TOKENS
    for groups in range(min(MAX_TOKEN_GROUPS, batch), 0, -1):
        nb = batch // groups
        if batch % groups == 0 and (nb * seq) % unit == 0:
            if nb % 2 == 0 and (nb // 2 * seq) % unit == 0:
                return [nb // 2] + [nb] * (groups - 1) + [nb // 2]
            return [nb] * groups
    raise ValueError("token count must be a multiple of the SparseCore work split")


def kernel(x, c, ctx, c_ctx, w_mod, b_mod, norm1_g, norm2_g, w_in, attn_sink, conv_w, w_attn_out,
           w_conv_out, w_mix_out, peer_w_q, peer_sub_keys, peer_u, peer_v, final_g):
    assert w_mod.shape[0] == 1, "only the single-layer configuration is implemented"
    seq = x.shape[1]
    return _layer(x, c, ctx, c_ctx, w_mod[0], b_mod[0], norm1_g[0], norm2_g[0], w_in[0],
                  attn_sink[0], conv_w[0], w_attn_out[0], w_conv_out[0], w_mix_out[0],
                  peer_w_q[0], peer_sub_keys[0], peer_u[0], peer_v[0], final_g,
                  tm=min(512, seq), tq=min(256, seq), tr=min(256, seq), tt=min(256, seq),
                  groups=_token_groups(x.shape[0], seq))
```

```python
import dataclasses

import jax
import jax.numpy as jnp
from jax import lax
from jax.experimental import pallas as pl
from jax.experimental.pallas import tpu as pltpu
from jax.experimental.pallas import tpu_sc as plsc

HEAD_DIM = 64
N_Q_HEADS = 8
N_KV_HEADS = 2
Q_PER_KV = N_Q_HEADS // N_KV_HEADS
WINDOW = 128
GRID_W = 64
ROPE_BASE = 10000.0
ROPE_PAIRS = HEAD_DIM // 4
PEER_HEADS = 8
PEER_N_KEYS = 128
PEER_TOPK = 16
N_MOD = 6
EPS = 1e-6
NEG_INF = -1e30

LANES = 128
VMEM_LIMIT = 56 * 1024 * 1024

F32 = jnp.float32
BF16 = jnp.bfloat16


def _cparams(*sem):
    return pltpu.CompilerParams(dimension_semantics=sem, vmem_limit_bytes=VMEM_LIMIT)


def _rmsnorm(x, g):
    return x * lax.rsqrt(jnp.mean(x * x, axis=-1, keepdims=True) + EPS) * g


def _gelu_exact(x):
    return 0.5 * x * (1.0 + lax.erf(x * (2.0 ** -0.5)))


def _adaln_kernel(cond_ref, w_ref, b_ref, o_ref):
    act = jax.nn.silu(cond_ref[...])
    o_ref[...] = jnp.dot(act, w_ref[...], precision=lax.Precision.HIGHEST,
                         preferred_element_type=F32) + b_ref[...]


def _adaln(cond, w_mod, b_mod):
    rows, d = cond.shape
    n = w_mod.shape[1]
    tn = d
    return pl.pallas_call(
        _adaln_kernel,
        out_shape=jax.ShapeDtypeStruct((rows, n), F32),
        grid=(n // tn,),
        in_specs=[pl.BlockSpec((rows, d), lambda j: (0, 0)),
                  pl.BlockSpec((d, tn), lambda j: (0, j)),
                  pl.BlockSpec((1, tn), lambda j: (0, j))],
        out_specs=pl.BlockSpec((rows, tn), lambda j: (0, j)),
        compiler_params=_cparams("parallel"),
        name="adaln",
    )(cond, w_mod, b_mod.reshape(1, n))


def _ctx_kv_kernel(xc_ref, mod_ref, n1_ref, w_ref, o_ref):
    d = xc_ref.shape[-1]
    mod = mod_ref[...]
    hc = _rmsnorm(xc_ref[...], n1_ref[...]) * (1.0 + mod[:, d:2 * d]) + mod[:, 0:d]
    o_ref[...] = jnp.dot(hc.astype(BF16), w_ref[...], preferred_element_type=F32).astype(BF16)


def _ctx_kv(ctx, mod3, ctx_row, n1, w_kv):
    b, c, d = ctx.shape
    kvw = w_kv.shape[1]
    return pl.pallas_call(
        _ctx_kv_kernel,
        out_shape=jax.ShapeDtypeStruct((b, c, kvw), BF16),
        grid=(b,),
        in_specs=[pl.BlockSpec((None, c, d), lambda i: (i, 0, 0)),
                  pl.BlockSpec((None, 1, N_MOD * d), lambda i: (ctx_row, 0, 0)),
                  pl.BlockSpec((1, d), lambda i: (0, 0)),
                  pl.BlockSpec((d, kvw), lambda i: (0, 0))],
        out_specs=pl.BlockSpec((None, c, kvw), lambda i: (i, 0, 0)),
        compiler_params=_cparams("parallel"),
        name="ctx_kv",
    )(ctx, mod3, n1, w_kv)


def _proj_kernel(x_ref, mod_ref, n1_ref, w_ref, cos_ref, sin_ref,
                 q_ref, kv_ref, gb_ref, cu_ref, ga_ref, gv_ref):
    d = x_ref.shape[-1]
    aw = q_ref.shape[-1]
    kw = kv_ref.shape[-1] // 2
    cw = gb_ref.shape[-1]
    mod = mod_ref[...]
    h = (_rmsnorm(x_ref[...], n1_ref[...]) * (1.0 + mod[:, d:2 * d]) + mod[:, 0:d]).astype(BF16)
    cos = cos_ref[...]
    sin = sin_ref[...]
    lane = lax.broadcasted_iota(jnp.int32, cos.shape, 1)
    first_half = (lane % (2 * ROPE_PAIRS)) < ROPE_PAIRS

    def rope(z):
        partner = jnp.where(first_half, pltpu.roll(z, LANES - ROPE_PAIRS, 1),
                            pltpu.roll(z, ROPE_PAIRS, 1))
        return z * cos + partner * sin

    def proj(lo, width):
        return jnp.dot(h, w_ref[:, lo:lo + width], preferred_element_type=F32)

    off = 0
    zq = proj(off, aw)
    for g in range(aw // LANES):
        q_ref[:, g * LANES:(g + 1) * LANES] = rope(zq[:, g * LANES:(g + 1) * LANES]).astype(BF16)
    off += aw
    zkv = proj(off, 2 * kw)
    for g in range(kw // LANES):
        kv_ref[:, g * LANES:(g + 1) * LANES] = rope(zkv[:, g * LANES:(g + 1) * LANES]).astype(BF16)
    kv_ref[:, kw:] = zkv[:, kw:].astype(BF16)
    off += 2 * kw
    gb_ref[...] = proj(off, cw)
    off += cw
    zc = proj(off, cw)
    off += cw
    cu_ref[...] = zc * proj(off, cw)
    off += cw
    ga_ref[...] = proj(off, d)
    off += d
    gv_ref[...] = proj(off, d)


def _proj(x2, mod3, n1, w_in, cos, sin, seq, tm, b0, nb):
    d = x2.shape[1]
    t = nb * seq
    aw = N_Q_HEADS * HEAD_DIM
    kw = N_KV_HEADS * HEAD_DIM
    cw = d // 2
    per_seq = seq // tm
    row = lambda i: (i, 0)
    return pl.pallas_call(
        _proj_kernel,
        out_shape=(jax.ShapeDtypeStruct((t, aw), BF16),
                   jax.ShapeDtypeStruct((t, 2 * kw), BF16),
                   jax.ShapeDtypeStruct((t, cw), F32),
                   jax.ShapeDtypeStruct((t, cw), F32),
                   jax.ShapeDtypeStruct((t, d), F32),
                   jax.ShapeDtypeStruct((t, d), F32)),
        grid=(t // tm,),
        in_specs=[pl.BlockSpec((tm, d), lambda i: (b0 * per_seq + i, 0)),
                  pl.BlockSpec((None, 1, N_MOD * d), lambda i: (b0 + i // per_seq, 0, 0)),
                  pl.BlockSpec((1, d), lambda i: (0, 0)),
                  pl.BlockSpec(w_in.shape, lambda i: (0, 0)),
                  pl.BlockSpec((tm, LANES), lambda i: (i % per_seq, 0)),
                  pl.BlockSpec((tm, LANES), lambda i: (i % per_seq, 0))],
        out_specs=(pl.BlockSpec((tm, aw), row), pl.BlockSpec((tm, 2 * kw), row),
                   pl.BlockSpec((tm, cw), row), pl.BlockSpec((tm, cw), row),
                   pl.BlockSpec((tm, d), row), pl.BlockSpec((tm, d), row)),
        compiler_params=_cparams("parallel"),
        name="proj",
    )(x2, mod3, n1, w_in, cos, sin)


def _mixer_kernel(sink_ref, x_ref, q_ref, kv_ref, kvp_ref, kvn_ref, kvc_ref,
                  cu_ref, cup_ref, cun_ref, gb_ref, ga_ref, gv_ref, convw_ref,
                  wa_ref, wc_ref, wm_ref, mod_ref, o_ref, attn_scr):
    n = pl.program_id(1)
    has_prev = n > 0
    has_next = n < pl.num_programs(1) - 1
    tq, d = x_ref.shape
    kw = N_KV_HEADS * HEAD_DIM
    scale = HEAD_DIM ** -0.5
    nt = (((1,), (1,)), ((), ()))

    kext = jnp.concatenate([kvp_ref[...], kv_ref[...], kvn_ref[...]], axis=0)
    kctx = kvc_ref[...]
    qi = lax.broadcasted_iota(jnp.int32, (WINDOW, WINDOW), 0)
    ki = lax.broadcasted_iota(jnp.int32, (WINDOW, WINDOW), 1)
    nblk = tq // WINDOW
    for j in range(nblk):
        prev_ok = ki >= qi
        next_ok = ki <= qi
        if j == 0:
            prev_ok = jnp.logical_and(prev_ok, has_prev)
        if j == nblk - 1:
            next_ok = jnp.logical_and(next_ok, has_next)
        mask = jnp.concatenate([prev_ok, jnp.ones_like(prev_ok), next_ok], axis=1)
        rows = slice(j * WINDOW, (j + 1) * WINDOW)
        krows = slice(j * WINDOW, (j + 3) * WINDOW)
        for hq in range(N_Q_HEADS):
            kvh = hq // Q_PER_KV
            kcol = slice(kvh * HEAD_DIM, (kvh + 1) * HEAD_DIM)
            vcol = slice(kw + kvh * HEAD_DIM, kw + (kvh + 1) * HEAD_DIM)
            qh = q_ref[rows, hq * HEAD_DIM:(hq + 1) * HEAD_DIM]
            s_loc = lax.dot_general(qh, kext[krows, kcol], nt, preferred_element_type=F32) * scale
            s_loc = jnp.where(mask, s_loc, NEG_INF)
            s_ctx = lax.dot_general(qh, kctx[:, kcol], nt, preferred_element_type=F32) * scale
            sink = sink_ref[0, hq]
            m = jnp.maximum(jnp.max(s_loc, axis=-1, keepdims=True),
                            jnp.max(s_ctx, axis=-1, keepdims=True))
            m = jnp.maximum(m, sink)
            p_loc = jnp.exp(s_loc - m)
            p_ctx = jnp.exp(s_ctx - m)
            den = (jnp.sum(p_loc, axis=-1, keepdims=True) + jnp.sum(p_ctx, axis=-1, keepdims=True)
                   + jnp.exp(sink - m))
            o = (jnp.dot(p_loc.astype(BF16), kext[krows, vcol], preferred_element_type=F32)
                 + jnp.dot(p_ctx.astype(BF16), kctx[:, vcol], preferred_element_type=F32))
            attn_scr[rows, hq * HEAD_DIM:(hq + 1) * HEAD_DIM] = o / den

    y_attn = jnp.dot(attn_scr[...].astype(BF16), wa_ref[...], preferred_element_type=F32)

    cu = cu_ref[...]
    ri = lax.broadcasted_iota(jnp.int32, cu.shape, 0)
    prev_row = jnp.where(has_prev, cup_ref[7:8, :], 0.0)
    next_row = jnp.where(has_next, cun_ref[0:1, :], 0.0)
    cu_m1 = jnp.where(ri == 0, prev_row, pltpu.roll(cu, 1, 0))
    cu_p1 = jnp.where(ri == tq - 1, next_row, pltpu.roll(cu, tq - 1, 0))
    cw = convw_ref[...]
    conv = cu_m1 * cw[0:1, :] + cu * cw[1:2, :] + cu_p1 * cw[2:3, :]
    y_conv = jnp.dot((gb_ref[...] * conv).astype(BF16), wc_ref[...], preferred_element_type=F32)

    merged = jax.nn.sigmoid(ga_ref[...]) * y_attn + jax.nn.sigmoid(gv_ref[...]) * y_conv
    y = jnp.dot(merged.astype(BF16), wm_ref[...], preferred_element_type=F32)
    g1 = mod_ref[:, 2 * d:3 * d]
    o_ref[...] = x_ref[...] + g1 * y


def _mixer(x2, q, kv, kvc, cu, gb, ga, gv, sink, conv_w, wa, wc, wm, mod3, seq, tq, b0, batch):
    d = x2.shape[1]
    t, aw = q.shape
    kv2 = kv.shape[1]
    cw = cu.shape[1]
    c = kvc.shape[1]
    nq = seq // tq
    nb = seq // WINDOW
    sub = tq // WINDOW
    kv3 = kv.reshape(t // WINDOW, WINDOW, kv2)
    cu3 = cu.reshape(t // 8, 8, cw)
    tile = lambda b, n: (b * nq + n, 0)
    const = lambda b, n: (0, 0)
    return pl.pallas_call(
        _mixer_kernel,
        out_shape=jax.ShapeDtypeStruct((t, d), F32),
        grid=(batch, nq),
        in_specs=[
            pl.BlockSpec(memory_space=pltpu.SMEM),
            pl.BlockSpec((tq, d), lambda b, n: ((b0 + b) * nq + n, 0)),
            pl.BlockSpec((tq, aw), tile),
            pl.BlockSpec((tq, kv2), tile),
            pl.BlockSpec((None, WINDOW, kv2), lambda b, n: (b * nb + jnp.maximum(n * sub - 1, 0), 0, 0)),
            pl.BlockSpec((None, WINDOW, kv2), lambda b, n: (b * nb + jnp.minimum((n + 1) * sub, nb - 1), 0, 0)),
            pl.BlockSpec((None, c, kv2), lambda b, n: (b0 + b, 0, 0)),
            pl.BlockSpec((tq, cw), tile),
            pl.BlockSpec((None, 8, cw), lambda b, n: (jnp.maximum((b * seq + n * tq) // 8 - 1, 0), 0, 0)),
            pl.BlockSpec((None, 8, cw), lambda b, n: (jnp.minimum((b * seq + (n + 1) * tq) // 8, t // 8 - 1), 0, 0)),
            pl.BlockSpec((tq, cw), tile),
            pl.BlockSpec((tq, d), tile),
            pl.BlockSpec((tq, d), tile),
            pl.BlockSpec(conv_w.shape, const),
            pl.BlockSpec(wa.shape, const),
            pl.BlockSpec(wc.shape, const),
            pl.BlockSpec(wm.shape, const),
            pl.BlockSpec((None, 1, N_MOD * d), lambda b, n: (b0 + b, 0, 0)),
        ],
        out_specs=pl.BlockSpec((tq, d), tile),
        scratch_shapes=[pltpu.VMEM((tq, aw), F32)],
        compiler_params=_cparams("parallel", "parallel"),
        name="mixer",
    )(sink, x2, q, kv, kv3, kv3, kvc, cu, cu3, cu3, gb, ga, gv, conv_w, wa, wc, wm, mod3)


def _topk_rows(s, k, payload=None):
    n = s.shape[0]
    rows = lax.broadcasted_iota(jnp.int32, s.shape, 0).astype(F32)
    vals, picks = [], []
    for _ in range(k):
        m = jnp.max(s, axis=0, keepdims=True)
        am = jnp.min(jnp.where(s == m, rows, float(n)), axis=0, keepdims=True)
        hit = rows == am
        vals.append(m)
        if payload is None:
            picks.append(am)
        else:
            picks.append(jnp.max(jnp.where(hit, payload, -1.0), axis=0, keepdims=True))
        s = jnp.where(hit, -jnp.inf, s)
    return jnp.concatenate(vals, axis=0), jnp.concatenate(picks, axis=0)


def _route_kernel(x1_ref, mod_ref, n2_ref, wq_ref, keys_ref, h2_ref, idx_ref, gate_ref, h2b_scr):
    d = x1_ref.shape[-1]
    nt = (((1,), (1,)), ((), ()))

    @pl.when(pl.program_id(1) == 0)
    def _():
        mod = mod_ref[...]
        h2 = _rmsnorm(x1_ref[...], n2_ref[...]) * (1.0 + mod[:, 4 * d:5 * d]) + mod[:, 3 * d:4 * d]
        h2_ref[...] = h2
        h2b_scr[...] = h2.astype(BF16)

    qp = jnp.dot(h2b_scr[...], wq_ref[...], preferred_element_type=F32)
    half = qp.shape[1] // 2
    tops = []
    for p in range(2):
        qh = qp[:, p * half:(p + 1) * half].astype(BF16)
        st = lax.dot_general(keys_ref[p], qh, nt, preferred_element_type=F32)
        tops.append(_topk_rows(st, PEER_TOPK))
    (a, ia), (b, ib) = tops
    width = [PEER_TOPK // (i + 1) for i in range(PEER_TOPK)]
    pad = -sum(width) % 8
    tm = a.shape[1]
    cand = jnp.concatenate([a[i:i + 1, :] + b[:width[i], :] for i in range(PEER_TOPK)]
                           + [jnp.full((pad, tm), -jnp.inf, F32)], axis=0)
    cidx = jnp.concatenate([ia[i:i + 1, :] * PEER_N_KEYS + ib[:width[i], :] for i in range(PEER_TOPK)]
                           + [jnp.zeros((pad, tm), F32)], axis=0)
    best, idx = _topk_rows(cand, PEER_TOPK, payload=cidx)
    e = jnp.exp(best - best[0:1, :])
    gate_ref[...] = e / jnp.sum(e, axis=0, keepdims=True)
    idx_ref[...] = idx.astype(jnp.int32)


def _route(x1, mod3, n2, wq, keys, seq, tm, b0):
    t, d = x1.shape
    hw = wq.shape[1] // PEER_HEADS
    per_seq = seq // tm
    return pl.pallas_call(
        _route_kernel,
        out_shape=(jax.ShapeDtypeStruct((t, d), F32),
                   jax.ShapeDtypeStruct((PEER_HEADS, PEER_TOPK, t), jnp.int32),
                   jax.ShapeDtypeStruct((PEER_HEADS, PEER_TOPK, t), F32)),
        grid=(t // tm, PEER_HEADS),
        in_specs=[pl.BlockSpec((tm, d), lambda i, h: (i, 0)),
                  pl.BlockSpec((None, 1, N_MOD * d), lambda i, h: (b0 + i // per_seq, 0, 0)),
                  pl.BlockSpec((1, d), lambda i, h: (0, 0)),
                  pl.BlockSpec((d, hw), lambda i, h: (0, h)),
                  pl.BlockSpec((None, 2, PEER_N_KEYS, hw // 2), lambda i, h: (h, 0, 0, 0))],
        out_specs=(pl.BlockSpec((tm, d), lambda i, h: (i, 0)),
                   pl.BlockSpec((None, PEER_TOPK, tm), lambda i, h: (h, 0, i)),
                   pl.BlockSpec((None, PEER_TOPK, tm), lambda i, h: (h, 0, i))),
        scratch_shapes=[pltpu.VMEM((tm, d), BF16)],
        compiler_params=_cparams("parallel", "arbitrary"),
        name="route",
    )(x1, mod3, n2, wq, keys)


SC_CORES = 2
SC_SUBCORES = 16
SC_LANES = 16
SC_WORKERS = SC_CORES * SC_SUBCORES
SC_ROWS_U = 64
SC_ROWS_V = 64
SC_ROWS_PER_ITER = 8
SC_TOKENS = 16


def _sc_mesh():
    return plsc.VectorSubcoreMesh(core_axis_name="c", subcore_axis_name="s")


def _sc_params():
    return dataclasses.replace(pltpu.CompilerParams(), needs_layout_passes=False)


def _sc_worker_base(tokens_per_worker):
    return (lax.axis_index("s") * SC_CORES + lax.axis_index("c")) * tokens_per_worker


def _sc_chunk_pipeline(tab_hbm, idx_v, bufs, n_chunks, compute):
    rows = bufs[0][0].shape[0]

    def gather(g, b):
        buf, sem = bufs[b]
        return pltpu.make_async_copy(tab_hbm.at[idx_v.at[pl.ds(g * rows, rows)]], buf, sem)

    gather(0, 0).start()

    @pl.loop(0, n_chunks, step=2)
    def _(g):
        gather(g, 0).wait()
        gather(g + 1, 1).start()
        compute(g, bufs[0][0])
        gather(g + 1, 1).wait()

        @pl.when(g + 2 < n_chunks)
        def _():
            gather(g + 2, 0).start()

        compute(g + 1, bufs[1][0])


def _pack_table(tab):
    half = tab.shape[1] // 2
    lo = lax.bitcast_convert_type(tab[:, :half].astype(BF16), jnp.uint16).astype(jnp.int32)
    bits = lax.bitcast_convert_type(tab[:, half:], jnp.int32)
    sign = bits & jnp.int32(-2 ** 31)
    magnitude = bits & jnp.int32(2 ** 31 - 1)
    top = jnp.maximum((magnitude - lo + 0x8000) >> 16, 0)
    return sign | ((top << 16) + lo)


def _unpack_lo(x):
    return lax.bitcast_convert_type(x << 16, F32)


def _unpack_hi(x):
    return lax.bitcast_convert_type(x, F32)


def _peer_u_sc(u_pk, idx_flat, h2, t, k):
    d = h2.shape[1]
    words = u_pk.shape[1]
    tpw = t // SC_WORKERS
    rows = SC_ROWS_U
    cpt = k // rows
    cpb = SC_TOKENS * cpt
    nj = words // SC_LANES

    def body(u_hbm, idx_hbm, h_hbm, o_hbm, idx_v, h_v, buf0, buf1, acc_v, pre_v, sem0, sem1):
        base = _sc_worker_base(tpw)
        lanes = lax.iota(jnp.int32, SC_LANES)

        def compute(g, buf):
            tl = g // cpt
            c = g % cpt

            @plsc.parallel_loop(0, rows // SC_ROWS_PER_ITER)
            def _(it):
                r0 = it * SC_ROWS_PER_ITER
                accs = [[None, None] for _ in range(SC_ROWS_PER_ITER)]
                for j in range(nj):
                    h_lo = h_v[tl, pl.ds(j * SC_LANES, SC_LANES)]
                    h_hi = h_v[tl, pl.ds(words + j * SC_LANES, SC_LANES)]
                    for a in range(SC_ROWS_PER_ITER):
                        x = buf[r0 + a, pl.ds(j * SC_LANES, SC_LANES)]
                        term = _unpack_lo(x) * h_lo + _unpack_hi(x) * h_hi
                        accs[a][j % 2] = term if accs[a][j % 2] is None else accs[a][j % 2] + term
                for a in range(SC_ROWS_PER_ITER):
                    acc_v[r0 + a, :] = accs[a][0] + accs[a][1]

            for q in range(rows // SC_LANES):
                s = plsc.load_gather(acc_v, [lanes + q * SC_LANES, jnp.zeros((SC_LANES,), jnp.int32)])
                for l in range(1, SC_LANES):
                    s = s + plsc.load_gather(acc_v, [lanes + q * SC_LANES,
                                                     jnp.full((SC_LANES,), l, jnp.int32)])
                pre_v[tl, pl.ds(c * rows + q * SC_LANES, SC_LANES)] = s

        @pl.loop(0, tpw // SC_TOKENS)
        def _(blk):
            tok0 = base + blk * SC_TOKENS
            pltpu.sync_copy(idx_hbm.at[pl.ds(tok0 * k, SC_TOKENS * k)], idx_v)
            pltpu.sync_copy(h_hbm.at[pl.ds(tok0, SC_TOKENS)], h_v)
            _sc_chunk_pipeline(u_hbm, idx_v, ((buf0, sem0), (buf1, sem1)), cpb, compute)
            pltpu.sync_copy(pre_v, o_hbm.at[pl.ds(tok0, SC_TOKENS)])

    return pl.kernel(
        body,
        out_type=jax.ShapeDtypeStruct((t, k), F32),
        mesh=_sc_mesh(),
        scratch_types=[pltpu.VMEM((SC_TOKENS * k,), jnp.int32),
                       pltpu.VMEM((SC_TOKENS, d), F32),
                       pltpu.VMEM((rows, words), jnp.int32),
                       pltpu.VMEM((rows, words), jnp.int32),
                       pltpu.VMEM((rows, SC_LANES), F32),
                       pltpu.VMEM((SC_TOKENS, k), F32),
                       pltpu.SemaphoreType.DMA,
                       pltpu.SemaphoreType.DMA],
        compiler_params=_sc_params(),
        name="peer_u_sc",
    )(u_pk, idx_flat, h2)


def _peer_v_sc(v_pk, idx_flat, w_flat, t, k, after):
    words = v_pk.shape[1]
    d = 2 * words
    tpw = t // SC_WORKERS
    rows = SC_ROWS_V
    cpt = k // rows
    cpb = SC_TOKENS * cpt
    pw = words // 2
    nj = pw // SC_LANES

    def body(v_hbm, idx_hbm, w_hbm, after_hbm, o_hbm, idx_v, w_v, buf0, buf1, out_v, sem0, sem1):
        del after_hbm
        base = _sc_worker_base(tpw)

        def compute(g, buf):
            tl = g // cpt
            c = g % cpt
            for p in range(2):
                lo_cols = [pl.ds(p * pw + j * SC_LANES, SC_LANES) for j in range(nj)]
                hi_cols = [pl.ds(words + p * pw + j * SC_LANES, SC_LANES) for j in range(nj)]

                def row(r, accs):
                    wb = plsc.load_gather(w_v, [jnp.full((SC_LANES,), g * rows + r, jnp.int32)])
                    new = []
                    for j in range(nj):
                        x = buf[r, lo_cols[j]]
                        new.append(accs[2 * j] + _unpack_lo(x) * wb)
                        new.append(accs[2 * j + 1] + _unpack_hi(x) * wb)
                    return tuple(new)

                init = []
                for j in range(nj):
                    init.append(jnp.where(c == 0, 0.0, out_v[tl, lo_cols[j]]))
                    init.append(jnp.where(c == 0, 0.0, out_v[tl, hi_cols[j]]))
                accs = lax.fori_loop(0, rows, row, tuple(init))
                for j in range(nj):
                    out_v[tl, lo_cols[j]] = accs[2 * j]
                    out_v[tl, hi_cols[j]] = accs[2 * j + 1]

        @pl.loop(0, tpw // SC_TOKENS)
        def _(blk):
            tok0 = base + blk * SC_TOKENS
            pltpu.sync_copy(idx_hbm.at[pl.ds(tok0 * k, SC_TOKENS * k)], idx_v)
            pltpu.sync_copy(w_hbm.at[pl.ds(tok0 * k, SC_TOKENS * k)], w_v)
            _sc_chunk_pipeline(v_hbm, idx_v, ((buf0, sem0), (buf1, sem1)), cpb, compute)
            pltpu.sync_copy(out_v, o_hbm.at[pl.ds(tok0, SC_TOKENS)])

    return pl.kernel(
        body,
        out_type=jax.ShapeDtypeStruct((t, d), F32),
        mesh=_sc_mesh(),
        scratch_types=[pltpu.VMEM((SC_TOKENS * k,), jnp.int32),
                       pltpu.VMEM((SC_TOKENS * k,), F32),
                       pltpu.VMEM((rows, words), jnp.int32),
                       pltpu.VMEM((rows, words), jnp.int32),
                       pltpu.VMEM((SC_TOKENS, d), F32),
                       pltpu.SemaphoreType.DMA,
                       pltpu.SemaphoreType.DMA],
        compiler_params=_sc_params(),
        name="peer_v_sc",
    )(v_pk, idx_flat, w_flat, after)


def _act_kernel(gate_ref, pre_ref, o_ref):
    o_ref[...] = gate_ref[...] * _gelu_exact(pre_ref[...])


def _act(gate, pre, tile):
    t, k = pre.shape
    spec = pl.BlockSpec((tile, k), lambda i: (i, 0))
    return pl.pallas_call(
        _act_kernel,
        out_shape=jax.ShapeDtypeStruct((t, k), F32),
        grid=(t // tile,),
        in_specs=[spec, spec],
        out_specs=spec,
        compiler_params=_cparams("parallel"),
        name="expert_act",
    )(gate, pre)


def _final_kernel(x1_ref, y_ref, mod_ref, fg_ref, out_so_far_ref, o_ref):
    del out_so_far_ref
    d = x1_ref.shape[-1]
    x2 = x1_ref[...] + mod_ref[:, 5 * d:6 * d] * y_ref[...]
    o_ref[...] = _rmsnorm(x2, fg_ref[...])


def _final(x1, y, mod3, fg, out_so_far, seq, tile, b0):
    t, d = y.shape
    per_seq = seq // tile
    row = pl.BlockSpec((tile, d), lambda i: (i, 0))
    return pl.pallas_call(
        _final_kernel,
        out_shape=jax.ShapeDtypeStruct(out_so_far.shape, F32),
        grid=(t // tile,),
        in_specs=[row, row,
                  pl.BlockSpec((None, 1, N_MOD * d), lambda i: (b0 + i // per_seq, 0, 0)),
                  pl.BlockSpec((1, d), lambda i: (0, 0)),
                  pl.BlockSpec(memory_space=pl.ANY)],
        out_specs=pl.BlockSpec((tile, d), lambda i: (b0 * per_seq + i, 0)),
        input_output_aliases={4: 0},
        compiler_params=_cparams("parallel"),
        name="final",
    )(x1, y, mod3, fg, out_so_far)


def _peer_sc_u(idx, h2, u_pk, t):
    k = idx.shape[1]
    assert t % (SC_WORKERS * SC_TOKENS) == 0 and k % (2 * SC_ROWS_U) == 0 and k % (2 * SC_ROWS_V) == 0
    return _peer_u_sc(u_pk, idx.reshape(idx.shape[0] * k), h2, t, k)


def _peer_sc_v(idx, gate, pre, v_pk, tile, after):
    t, k = pre.shape
    w = _act(gate, pre, tile)
    return _peer_v_sc(v_pk, idx.reshape(idx.shape[0] * k), w.reshape(t * k), t, k, after)


TC_PEER_TOKENS = 8


def _peer_tc_kernel(idx_ref, idxn_ref, gate_ref, h2_ref, x1_ref, mod_ref, fg_ref, u_hbm, v_hbm,
                    out_so_far_ref, o_ref, ubuf, vbuf, sem):
    del out_so_far_ref
    i = pl.program_id(0)
    tt, k = gate_ref.shape
    d = x1_ref.shape[-1]
    words = d // 2
    slot = i % 2

    def row_copy(tab, buf, which, sl, src_row, dst_row):
        return pltpu.make_async_copy(tab.at[pl.ds(src_row, 1)], buf.at[sl, pl.ds(dst_row, 1)],
                                     sem.at[which, sl])

    def issue_token(iref, sl, t):
        for j in range(k):
            row = iref[t, j]
            row_copy(u_hbm, ubuf, 0, sl, row, t * k + j).start(priority=j % 2)
            row_copy(v_hbm, vbuf, 1, sl, row, t * k + j).start(priority=(j + 1) % 2)

    @pl.when(i == 0)
    def _():
        def tok(t, carry):
            issue_token(idx_ref, 0, t)
            return carry
        lax.fori_loop(0, tt, tok, 0)

    @pl.when(i + 1 < pl.num_programs(0))
    def _():
        for t in range(tt):
            issue_token(idxn_ref, 1 - slot, t)

    pltpu.make_async_copy(u_hbm.at[pl.ds(0, tt * k)], ubuf.at[slot], sem.at[0, slot]).wait()
    pltpu.make_async_copy(v_hbm.at[pl.ds(0, tt * k)], vbuf.at[slot], sem.at[1, slot]).wait()

    def unpack(x):
        return _unpack_lo(x), _unpack_hi(x)

    cols = []
    for t in range(tt):
        lo, hi = unpack(ubuf[slot, t * k:(t + 1) * k, :])
        cols.append(jnp.sum(lo * h2_ref[t:t + 1, :words] + hi * h2_ref[t:t + 1, words:],
                            axis=1, keepdims=True))
    pre = jnp.concatenate(cols, axis=1)
    w = gate_ref[...].T * _gelu_exact(pre)
    outs = []
    for t in range(tt):
        lo, hi = unpack(vbuf[slot, t * k:(t + 1) * k, :])
        wt = w[:, t:t + 1]
        outs.append(jnp.concatenate([jnp.sum(wt * lo, axis=0, keepdims=True),
                                     jnp.sum(wt * hi, axis=0, keepdims=True)], axis=1))
    y = jnp.concatenate(outs, axis=0)
    x2 = x1_ref[...] + mod_ref[:, 5 * d:6 * d] * y
    o_ref[...] = _rmsnorm(x2, fg_ref[...])


def _peer_tc(idx, gate, h2, x1, mod3, fg, u_pk, v_pk, out_so_far, seq, b0, t0):
    tg, k = idx.shape
    d = x1.shape[1]
    tt = TC_PEER_TOKENS
    n = (tg - t0) // tt
    first = t0 // tt
    per_seq = seq // tt
    row = lambda i: (first + i, 0)
    return pl.pallas_call(
        _peer_tc_kernel,
        out_shape=jax.ShapeDtypeStruct(out_so_far.shape, F32),
        grid=(n,),
        in_specs=[pl.BlockSpec((tt, k), row, memory_space=pltpu.SMEM),
                  pl.BlockSpec((tt, k), lambda i: (first + jnp.minimum(i + 1, n - 1), 0),
                               memory_space=pltpu.SMEM),
                  pl.BlockSpec((tt, k), row),
                  pl.BlockSpec((tt, d), row),
                  pl.BlockSpec((tt, d), row),
                  pl.BlockSpec((None, 1, N_MOD * d), lambda i: (b0 + (first + i) // per_seq, 0, 0)),
                  pl.BlockSpec((1, d), lambda i: (0, 0)),
                  pl.BlockSpec(memory_space=pl.ANY),
                  pl.BlockSpec(memory_space=pl.ANY),
                  pl.BlockSpec(memory_space=pl.ANY)],
        out_specs=pl.BlockSpec((tt, d), lambda i: (b0 * per_seq + first + i, 0)),
        scratch_shapes=[pltpu.VMEM((2, tt * k, d // 2), jnp.int32),
                        pltpu.VMEM((2, tt * k, d // 2), jnp.int32),
                        pltpu.SemaphoreType.DMA((2, 2))],
        input_output_aliases={9: 0},
        compiler_params=_cparams("arbitrary"),
        name="peer_tc",
    )(idx, idx, gate, h2, x1, mod3, fg, u_pk, v_pk, out_so_far)


def _rope_tables(length):
    rows = length // GRID_W
    row = jnp.repeat(jnp.arange(rows, dtype=F32), GRID_W)
    col = jnp.tile(jnp.arange(GRID_W, dtype=F32), rows)
    inv_freq = ROPE_BASE ** (-jnp.arange(ROPE_PAIRS, dtype=F32) / ROPE_PAIRS)
    ang_r = row[:, None] * inv_freq
    ang_c = col[:, None] * inv_freq
    cos = jnp.concatenate([jnp.cos(ang_r)] * 2 + [jnp.cos(ang_c)] * 2, axis=-1)
    sin = jnp.concatenate([-jnp.sin(ang_r), jnp.sin(ang_r), -jnp.sin(ang_c), jnp.sin(ang_c)], axis=-1)
    reps = LANES // HEAD_DIM
    return jnp.tile(cos, (1, reps)), jnp.tile(sin, (1, reps))


def _layer(x, c, ctx, c_ctx, w_mod, b_mod, n1, n2, w_in, sink, conv_w, w_attn_out, w_conv_out,
           w_mix_out, pw_q, p_keys, p_u, p_v, final_g, tm, tq, tr, tt, groups):
    batch, seq, d = x.shape
    t = batch * seq
    aw = N_Q_HEADS * HEAD_DIM
    kw = N_KV_HEADS * HEAD_DIM

    rows = -(-(batch + 1) // 8) * 8
    cond = jnp.zeros((rows, d), F32).at[:batch].set(c).at[batch].set(c_ctx)
    mod3 = _adaln(cond, w_mod, b_mod).reshape(rows, 1, N_MOD * d)

    w_in_b = w_in.astype(BF16)
    kvc = _ctx_kv(ctx, mod3, batch, n1.reshape(1, d), w_in_b[:, aw:aw + 2 * kw])

    cos, sin = _rope_tables(seq)
    x2 = x.reshape(t, d)
    wa, wc, wm = w_attn_out.astype(BF16), w_conv_out.astype(BF16), w_mix_out.astype(BF16)
    wq, keys = pw_q.astype(BF16), p_keys.astype(BF16)
    u_pk, v_pk = _pack_table(p_u), _pack_table(p_v)
    nsel = PEER_HEADS * PEER_TOPK

    fg = final_g.reshape(1, d)
    out = pl.empty((t, d), F32)
    b0 = 0
    pending = None

    def finish(p, after, out):
        idx, gate, pre, x1, pb0 = p
        y = _peer_sc_v(idx, gate, pre, v_pk, tt, after)
        return _final(x1, y, mod3, fg, out, seq, tt, pb0)

    for g, nb in enumerate(groups):
        tg = nb * seq
        t_sc = _sparsecore_share(tg, g)
        q, kv, gb, cu, ga, gv = _proj(x2, mod3, n1.reshape(1, d), w_in_b, cos, sin, seq, tm, b0, nb)
        x1 = _mixer(x2, q, kv, kvc, cu, gb, ga, gv, sink.reshape(1, N_Q_HEADS), conv_w,
                    wa, wc, wm, mod3, seq, tq, b0, nb)
        h2, idx_t, gate_t = _route(x1, mod3, n2.reshape(1, d), wq, keys, seq, tr, b0)
        idx, gate = idx_t.reshape(nsel, tg).T, gate_t.reshape(nsel, tg).T
        pre = _peer_sc_u(idx, h2, u_pk, t_sc)
        if pending is not None:
            out = finish(pending, pre, out)
        pending = (idx, gate, pre, x1, b0)
        if t_sc < tg:
            out = _peer_tc(idx, gate, h2, x1, mod3, fg, u_pk, v_pk, out, seq, b0, t_sc)
        b0 += nb
    out = finish(pending, pending[2], out)
    return out.reshape(batch, seq, d)


MAX_TOKEN_GROUPS = 8
SC_SHARE_NUMS, SC_SHARE_DEN = (14, 12, 14, 14, 12, 14, 14, 12), 16


def _sparsecore_share(tokens, group):
    unit = SC_WORKERS * SC_TOKENS
    t_sc = tokens * SC_SHARE_NUMS[group % len(SC_SHARE_NUMS)] // SC_SHARE_DEN // unit * unit
    if t_sc == 0 or (tokens - t_sc) % TC_PEER_TOKENS:
        return tokens
    return t_sc


def _token_groups(batch, seq):
    unit = SC_WORKERS * SC_TOKENS
    for groups in range(min(MAX_TOKEN_GROUPS, batch), 0, -1):
        nb = batch // groups
        if batch % groups == 0 and (nb * seq) % unit == 0:
            return [nb] * groups
    raise ValueError("token count must be a multiple of the SparseCore work split")


def kernel(x, c, ctx, c_ctx, w_mod, b_mod, norm1_g, norm2_g, w_in, attn_sink, conv_w, w_attn_out,
           w_conv_out, w_mix_out, peer_w_q, peer_sub_keys, peer_u, peer_v, final_g):
    assert w_mod.shape[0] == 1, "only the single-layer configuration is implemented"
    seq = x.shape[1]
    return _layer(x, c, ctx, c_ctx, w_mod[0], b_mod[0], norm1_g[0], norm2_g[0], w_in[0],
                  attn_sink[0], conv_w[0], w_attn_out[0], w_conv_out[0], w_mix_out[0],
                  peer_w_q[0], peer_sub_keys[0], peer_u[0], peer_v[0], final_g,
                  tm=min(512, seq), tq=min(256, seq), tr=min(256, seq), tt=min(256, seq),
                  groups=_token_groups(x.shape[0], seq))
```

```python
import dataclasses

import jax
import jax.numpy as jnp
from jax import lax
from jax.experimental import pallas as pl
from jax.experimental.pallas import tpu as pltpu
from jax.experimental.pallas import tpu_sc as plsc

HEAD_DIM = 64
N_Q_HEADS = 8
N_KV_HEADS = 2
Q_PER_KV = N_Q_HEADS // N_KV_HEADS
WINDOW = 128
GRID_W = 64
ROPE_BASE = 10000.0
ROPE_PAIRS = HEAD_DIM // 4
PEER_HEADS = 8
PEER_N_KEYS = 128
PEER_TOPK = 16
N_MOD = 6
EPS = 1e-6
NEG_INF = -1e30

LANES = 128
VMEM_LIMIT = 56 * 1024 * 1024

F32 = jnp.float32
BF16 = jnp.bfloat16


def _cparams(*sem):
    return pltpu.CompilerParams(dimension_semantics=sem, vmem_limit_bytes=VMEM_LIMIT)


def _rmsnorm(x, g):
    return x * lax.rsqrt(jnp.mean(x * x, axis=-1, keepdims=True) + EPS) * g


def _gelu_exact(x):
    return 0.5 * x * (1.0 + lax.erf(x * (2.0 ** -0.5)))


def _adaln_kernel(cond_ref, w_ref, b_ref, o_ref):
    act = jax.nn.silu(cond_ref[...])
    o_ref[...] = jnp.dot(act, w_ref[...], precision=lax.Precision.HIGHEST,
                         preferred_element_type=F32) + b_ref[...]


def _adaln(cond, w_mod, b_mod):
    rows, d = cond.shape
    n = w_mod.shape[1]
    tn = d
    return pl.pallas_call(
        _adaln_kernel,
        out_shape=jax.ShapeDtypeStruct((rows, n), F32),
        grid=(n // tn,),
        in_specs=[pl.BlockSpec((rows, d), lambda j: (0, 0)),
                  pl.BlockSpec((d, tn), lambda j: (0, j)),
                  pl.BlockSpec((1, tn), lambda j: (0, j))],
        out_specs=pl.BlockSpec((rows, tn), lambda j: (0, j)),
        compiler_params=_cparams("parallel"),
        name="adaln",
    )(cond, w_mod, b_mod.reshape(1, n))


def _ctx_kv_kernel(xc_ref, mod_ref, n1_ref, w_ref, o_ref):
    d = xc_ref.shape[-1]
    mod = mod_ref[...]
    hc = _rmsnorm(xc_ref[...], n1_ref[...]) * (1.0 + mod[:, d:2 * d]) + mod[:, 0:d]
    o_ref[...] = jnp.dot(hc.astype(BF16), w_ref[...], preferred_element_type=F32).astype(BF16)


def _ctx_kv(ctx, mod3, ctx_row, n1, w_kv):
    b, c, d = ctx.shape
    kvw = w_kv.shape[1]
    return pl.pallas_call(
        _ctx_kv_kernel,
        out_shape=jax.ShapeDtypeStruct((b, c, kvw), BF16),
        grid=(b,),
        in_specs=[pl.BlockSpec((None, c, d), lambda i: (i, 0, 0)),
                  pl.BlockSpec((None, 1, N_MOD * d), lambda i: (ctx_row, 0, 0)),
                  pl.BlockSpec((1, d), lambda i: (0, 0)),
                  pl.BlockSpec((d, kvw), lambda i: (0, 0))],
        out_specs=pl.BlockSpec((None, c, kvw), lambda i: (i, 0, 0)),
        compiler_params=_cparams("parallel"),
        name="ctx_kv",
    )(ctx, mod3, n1, w_kv)


def _proj_kernel(x_ref, mod_ref, n1_ref, w_ref, cos_ref, sin_ref,
                 q_ref, kv_ref, gb_ref, cu_ref, ga_ref, gv_ref):
    d = x_ref.shape[-1]
    aw = q_ref.shape[-1]
    kw = kv_ref.shape[-1] // 2
    cw = gb_ref.shape[-1]
    mod = mod_ref[...]
    h = (_rmsnorm(x_ref[...], n1_ref[...]) * (1.0 + mod[:, d:2 * d]) + mod[:, 0:d]).astype(BF16)
    cos = cos_ref[...]
    sin = sin_ref[...]
    lane = lax.broadcasted_iota(jnp.int32, cos.shape, 1)
    first_half = (lane % (2 * ROPE_PAIRS)) < ROPE_PAIRS

    def rope(z):
        partner = jnp.where(first_half, pltpu.roll(z, LANES - ROPE_PAIRS, 1),
                            pltpu.roll(z, ROPE_PAIRS, 1))
        return z * cos + partner * sin

    def proj(lo, width):
        return jnp.dot(h, w_ref[:, lo:lo + width], preferred_element_type=F32)

    off = 0
    zq = proj(off, aw)
    for g in range(aw // LANES):
        q_ref[:, g * LANES:(g + 1) * LANES] = rope(zq[:, g * LANES:(g + 1) * LANES]).astype(BF16)
    off += aw
    zkv = proj(off, 2 * kw)
    for g in range(kw // LANES):
        kv_ref[:, g * LANES:(g + 1) * LANES] = rope(zkv[:, g * LANES:(g + 1) * LANES]).astype(BF16)
    kv_ref[:, kw:] = zkv[:, kw:].astype(BF16)
    off += 2 * kw
    gb_ref[...] = proj(off, cw)
    off += cw
    zc = proj(off, cw)
    off += cw
    cu_ref[...] = zc * proj(off, cw)
    off += cw
    ga_ref[...] = proj(off, d)
    off += d
    gv_ref[...] = proj(off, d)


def _proj(x2, mod3, n1, w_in, cos, sin, seq, tm, b0, nb):
    d = x2.shape[1]
    t = nb * seq
    aw = N_Q_HEADS * HEAD_DIM
    kw = N_KV_HEADS * HEAD_DIM
    cw = d // 2
    per_seq = seq // tm
    row = lambda i: (i, 0)
    return pl.pallas_call(
        _proj_kernel,
        out_shape=(jax.ShapeDtypeStruct((t, aw), BF16),
                   jax.ShapeDtypeStruct((t, 2 * kw), BF16),
                   jax.ShapeDtypeStruct((t, cw), F32),
                   jax.ShapeDtypeStruct((t, cw), F32),
                   jax.ShapeDtypeStruct((t, d), F32),
                   jax.ShapeDtypeStruct((t, d), F32)),
        grid=(t // tm,),
        in_specs=[pl.BlockSpec((tm, d), lambda i: (b0 * per_seq + i, 0)),
                  pl.BlockSpec((None, 1, N_MOD * d), lambda i: (b0 + i // per_seq, 0, 0)),
                  pl.BlockSpec((1, d), lambda i: (0, 0)),
                  pl.BlockSpec(w_in.shape, lambda i: (0, 0)),
                  pl.BlockSpec((tm, LANES), lambda i: (i % per_seq, 0)),
                  pl.BlockSpec((tm, LANES), lambda i: (i % per_seq, 0))],
        out_specs=(pl.BlockSpec((tm, aw), row), pl.BlockSpec((tm, 2 * kw), row),
                   pl.BlockSpec((tm, cw), row), pl.BlockSpec((tm, cw), row),
                   pl.BlockSpec((tm, d), row), pl.BlockSpec((tm, d), row)),
        compiler_params=_cparams("parallel"),
        name="proj",
    )(x2, mod3, n1, w_in, cos, sin)


def _mixer_kernel(sink_ref, x_ref, q_ref, kv_ref, kvp_ref, kvn_ref, kvc_ref,
                  cu_ref, cup_ref, cun_ref, gb_ref, ga_ref, gv_ref, convw_ref,
                  wa_ref, wc_ref, wm_ref, mod_ref, o_ref, attn_scr):
    n = pl.program_id(1)
    has_prev = n > 0
    has_next = n < pl.num_programs(1) - 1
    tq, d = x_ref.shape
    kw = N_KV_HEADS * HEAD_DIM
    scale = HEAD_DIM ** -0.5
    nt = (((1,), (1,)), ((), ()))

    kext = jnp.concatenate([kvp_ref[...], kv_ref[...], kvn_ref[...]], axis=0)
    kctx = kvc_ref[...]
    qi = lax.broadcasted_iota(jnp.int32, (WINDOW, WINDOW), 0)
    ki = lax.broadcasted_iota(jnp.int32, (WINDOW, WINDOW), 1)
    nblk = tq // WINDOW
    for j in range(nblk):
        prev_ok = ki >= qi
        next_ok = ki <= qi
        if j == 0:
            prev_ok = jnp.logical_and(prev_ok, has_prev)
        if j == nblk - 1:
            next_ok = jnp.logical_and(next_ok, has_next)
        mask = jnp.concatenate([prev_ok, jnp.ones_like(prev_ok), next_ok], axis=1)
        rows = slice(j * WINDOW, (j + 1) * WINDOW)
        krows = slice(j * WINDOW, (j + 3) * WINDOW)
        for hq in range(N_Q_HEADS):
            kvh = hq // Q_PER_KV
            kcol = slice(kvh * HEAD_DIM, (kvh + 1) * HEAD_DIM)
            vcol = slice(kw + kvh * HEAD_DIM, kw + (kvh + 1) * HEAD_DIM)
            qh = q_ref[rows, hq * HEAD_DIM:(hq + 1) * HEAD_DIM]
            s_loc = lax.dot_general(qh, kext[krows, kcol], nt, preferred_element_type=F32) * scale
            s_loc = jnp.where(mask, s_loc, NEG_INF)
            s_ctx = lax.dot_general(qh, kctx[:, kcol], nt, preferred_element_type=F32) * scale
            sink = sink_ref[0, hq]
            m = jnp.maximum(jnp.max(s_loc, axis=-1, keepdims=True),
                            jnp.max(s_ctx, axis=-1, keepdims=True))
            m = jnp.maximum(m, sink)
            p_loc = jnp.exp(s_loc - m)
            p_ctx = jnp.exp(s_ctx - m)
            den = (jnp.sum(p_loc, axis=-1, keepdims=True) + jnp.sum(p_ctx, axis=-1, keepdims=True)
                   + jnp.exp(sink - m))
            o = (jnp.dot(p_loc.astype(BF16), kext[krows, vcol], preferred_element_type=F32)
                 + jnp.dot(p_ctx.astype(BF16), kctx[:, vcol], preferred_element_type=F32))
            attn_scr[rows, hq * HEAD_DIM:(hq + 1) * HEAD_DIM] = o / den

    y_attn = jnp.dot(attn_scr[...].astype(BF16), wa_ref[...], preferred_element_type=F32)

    cu = cu_ref[...]
    ri = lax.broadcasted_iota(jnp.int32, cu.shape, 0)
    prev_row = jnp.where(has_prev, cup_ref[7:8, :], 0.0)
    next_row = jnp.where(has_next, cun_ref[0:1, :], 0.0)
    cu_m1 = jnp.where(ri == 0, prev_row, pltpu.roll(cu, 1, 0))
    cu_p1 = jnp.where(ri == tq - 1, next_row, pltpu.roll(cu, tq - 1, 0))
    cw = convw_ref[...]
    conv = cu_m1 * cw[0:1, :] + cu * cw[1:2, :] + cu_p1 * cw[2:3, :]
    y_conv = jnp.dot((gb_ref[...] * conv).astype(BF16), wc_ref[...], preferred_element_type=F32)

    merged = jax.nn.sigmoid(ga_ref[...]) * y_attn + jax.nn.sigmoid(gv_ref[...]) * y_conv
    y = jnp.dot(merged.astype(BF16), wm_ref[...], preferred_element_type=F32)
    g1 = mod_ref[:, 2 * d:3 * d]
    o_ref[...] = x_ref[...] + g1 * y


def _mixer(x2, q, kv, kvc, cu, gb, ga, gv, sink, conv_w, wa, wc, wm, mod3, seq, tq, b0, batch):
    d = x2.shape[1]
    t, aw = q.shape
    kv2 = kv.shape[1]
    cw = cu.shape[1]
    c = kvc.shape[1]
    nq = seq // tq
    nb = seq // WINDOW
    sub = tq // WINDOW
    kv3 = kv.reshape(t // WINDOW, WINDOW, kv2)
    cu3 = cu.reshape(t // 8, 8, cw)
    tile = lambda b, n: (b * nq + n, 0)
    const = lambda b, n: (0, 0)
    return pl.pallas_call(
        _mixer_kernel,
        out_shape=jax.ShapeDtypeStruct((t, d), F32),
        grid=(batch, nq),
        in_specs=[
            pl.BlockSpec(memory_space=pltpu.SMEM),
            pl.BlockSpec((tq, d), lambda b, n: ((b0 + b) * nq + n, 0)),
            pl.BlockSpec((tq, aw), tile),
            pl.BlockSpec((tq, kv2), tile),
            pl.BlockSpec((None, WINDOW, kv2), lambda b, n: (b * nb + jnp.maximum(n * sub - 1, 0), 0, 0)),
            pl.BlockSpec((None, WINDOW, kv2), lambda b, n: (b * nb + jnp.minimum((n + 1) * sub, nb - 1), 0, 0)),
            pl.BlockSpec((None, c, kv2), lambda b, n: (b0 + b, 0, 0)),
            pl.BlockSpec((tq, cw), tile),
            pl.BlockSpec((None, 8, cw), lambda b, n: (jnp.maximum((b * seq + n * tq) // 8 - 1, 0), 0, 0)),
            pl.BlockSpec((None, 8, cw), lambda b, n: (jnp.minimum((b * seq + (n + 1) * tq) // 8, t // 8 - 1), 0, 0)),
            pl.BlockSpec((tq, cw), tile),
            pl.BlockSpec((tq, d), tile),
            pl.BlockSpec((tq, d), tile),
            pl.BlockSpec(conv_w.shape, const),
            pl.BlockSpec(wa.shape, const),
            pl.BlockSpec(wc.shape, const),
            pl.BlockSpec(wm.shape, const),
            pl.BlockSpec((None, 1, N_MOD * d), lambda b, n: (b0 + b, 0, 0)),
        ],
        out_specs=pl.BlockSpec((tq, d), tile),
        scratch_shapes=[pltpu.VMEM((tq, aw), F32)],
        compiler_params=_cparams("parallel", "parallel"),
        name="mixer",
    )(sink, x2, q, kv, kv3, kv3, kvc, cu, cu3, cu3, gb, ga, gv, conv_w, wa, wc, wm, mod3)


def _topk_rows(s, k, payload=None):
    n = s.shape[0]
    rows = lax.broadcasted_iota(jnp.int32, s.shape, 0).astype(F32)
    vals, picks = [], []
    for _ in range(k):
        m = jnp.max(s, axis=0, keepdims=True)
        am = jnp.min(jnp.where(s == m, rows, float(n)), axis=0, keepdims=True)
        hit = rows == am
        vals.append(m)
        if payload is None:
            picks.append(am)
        else:
            picks.append(jnp.max(jnp.where(hit, payload, -1.0), axis=0, keepdims=True))
        s = jnp.where(hit, -jnp.inf, s)
    return jnp.concatenate(vals, axis=0), jnp.concatenate(picks, axis=0)


def _route_kernel(x1_ref, mod_ref, n2_ref, wq_ref, keys_ref, h2_ref, idx_ref, gate_ref, h2b_scr):
    d = x1_ref.shape[-1]
    nt = (((1,), (1,)), ((), ()))

    @pl.when(pl.program_id(1) == 0)
    def _():
        mod = mod_ref[...]
        h2 = _rmsnorm(x1_ref[...], n2_ref[...]) * (1.0 + mod[:, 4 * d:5 * d]) + mod[:, 3 * d:4 * d]
        h2_ref[...] = h2
        h2b_scr[...] = h2.astype(BF16)

    qp = jnp.dot(h2b_scr[...], wq_ref[...], preferred_element_type=F32)
    half = qp.shape[1] // 2
    tops = []
    for p in range(2):
        qh = qp[:, p * half:(p + 1) * half].astype(BF16)
        st = lax.dot_general(keys_ref[p], qh, nt, preferred_element_type=F32)
        tops.append(_topk_rows(st, PEER_TOPK))
    (a, ia), (b, ib) = tops
    width = [PEER_TOPK // (i + 1) for i in range(PEER_TOPK)]
    pad = -sum(width) % 8
    tm = a.shape[1]
    cand = jnp.concatenate([a[i:i + 1, :] + b[:width[i], :] for i in range(PEER_TOPK)]
                           + [jnp.full((pad, tm), -jnp.inf, F32)], axis=0)
    cidx = jnp.concatenate([ia[i:i + 1, :] * PEER_N_KEYS + ib[:width[i], :] for i in range(PEER_TOPK)]
                           + [jnp.zeros((pad, tm), F32)], axis=0)
    best, idx = _topk_rows(cand, PEER_TOPK, payload=cidx)
    e = jnp.exp(best - best[0:1, :])
    gate_ref[...] = e / jnp.sum(e, axis=0, keepdims=True)
    idx_ref[...] = idx.astype(jnp.int32)


def _route(x1, mod3, n2, wq, keys, seq, tm, b0):
    t, d = x1.shape
    hw = wq.shape[1] // PEER_HEADS
    per_seq = seq // tm
    return pl.pallas_call(
        _route_kernel,
        out_shape=(jax.ShapeDtypeStruct((t, d), F32),
                   jax.ShapeDtypeStruct((PEER_HEADS, PEER_TOPK, t), jnp.int32),
                   jax.ShapeDtypeStruct((PEER_HEADS, PEER_TOPK, t), F32)),
        grid=(t // tm, PEER_HEADS),
        in_specs=[pl.BlockSpec((tm, d), lambda i, h: (i, 0)),
                  pl.BlockSpec((None, 1, N_MOD * d), lambda i, h: (b0 + i // per_seq, 0, 0)),
                  pl.BlockSpec((1, d), lambda i, h: (0, 0)),
                  pl.BlockSpec((d, hw), lambda i, h: (0, h)),
                  pl.BlockSpec((None, 2, PEER_N_KEYS, hw // 2), lambda i, h: (h, 0, 0, 0))],
        out_specs=(pl.BlockSpec((tm, d), lambda i, h: (i, 0)),
                   pl.BlockSpec((None, PEER_TOPK, tm), lambda i, h: (h, 0, i)),
                   pl.BlockSpec((None, PEER_TOPK, tm), lambda i, h: (h, 0, i))),
        scratch_shapes=[pltpu.VMEM((tm, d), BF16)],
        compiler_params=_cparams("parallel", "arbitrary"),
        name="route",
    )(x1, mod3, n2, wq, keys)


SC_CORES = 2
SC_SUBCORES = 16
SC_LANES = 16
SC_WORKERS = SC_CORES * SC_SUBCORES
SC_ROWS_U = 64
SC_ROWS_V = 64
SC_ROWS_PER_ITER = 8
SC_TOKENS = 16


def _sc_mesh():
    return plsc.VectorSubcoreMesh(core_axis_name="c", subcore_axis_name="s")


def _sc_params():
    return dataclasses.replace(pltpu.CompilerParams(), needs_layout_passes=False)


def _sc_worker_base(tokens_per_worker):
    return (lax.axis_index("s") * SC_CORES + lax.axis_index("c")) * tokens_per_worker


def _sc_chunk_pipeline(tab_hbm, idx_v, bufs, n_chunks, compute):
    rows = bufs[0][0].shape[0]

    def gather(g, b):
        buf, sem = bufs[b]
        return pltpu.make_async_copy(tab_hbm.at[idx_v.at[pl.ds(g * rows, rows)]], buf, sem)

    gather(0, 0).start()

    @pl.loop(0, n_chunks, step=2)
    def _(g):
        gather(g, 0).wait()
        gather(g + 1, 1).start()
        compute(g, bufs[0][0])
        gather(g + 1, 1).wait()

        @pl.when(g + 2 < n_chunks)
        def _():
            gather(g + 2, 0).start()

        compute(g + 1, bufs[1][0])


def _pack_table(tab):
    half = tab.shape[1] // 2
    lo = lax.bitcast_convert_type(tab[:, :half].astype(BF16), jnp.uint16).astype(jnp.int32)
    bits = lax.bitcast_convert_type(tab[:, half:], jnp.int32)
    sign = bits & jnp.int32(-2 ** 31)
    magnitude = bits & jnp.int32(2 ** 31 - 1)
    top = jnp.maximum((magnitude - lo + 0x8000) >> 16, 0)
    return sign | ((top << 16) + lo)


def _unpack_lo(x):
    return lax.bitcast_convert_type(x << 16, F32)


def _unpack_hi(x):
    return lax.bitcast_convert_type(x, F32)


def _peer_u_sc(u_pk, idx_flat, h2, t, k):
    d = h2.shape[1]
    words = u_pk.shape[1]
    tpw = t // SC_WORKERS
    rows = SC_ROWS_U
    cpt = k // rows
    cpb = SC_TOKENS * cpt
    nj = words // SC_LANES

    def body(u_hbm, idx_hbm, h_hbm, o_hbm, idx_v, h_v, buf0, buf1, acc_v, pre_v, sem0, sem1):
        base = _sc_worker_base(tpw)
        lanes = lax.iota(jnp.int32, SC_LANES)

        def compute(g, buf):
            tl = g // cpt
            c = g % cpt

            @plsc.parallel_loop(0, rows // SC_ROWS_PER_ITER)
            def _(it):
                r0 = it * SC_ROWS_PER_ITER
                accs = [[None, None] for _ in range(SC_ROWS_PER_ITER)]
                for j in range(nj):
                    h_lo = h_v[tl, pl.ds(j * SC_LANES, SC_LANES)]
                    h_hi = h_v[tl, pl.ds(words + j * SC_LANES, SC_LANES)]
                    for a in range(SC_ROWS_PER_ITER):
                        x = buf[r0 + a, pl.ds(j * SC_LANES, SC_LANES)]
                        term = _unpack_lo(x) * h_lo + _unpack_hi(x) * h_hi
                        accs[a][j % 2] = term if accs[a][j % 2] is None else accs[a][j % 2] + term
                for a in range(SC_ROWS_PER_ITER):
                    acc_v[r0 + a, :] = accs[a][0] + accs[a][1]

            for q in range(rows // SC_LANES):
                s = plsc.load_gather(acc_v, [lanes + q * SC_LANES, jnp.zeros((SC_LANES,), jnp.int32)])
                for l in range(1, SC_LANES):
                    s = s + plsc.load_gather(acc_v, [lanes + q * SC_LANES,
                                                     jnp.full((SC_LANES,), l, jnp.int32)])
                pre_v[tl, pl.ds(c * rows + q * SC_LANES, SC_LANES)] = s

        @pl.loop(0, tpw // SC_TOKENS)
        def _(blk):
            tok0 = base + blk * SC_TOKENS
            pltpu.sync_copy(idx_hbm.at[pl.ds(tok0 * k, SC_TOKENS * k)], idx_v)
            pltpu.sync_copy(h_hbm.at[pl.ds(tok0, SC_TOKENS)], h_v)
            _sc_chunk_pipeline(u_hbm, idx_v, ((buf0, sem0), (buf1, sem1)), cpb, compute)
            pltpu.sync_copy(pre_v, o_hbm.at[pl.ds(tok0, SC_TOKENS)])

    return pl.kernel(
        body,
        out_type=jax.ShapeDtypeStruct((t, k), F32),
        mesh=_sc_mesh(),
        scratch_types=[pltpu.VMEM((SC_TOKENS * k,), jnp.int32),
                       pltpu.VMEM((SC_TOKENS, d), F32),
                       pltpu.VMEM((rows, words), jnp.int32),
                       pltpu.VMEM((rows, words), jnp.int32),
                       pltpu.VMEM((rows, SC_LANES), F32),
                       pltpu.VMEM((SC_TOKENS, k), F32),
                       pltpu.SemaphoreType.DMA,
                       pltpu.SemaphoreType.DMA],
        compiler_params=_sc_params(),
        name="peer_u_sc",
    )(u_pk, idx_flat, h2)


def _peer_v_sc(v_pk, idx_flat, w_flat, t, k, after):
    words = v_pk.shape[1]
    d = 2 * words
    tpw = t // SC_WORKERS
    rows = SC_ROWS_V
    cpt = k // rows
    cpb = SC_TOKENS * cpt
    pw = words // 2
    nj = pw // SC_LANES

    def body(v_hbm, idx_hbm, w_hbm, after_hbm, o_hbm, idx_v, w_v, buf0, buf1, out_v, sem0, sem1):
        del after_hbm
        base = _sc_worker_base(tpw)

        def compute(g, buf):
            tl = g // cpt
            c = g % cpt
            for p in range(2):
                lo_cols = [pl.ds(p * pw + j * SC_LANES, SC_LANES) for j in range(nj)]
                hi_cols = [pl.ds(words + p * pw + j * SC_LANES, SC_LANES) for j in range(nj)]

                def row(r, accs):
                    wb = plsc.load_gather(w_v, [jnp.full((SC_LANES,), g * rows + r, jnp.int32)])
                    new = []
                    for j in range(nj):
                        x = buf[r, lo_cols[j]]
                        new.append(accs[2 * j] + _unpack_lo(x) * wb)
                        new.append(accs[2 * j + 1] + _unpack_hi(x) * wb)
                    return tuple(new)

                init = []
                for j in range(nj):
                    init.append(jnp.where(c == 0, 0.0, out_v[tl, lo_cols[j]]))
                    init.append(jnp.where(c == 0, 0.0, out_v[tl, hi_cols[j]]))
                accs = lax.fori_loop(0, rows, row, tuple(init))
                for j in range(nj):
                    out_v[tl, lo_cols[j]] = accs[2 * j]
                    out_v[tl, hi_cols[j]] = accs[2 * j + 1]

        @pl.loop(0, tpw // SC_TOKENS)
        def _(blk):
            tok0 = base + blk * SC_TOKENS
            pltpu.sync_copy(idx_hbm.at[pl.ds(tok0 * k, SC_TOKENS * k)], idx_v)
            pltpu.sync_copy(w_hbm.at[pl.ds(tok0 * k, SC_TOKENS * k)], w_v)
            _sc_chunk_pipeline(v_hbm, idx_v, ((buf0, sem0), (buf1, sem1)), cpb, compute)
            pltpu.sync_copy(out_v, o_hbm.at[pl.ds(tok0, SC_TOKENS)])

    return pl.kernel(
        body,
        out_type=jax.ShapeDtypeStruct((t, d), F32),
        mesh=_sc_mesh(),
        scratch_types=[pltpu.VMEM((SC_TOKENS * k,), jnp.int32),
                       pltpu.VMEM((SC_TOKENS * k,), F32),
                       pltpu.VMEM((rows, words), jnp.int32),
                       pltpu.VMEM((rows, words), jnp.int32),
                       pltpu.VMEM((SC_TOKENS, d), F32),
                       pltpu.SemaphoreType.DMA,
                       pltpu.SemaphoreType.DMA],
        compiler_params=_sc_params(),
        name="peer_v_sc",
    )(v_pk, idx_flat, w_flat, after)


def _act_kernel(gate_ref, pre_ref, o_ref):
    o_ref[...] = gate_ref[...] * _gelu_exact(pre_ref[...])


def _act(gate, pre, tile):
    t, k = pre.shape
    spec = pl.BlockSpec((tile, k), lambda i: (i, 0))
    return pl.pallas_call(
        _act_kernel,
        out_shape=jax.ShapeDtypeStruct((t, k), F32),
        grid=(t // tile,),
        in_specs=[spec, spec],
        out_specs=spec,
        compiler_params=_cparams("parallel"),
        name="expert_act",
    )(gate, pre)


def _final_kernel(x1_ref, y_ref, mod_ref, fg_ref, out_so_far_ref, o_ref):
    del out_so_far_ref
    d = x1_ref.shape[-1]
    x2 = x1_ref[...] + mod_ref[:, 5 * d:6 * d] * y_ref[...]
    o_ref[...] = _rmsnorm(x2, fg_ref[...])


def _final(x1, y, mod3, fg, out_so_far, seq, tile, b0):
    t, d = y.shape
    per_seq = seq // tile
    row = pl.BlockSpec((tile, d), lambda i: (i, 0))
    return pl.pallas_call(
        _final_kernel,
        out_shape=jax.ShapeDtypeStruct(out_so_far.shape, F32),
        grid=(t // tile,),
        in_specs=[row, row,
                  pl.BlockSpec((None, 1, N_MOD * d), lambda i: (b0 + i // per_seq, 0, 0)),
                  pl.BlockSpec((1, d), lambda i: (0, 0)),
                  pl.BlockSpec(memory_space=pl.ANY)],
        out_specs=pl.BlockSpec((tile, d), lambda i: (b0 * per_seq + i, 0)),
        input_output_aliases={4: 0},
        compiler_params=_cparams("parallel"),
        name="final",
    )(x1, y, mod3, fg, out_so_far)


def _peer_sc_u(idx, h2, u_pk, t):
    k = idx.shape[1]
    assert t % (SC_WORKERS * SC_TOKENS) == 0 and k % (2 * SC_ROWS_U) == 0 and k % (2 * SC_ROWS_V) == 0
    return _peer_u_sc(u_pk, idx.reshape(idx.shape[0] * k), h2, t, k)


def _peer_sc_v(idx, gate, pre, v_pk, tile, after):
    t, k = pre.shape
    w = _act(gate, pre, tile)
    return _peer_v_sc(v_pk, idx.reshape(idx.shape[0] * k), w.reshape(t * k), t, k, after)


TC_PEER_TOKENS = 8


def _peer_tc_kernel(idx_ref, idxn_ref, gate_ref, h2_ref, x1_ref, mod_ref, fg_ref, u_hbm, v_hbm,
                    out_so_far_ref, o_ref, ubuf, vbuf, sem):
    del out_so_far_ref
    i = pl.program_id(0)
    tt, k = gate_ref.shape
    d = x1_ref.shape[-1]
    words = d // 2
    slot = i % 2

    def row_copy(tab, buf, which, sl, src_row, dst_row):
        return pltpu.make_async_copy(tab.at[pl.ds(src_row, 1)], buf.at[sl, pl.ds(dst_row, 1)],
                                     sem.at[which, sl])

    def issue_token(iref, sl, t):
        for j in range(k):
            row = iref[t, j]
            row_copy(u_hbm, ubuf, 0, sl, row, t * k + j).start(priority=j % 2)
            row_copy(v_hbm, vbuf, 1, sl, row, t * k + j).start(priority=(j + 1) % 2)

    @pl.when(i == 0)
    def _():
        def tok(t, carry):
            issue_token(idx_ref, 0, t)
            return carry
        lax.fori_loop(0, tt, tok, 0)

    @pl.when(i + 1 < pl.num_programs(0))
    def _():
        for t in range(tt):
            issue_token(idxn_ref, 1 - slot, t)

    pltpu.make_async_copy(u_hbm.at[pl.ds(0, tt * k)], ubuf.at[slot], sem.at[0, slot]).wait()
    pltpu.make_async_copy(v_hbm.at[pl.ds(0, tt * k)], vbuf.at[slot], sem.at[1, slot]).wait()

    def unpack(x):
        return _unpack_lo(x), _unpack_hi(x)

    cols = []
    for t in range(tt):
        lo, hi = unpack(ubuf[slot, t * k:(t + 1) * k, :])
        cols.append(jnp.sum(lo * h2_ref[t:t + 1, :words] + hi * h2_ref[t:t + 1, words:],
                            axis=1, keepdims=True))
    pre = jnp.concatenate(cols, axis=1)
    w = gate_ref[...].T * _gelu_exact(pre)
    outs = []
    for t in range(tt):
        lo, hi = unpack(vbuf[slot, t * k:(t + 1) * k, :])
        wt = w[:, t:t + 1]
        outs.append(jnp.concatenate([jnp.sum(wt * lo, axis=0, keepdims=True),
                                     jnp.sum(wt * hi, axis=0, keepdims=True)], axis=1))
    y = jnp.concatenate(outs, axis=0)
    x2 = x1_ref[...] + mod_ref[:, 5 * d:6 * d] * y
    o_ref[...] = _rmsnorm(x2, fg_ref[...])


def _peer_tc(idx, gate, h2, x1, mod3, fg, u_pk, v_pk, out_so_far, seq, b0, t0):
    tg, k = idx.shape
    d = x1.shape[1]
    tt = TC_PEER_TOKENS
    n = (tg - t0) // tt
    first = t0 // tt
    per_seq = seq // tt
    row = lambda i: (first + i, 0)
    return pl.pallas_call(
        _peer_tc_kernel,
        out_shape=jax.ShapeDtypeStruct(out_so_far.shape, F32),
        grid=(n,),
        in_specs=[pl.BlockSpec((tt, k), row, memory_space=pltpu.SMEM),
                  pl.BlockSpec((tt, k), lambda i: (first + jnp.minimum(i + 1, n - 1), 0),
                               memory_space=pltpu.SMEM),
                  pl.BlockSpec((tt, k), row),
                  pl.BlockSpec((tt, d), row),
                  pl.BlockSpec((tt, d), row),
                  pl.BlockSpec((None, 1, N_MOD * d), lambda i: (b0 + (first + i) // per_seq, 0, 0)),
                  pl.BlockSpec((1, d), lambda i: (0, 0)),
                  pl.BlockSpec(memory_space=pl.ANY),
                  pl.BlockSpec(memory_space=pl.ANY),
                  pl.BlockSpec(memory_space=pl.ANY)],
        out_specs=pl.BlockSpec((tt, d), lambda i: (b0 * per_seq + first + i, 0)),
        scratch_shapes=[pltpu.VMEM((2, tt * k, d // 2), jnp.int32),
                        pltpu.VMEM((2, tt * k, d // 2), jnp.int32),
                        pltpu.SemaphoreType.DMA((2, 2))],
        input_output_aliases={9: 0},
        compiler_params=_cparams("arbitrary"),
        name="peer_tc",
    )(idx, idx, gate, h2, x1, mod3, fg, u_pk, v_pk, out_so_far)


def _rope_tables(length):
    rows = length // GRID_W
    row = jnp.repeat(jnp.arange(rows, dtype=F32), GRID_W)
    col = jnp.tile(jnp.arange(GRID_W, dtype=F32), rows)
    inv_freq = ROPE_BASE ** (-jnp.arange(ROPE_PAIRS, dtype=F32) / ROPE_PAIRS)
    ang_r = row[:, None] * inv_freq
    ang_c = col[:, None] * inv_freq
    cos = jnp.concatenate([jnp.cos(ang_r)] * 2 + [jnp.cos(ang_c)] * 2, axis=-1)
    sin = jnp.concatenate([-jnp.sin(ang_r), jnp.sin(ang_r), -jnp.sin(ang_c), jnp.sin(ang_c)], axis=-1)
    reps = LANES // HEAD_DIM
    return jnp.tile(cos, (1, reps)), jnp.tile(sin, (1, reps))


def _layer(x, c, ctx, c_ctx, w_mod, b_mod, n1, n2, w_in, sink, conv_w, w_attn_out, w_conv_out,
           w_mix_out, pw_q, p_keys, p_u, p_v, final_g, tm, tq, tr, tt, groups):
    batch, seq, d = x.shape
    t = batch * seq
    aw = N_Q_HEADS * HEAD_DIM
    kw = N_KV_HEADS * HEAD_DIM

    rows = -(-(batch + 1) // 8) * 8
    cond = jnp.zeros((rows, d), F32).at[:batch].set(c).at[batch].set(c_ctx)
    mod3 = _adaln(cond, w_mod, b_mod).reshape(rows, 1, N_MOD * d)

    w_in_b = w_in.astype(BF16)
    kvc = _ctx_kv(ctx, mod3, batch, n1.reshape(1, d), w_in_b[:, aw:aw + 2 * kw])

    cos, sin = _rope_tables(seq)
    x2 = x.reshape(t, d)
    wa, wc, wm = w_attn_out.astype(BF16), w_conv_out.astype(BF16), w_mix_out.astype(BF16)
    wq, keys = pw_q.astype(BF16), p_keys.astype(BF16)
    u_pk, v_pk = _pack_table(p_u), _pack_table(p_v)
    nsel = PEER_HEADS * PEER_TOPK

    fg = final_g.reshape(1, d)
    out = pl.empty((t, d), F32)
    b0 = 0
    pending = None
    finals = []

    def launch_v(p, after):
        idx, gate, pre, x1, pb0 = p
        finals.append((x1, _peer_sc_v(idx, gate, pre, v_pk, tt, after), pb0))

    for g, nb in enumerate(groups):
        tg = nb * seq
        t_sc = _sparsecore_share(tg, g)
        q, kv, gb, cu, ga, gv = _proj(x2, mod3, n1.reshape(1, d), w_in_b, cos, sin, seq, tm, b0, nb)
        x1 = _mixer(x2, q, kv, kvc, cu, gb, ga, gv, sink.reshape(1, N_Q_HEADS), conv_w,
                    wa, wc, wm, mod3, seq, tq, b0, nb)
        h2, idx_t, gate_t = _route(x1, mod3, n2.reshape(1, d), wq, keys, seq, tr, b0)
        idx, gate = idx_t.reshape(nsel, tg).T, gate_t.reshape(nsel, tg).T
        pre = _peer_sc_u(idx, h2, u_pk, t_sc)
        if pending is not None:
            launch_v(pending, pre)
        pending = (idx, gate, pre, x1, b0)
        if t_sc < tg:
            out = _peer_tc(idx, gate, h2, x1, mod3, fg, u_pk, v_pk, out, seq, b0, t_sc)
        b0 += nb
    launch_v(pending, pending[2])
    for x1, y, pb0 in finals:
        out = _final(x1, y, mod3, fg, out, seq, tt, pb0)
    return out.reshape(batch, seq, d)


MAX_TOKEN_GROUPS = 8
SC_SHARE_NUMS, SC_SHARE_DEN = (14, 12, 14, 14), 16


def _sparsecore_share(tokens, group):
    unit = SC_WORKERS * SC_TOKENS
    t_sc = tokens * SC_SHARE_NUMS[group % len(SC_SHARE_NUMS)] // SC_SHARE_DEN // unit * unit
    if t_sc == 0 or (tokens - t_sc) % TC_PEER_TOKENS:
        return tokens
    return t_sc


def _token_groups(batch, seq):
    unit = SC_WORKERS * SC_TOKENS
    for groups in range(min(MAX_TOKEN_GROUPS, batch), 0, -1):
        nb = batch // groups
        if batch % groups == 0 and (nb * seq) % unit == 0:
            return [nb] * groups
    raise ValueError("token count must be a multiple of the SparseCore work split")


def kernel(x, c, ctx, c_ctx, w_mod, b_mod, norm1_g, norm2_g, w_in, attn_sink, conv_w, w_attn_out,
           w_conv_out, w_mix_out, peer_w_q, peer_sub_keys, peer_u, peer_v, final_g):
    assert w_mod.shape[0] == 1, "only the single-layer configuration is implemented"
    seq = x.shape[1]
    return _layer(x, c, ctx, c_ctx, w_mod[0], b_mod[0], norm1_g[0], norm2_g[0], w_in[0],
                  attn_sink[0], conv_w[0], w_attn_out[0], w_conv_out[0], w_mix_out[0],
                  peer_w_q[0], peer_sub_keys[0], peer_u[0], peer_v[0], final_g,
                  tm=min(512, seq), tq=min(256, seq), tr=min(256, seq), tt=min(256, seq),
                  groups=_token_groups(x.shape[0], seq))
```

```python
import dataclasses

import jax
import jax.numpy as jnp
from jax import lax
from jax.experimental import pallas as pl
from jax.experimental.pallas import tpu as pltpu
from jax.experimental.pallas import tpu_sc as plsc

HEAD_DIM = 64
N_Q_HEADS = 8
N_KV_HEADS = 2
Q_PER_KV = N_Q_HEADS // N_KV_HEADS
WINDOW = 128
GRID_W = 64
ROPE_BASE = 10000.0
ROPE_PAIRS = HEAD_DIM // 4
PEER_HEADS = 8
PEER_N_KEYS = 128
PEER_TOPK = 16
N_MOD = 6
EPS = 1e-6
NEG_INF = -1e30

LANES = 128
VMEM_LIMIT = 56 * 1024 * 1024

F32 = jnp.float32
BF16 = jnp.bfloat16


def _cparams(*sem):
    return pltpu.CompilerParams(dimension_semantics=sem, vmem_limit_bytes=VMEM_LIMIT)


def _rmsnorm(x, g):
    return x * lax.rsqrt(jnp.mean(x * x, axis=-1, keepdims=True) + EPS) * g


def _gelu_exact(x):
    return 0.5 * x * (1.0 + lax.erf(x * (2.0 ** -0.5)))


def _adaln_kernel(cond_ref, w_ref, b_ref, o_ref):
    act = jax.nn.silu(cond_ref[...])
    o_ref[...] = jnp.dot(act, w_ref[...], precision=lax.Precision.HIGHEST,
                         preferred_element_type=F32) + b_ref[...]


def _adaln(cond, w_mod, b_mod):
    rows, d = cond.shape
    n = w_mod.shape[1]
    tn = d
    return pl.pallas_call(
        _adaln_kernel,
        out_shape=jax.ShapeDtypeStruct((rows, n), F32),
        grid=(n // tn,),
        in_specs=[pl.BlockSpec((rows, d), lambda j: (0, 0)),
                  pl.BlockSpec((d, tn), lambda j: (0, j)),
                  pl.BlockSpec((1, tn), lambda j: (0, j))],
        out_specs=pl.BlockSpec((rows, tn), lambda j: (0, j)),
        compiler_params=_cparams("parallel"),
        name="adaln",
    )(cond, w_mod, b_mod.reshape(1, n))


def _ctx_kv_kernel(xc_ref, mod_ref, n1_ref, w_ref, o_ref):
    d = xc_ref.shape[-1]
    mod = mod_ref[...]
    hc = _rmsnorm(xc_ref[...], n1_ref[...]) * (1.0 + mod[:, d:2 * d]) + mod[:, 0:d]
    o_ref[...] = jnp.dot(hc.astype(BF16), w_ref[...], preferred_element_type=F32).astype(BF16)


def _ctx_kv(ctx, mod3, ctx_row, n1, w_kv):
    b, c, d = ctx.shape
    kvw = w_kv.shape[1]
    return pl.pallas_call(
        _ctx_kv_kernel,
        out_shape=jax.ShapeDtypeStruct((b, c, kvw), BF16),
        grid=(b,),
        in_specs=[pl.BlockSpec((None, c, d), lambda i: (i, 0, 0)),
                  pl.BlockSpec((None, 1, N_MOD * d), lambda i: (ctx_row, 0, 0)),
                  pl.BlockSpec((1, d), lambda i: (0, 0)),
                  pl.BlockSpec((d, kvw), lambda i: (0, 0))],
        out_specs=pl.BlockSpec((None, c, kvw), lambda i: (i, 0, 0)),
        compiler_params=_cparams("parallel"),
        name="ctx_kv",
    )(ctx, mod3, n1, w_kv)


def _proj_kernel(x_ref, mod_ref, n1_ref, w_ref, cos_ref, sin_ref,
                 q_ref, kv_ref, gb_ref, cu_ref, ga_ref, gv_ref):
    d = x_ref.shape[-1]
    aw = q_ref.shape[-1]
    kw = kv_ref.shape[-1] // 2
    cw = gb_ref.shape[-1]
    mod = mod_ref[...]
    h = (_rmsnorm(x_ref[...], n1_ref[...]) * (1.0 + mod[:, d:2 * d]) + mod[:, 0:d]).astype(BF16)
    cos = cos_ref[...]
    sin = sin_ref[...]
    lane = lax.broadcasted_iota(jnp.int32, cos.shape, 1)
    first_half = (lane % (2 * ROPE_PAIRS)) < ROPE_PAIRS

    def rope(z):
        partner = jnp.where(first_half, pltpu.roll(z, LANES - ROPE_PAIRS, 1),
                            pltpu.roll(z, ROPE_PAIRS, 1))
        return z * cos + partner * sin

    def proj(lo, width):
        return jnp.dot(h, w_ref[:, lo:lo + width], preferred_element_type=F32)

    off = 0
    zq = proj(off, aw)
    for g in range(aw // LANES):
        q_ref[:, g * LANES:(g + 1) * LANES] = rope(zq[:, g * LANES:(g + 1) * LANES]).astype(BF16)
    off += aw
    zkv = proj(off, 2 * kw)
    for g in range(kw // LANES):
        kv_ref[:, g * LANES:(g + 1) * LANES] = rope(zkv[:, g * LANES:(g + 1) * LANES]).astype(BF16)
    kv_ref[:, kw:] = zkv[:, kw:].astype(BF16)
    off += 2 * kw
    gb_ref[...] = proj(off, cw)
    off += cw
    zc = proj(off, cw)
    off += cw
    cu_ref[...] = zc * proj(off, cw)
    off += cw
    ga_ref[...] = proj(off, d)
    off += d
    gv_ref[...] = proj(off, d)


def _proj(x2, mod3, n1, w_in, cos, sin, seq, tm, b0, nb):
    d = x2.shape[1]
    t = nb * seq
    aw = N_Q_HEADS * HEAD_DIM
    kw = N_KV_HEADS * HEAD_DIM
    cw = d // 2
    per_seq = seq // tm
    row = lambda i: (i, 0)
    return pl.pallas_call(
        _proj_kernel,
        out_shape=(jax.ShapeDtypeStruct((t, aw), BF16),
                   jax.ShapeDtypeStruct((t, 2 * kw), BF16),
                   jax.ShapeDtypeStruct((t, cw), F32),
                   jax.ShapeDtypeStruct((t, cw), F32),
                   jax.ShapeDtypeStruct((t, d), F32),
                   jax.ShapeDtypeStruct((t, d), F32)),
        grid=(t // tm,),
        in_specs=[pl.BlockSpec((tm, d), lambda i: (b0 * per_seq + i, 0)),
                  pl.BlockSpec((None, 1, N_MOD * d), lambda i: (b0 + i // per_seq, 0, 0)),
                  pl.BlockSpec((1, d), lambda i: (0, 0)),
                  pl.BlockSpec(w_in.shape, lambda i: (0, 0)),
                  pl.BlockSpec((tm, LANES), lambda i: (i % per_seq, 0)),
                  pl.BlockSpec((tm, LANES), lambda i: (i % per_seq, 0))],
        out_specs=(pl.BlockSpec((tm, aw), row), pl.BlockSpec((tm, 2 * kw), row),
                   pl.BlockSpec((tm, cw), row), pl.BlockSpec((tm, cw), row),
                   pl.BlockSpec((tm, d), row), pl.BlockSpec((tm, d), row)),
        compiler_params=_cparams("parallel"),
        name="proj",
    )(x2, mod3, n1, w_in, cos, sin)


def _mixer_kernel(sink_ref, x_ref, q_ref, kv_ref, kvp_ref, kvn_ref, kvc_ref,
                  cu_ref, cup_ref, cun_ref, gb_ref, ga_ref, gv_ref, convw_ref,
                  wa_ref, wc_ref, wm_ref, mod_ref, o_ref, attn_scr):
    n = pl.program_id(1)
    has_prev = n > 0
    has_next = n < pl.num_programs(1) - 1
    tq, d = x_ref.shape
    kw = N_KV_HEADS * HEAD_DIM
    scale = HEAD_DIM ** -0.5
    nt = (((1,), (1,)), ((), ()))

    kext = jnp.concatenate([kvp_ref[...], kv_ref[...], kvn_ref[...]], axis=0)
    kctx = kvc_ref[...]
    qi = lax.broadcasted_iota(jnp.int32, (WINDOW, WINDOW), 0)
    ki = lax.broadcasted_iota(jnp.int32, (WINDOW, WINDOW), 1)
    nblk = tq // WINDOW
    for j in range(nblk):
        prev_ok = ki >= qi
        next_ok = ki <= qi
        if j == 0:
            prev_ok = jnp.logical_and(prev_ok, has_prev)
        if j == nblk - 1:
            next_ok = jnp.logical_and(next_ok, has_next)
        mask = jnp.concatenate([prev_ok, jnp.ones_like(prev_ok), next_ok], axis=1)
        rows = slice(j * WINDOW, (j + 1) * WINDOW)
        krows = slice(j * WINDOW, (j + 3) * WINDOW)
        for hq in range(N_Q_HEADS):
            kvh = hq // Q_PER_KV
            kcol = slice(kvh * HEAD_DIM, (kvh + 1) * HEAD_DIM)
            vcol = slice(kw + kvh * HEAD_DIM, kw + (kvh + 1) * HEAD_DIM)
            qh = q_ref[rows, hq * HEAD_DIM:(hq + 1) * HEAD_DIM]
            s_loc = lax.dot_general(qh, kext[krows, kcol], nt, preferred_element_type=F32) * scale
            s_loc = jnp.where(mask, s_loc, NEG_INF)
            s_ctx = lax.dot_general(qh, kctx[:, kcol], nt, preferred_element_type=F32) * scale
            sink = sink_ref[0, hq]
            m = jnp.maximum(jnp.max(s_loc, axis=-1, keepdims=True),
                            jnp.max(s_ctx, axis=-1, keepdims=True))
            m = jnp.maximum(m, sink)
            p_loc = jnp.exp(s_loc - m)
            p_ctx = jnp.exp(s_ctx - m)
            den = (jnp.sum(p_loc, axis=-1, keepdims=True) + jnp.sum(p_ctx, axis=-1, keepdims=True)
                   + jnp.exp(sink - m))
            o = (jnp.dot(p_loc.astype(BF16), kext[krows, vcol], preferred_element_type=F32)
                 + jnp.dot(p_ctx.astype(BF16), kctx[:, vcol], preferred_element_type=F32))
            attn_scr[rows, hq * HEAD_DIM:(hq + 1) * HEAD_DIM] = o / den

    y_attn = jnp.dot(attn_scr[...].astype(BF16), wa_ref[...], preferred_element_type=F32)

    cu = cu_ref[...]
    ri = lax.broadcasted_iota(jnp.int32, cu.shape, 0)
    prev_row = jnp.where(has_prev, cup_ref[7:8, :], 0.0)
    next_row = jnp.where(has_next, cun_ref[0:1, :], 0.0)
    cu_m1 = jnp.where(ri == 0, prev_row, pltpu.roll(cu, 1, 0))
    cu_p1 = jnp.where(ri == tq - 1, next_row, pltpu.roll(cu, tq - 1, 0))
    cw = convw_ref[...]
    conv = cu_m1 * cw[0:1, :] + cu * cw[1:2, :] + cu_p1 * cw[2:3, :]
    y_conv = jnp.dot((gb_ref[...] * conv).astype(BF16), wc_ref[...], preferred_element_type=F32)

    merged = jax.nn.sigmoid(ga_ref[...]) * y_attn + jax.nn.sigmoid(gv_ref[...]) * y_conv
    y = jnp.dot(merged.astype(BF16), wm_ref[...], preferred_element_type=F32)
    g1 = mod_ref[:, 2 * d:3 * d]
    o_ref[...] = x_ref[...] + g1 * y


def _mixer(x2, q, kv, kvc, cu, gb, ga, gv, sink, conv_w, wa, wc, wm, mod3, seq, tq, b0, batch):
    d = x2.shape[1]
    t, aw = q.shape
    kv2 = kv.shape[1]
    cw = cu.shape[1]
    c = kvc.shape[1]
    nq = seq // tq
    nb = seq // WINDOW
    sub = tq // WINDOW
    kv3 = kv.reshape(t // WINDOW, WINDOW, kv2)
    cu3 = cu.reshape(t // 8, 8, cw)
    tile = lambda b, n: (b * nq + n, 0)
    const = lambda b, n: (0, 0)
    return pl.pallas_call(
        _mixer_kernel,
        out_shape=jax.ShapeDtypeStruct((t, d), F32),
        grid=(batch, nq),
        in_specs=[
            pl.BlockSpec(memory_space=pltpu.SMEM),
            pl.BlockSpec((tq, d), lambda b, n: ((b0 + b) * nq + n, 0)),
            pl.BlockSpec((tq, aw), tile),
            pl.BlockSpec((tq, kv2), tile),
            pl.BlockSpec((None, WINDOW, kv2), lambda b, n: (b * nb + jnp.maximum(n * sub - 1, 0), 0, 0)),
            pl.BlockSpec((None, WINDOW, kv2), lambda b, n: (b * nb + jnp.minimum((n + 1) * sub, nb - 1), 0, 0)),
            pl.BlockSpec((None, c, kv2), lambda b, n: (b0 + b, 0, 0)),
            pl.BlockSpec((tq, cw), tile),
            pl.BlockSpec((None, 8, cw), lambda b, n: (jnp.maximum((b * seq + n * tq) // 8 - 1, 0), 0, 0)),
            pl.BlockSpec((None, 8, cw), lambda b, n: (jnp.minimum((b * seq + (n + 1) * tq) // 8, t // 8 - 1), 0, 0)),
            pl.BlockSpec((tq, cw), tile),
            pl.BlockSpec((tq, d), tile),
            pl.BlockSpec((tq, d), tile),
            pl.BlockSpec(conv_w.shape, const),
            pl.BlockSpec(wa.shape, const),
            pl.BlockSpec(wc.shape, const),
            pl.BlockSpec(wm.shape, const),
            pl.BlockSpec((None, 1, N_MOD * d), lambda b, n: (b0 + b, 0, 0)),
        ],
        out_specs=pl.BlockSpec((tq, d), tile),
        scratch_shapes=[pltpu.VMEM((tq, aw), F32)],
        compiler_params=_cparams("parallel", "parallel"),
        name="mixer",
    )(sink, x2, q, kv, kv3, kv3, kvc, cu, cu3, cu3, gb, ga, gv, conv_w, wa, wc, wm, mod3)


def _topk_rows(s, k, payload=None):
    n = s.shape[0]
    rows = lax.broadcasted_iota(jnp.int32, s.shape, 0).astype(F32)
    vals, picks = [], []
    for _ in range(k):
        m = jnp.max(s, axis=0, keepdims=True)
        am = jnp.min(jnp.where(s == m, rows, float(n)), axis=0, keepdims=True)
        hit = rows == am
        vals.append(m)
        if payload is None:
            picks.append(am)
        else:
            picks.append(jnp.max(jnp.where(hit, payload, -1.0), axis=0, keepdims=True))
        s = jnp.where(hit, -jnp.inf, s)
    return jnp.concatenate(vals, axis=0), jnp.concatenate(picks, axis=0)


def _route_kernel(x1_ref, mod_ref, n2_ref, wq_ref, keys_ref, h2_ref, idx_ref, gate_ref, h2b_scr):
    d = x1_ref.shape[-1]
    nt = (((1,), (1,)), ((), ()))

    @pl.when(pl.program_id(1) == 0)
    def _():
        mod = mod_ref[...]
        h2 = _rmsnorm(x1_ref[...], n2_ref[...]) * (1.0 + mod[:, 4 * d:5 * d]) + mod[:, 3 * d:4 * d]
        h2_ref[...] = h2
        h2b_scr[...] = h2.astype(BF16)

    qp = jnp.dot(h2b_scr[...], wq_ref[...], preferred_element_type=F32)
    half = qp.shape[1] // 2
    tops = []
    for p in range(2):
        qh = qp[:, p * half:(p + 1) * half].astype(BF16)
        st = lax.dot_general(keys_ref[p], qh, nt, preferred_element_type=F32)
        tops.append(_topk_rows(st, PEER_TOPK))
    (a, ia), (b, ib) = tops
    width = [PEER_TOPK // (i + 1) for i in range(PEER_TOPK)]
    pad = -sum(width) % 8
    tm = a.shape[1]
    cand = jnp.concatenate([a[i:i + 1, :] + b[:width[i], :] for i in range(PEER_TOPK)]
                           + [jnp.full((pad, tm), -jnp.inf, F32)], axis=0)
    cidx = jnp.concatenate([ia[i:i + 1, :] * PEER_N_KEYS + ib[:width[i], :] for i in range(PEER_TOPK)]
                           + [jnp.zeros((pad, tm), F32)], axis=0)
    best, idx = _topk_rows(cand, PEER_TOPK, payload=cidx)
    e = jnp.exp(best - best[0:1, :])
    gate_ref[...] = e / jnp.sum(e, axis=0, keepdims=True)
    idx_ref[...] = idx.astype(jnp.int32)


def _route(x1, mod3, n2, wq, keys, seq, tm, b0):
    t, d = x1.shape
    hw = wq.shape[1] // PEER_HEADS
    per_seq = seq // tm
    return pl.pallas_call(
        _route_kernel,
        out_shape=(jax.ShapeDtypeStruct((t, d), F32),
                   jax.ShapeDtypeStruct((PEER_HEADS, PEER_TOPK, t), jnp.int32),
                   jax.ShapeDtypeStruct((PEER_HEADS, PEER_TOPK, t), F32)),
        grid=(t // tm, PEER_HEADS),
        in_specs=[pl.BlockSpec((tm, d), lambda i, h: (i, 0)),
                  pl.BlockSpec((None, 1, N_MOD * d), lambda i, h: (b0 + i // per_seq, 0, 0)),
                  pl.BlockSpec((1, d), lambda i, h: (0, 0)),
                  pl.BlockSpec((d, hw), lambda i, h: (0, h)),
                  pl.BlockSpec((None, 2, PEER_N_KEYS, hw // 2), lambda i, h: (h, 0, 0, 0))],
        out_specs=(pl.BlockSpec((tm, d), lambda i, h: (i, 0)),
                   pl.BlockSpec((None, PEER_TOPK, tm), lambda i, h: (h, 0, i)),
                   pl.BlockSpec((None, PEER_TOPK, tm), lambda i, h: (h, 0, i))),
        scratch_shapes=[pltpu.VMEM((tm, d), BF16)],
        compiler_params=_cparams("parallel", "arbitrary"),
        name="route",
    )(x1, mod3, n2, wq, keys)


SC_CORES = 2
SC_SUBCORES = 16
SC_LANES = 16
SC_WORKERS = SC_CORES * SC_SUBCORES
SC_ROWS_U = 64
SC_ROWS_V = 64
SC_ROWS_PER_ITER = 8
SC_TOKENS = 16


def _sc_mesh():
    return plsc.VectorSubcoreMesh(core_axis_name="c", subcore_axis_name="s")


def _sc_params():
    return dataclasses.replace(pltpu.CompilerParams(), needs_layout_passes=False)


def _sc_worker_base(tokens_per_worker):
    return (lax.axis_index("s") * SC_CORES + lax.axis_index("c")) * tokens_per_worker


def _sc_chunk_pipeline(tab_hbm, idx_v, bufs, n_chunks, compute):
    rows = bufs[0][0].shape[0]

    def gather(g, b):
        buf, sem = bufs[b]
        return pltpu.make_async_copy(tab_hbm.at[idx_v.at[pl.ds(g * rows, rows)]], buf, sem)

    gather(0, 0).start()

    @pl.loop(0, n_chunks, step=2)
    def _(g):
        gather(g, 0).wait()
        gather(g + 1, 1).start()
        compute(g, bufs[0][0])
        gather(g + 1, 1).wait()

        @pl.when(g + 2 < n_chunks)
        def _():
            gather(g + 2, 0).start()

        compute(g + 1, bufs[1][0])


def _pack_table(tab):
    half = tab.shape[1] // 2
    lo = lax.bitcast_convert_type(tab[:, :half].astype(BF16), jnp.uint16).astype(jnp.int32)
    bits = lax.bitcast_convert_type(tab[:, half:], jnp.int32)
    sign = bits & jnp.int32(-2 ** 31)
    magnitude = bits & jnp.int32(2 ** 31 - 1)
    top = jnp.maximum((magnitude - lo + 0x8000) >> 16, 0)
    return sign | ((top << 16) + lo)


def _unpack_lo(x):
    return lax.bitcast_convert_type(x << 16, F32)


def _unpack_hi(x):
    return lax.bitcast_convert_type(x, F32)


def _peer_u_sc(u_pk, idx_flat, h2, t, k):
    d = h2.shape[1]
    words = u_pk.shape[1]
    tpw = t // SC_WORKERS
    rows = SC_ROWS_U
    cpt = k // rows
    cpb = SC_TOKENS * cpt
    nj = words // SC_LANES

    def body(u_hbm, idx_hbm, h_hbm, o_hbm, idx_v, h_v, buf0, buf1, acc_v, pre_v, sem0, sem1):
        base = _sc_worker_base(tpw)
        lanes = lax.iota(jnp.int32, SC_LANES)

        def compute(g, buf):
            tl = g // cpt
            c = g % cpt

            @plsc.parallel_loop(0, rows // SC_ROWS_PER_ITER)
            def _(it):
                r0 = it * SC_ROWS_PER_ITER
                accs = [[None, None] for _ in range(SC_ROWS_PER_ITER)]
                for j in range(nj):
                    h_lo = h_v[tl, pl.ds(j * SC_LANES, SC_LANES)]
                    h_hi = h_v[tl, pl.ds(words + j * SC_LANES, SC_LANES)]
                    for a in range(SC_ROWS_PER_ITER):
                        x = buf[r0 + a, pl.ds(j * SC_LANES, SC_LANES)]
                        term = _unpack_lo(x) * h_lo + _unpack_hi(x) * h_hi
                        accs[a][j % 2] = term if accs[a][j % 2] is None else accs[a][j % 2] + term
                for a in range(SC_ROWS_PER_ITER):
                    acc_v[r0 + a, :] = accs[a][0] + accs[a][1]

            for q in range(rows // SC_LANES):
                s = plsc.load_gather(acc_v, [lanes + q * SC_LANES, jnp.zeros((SC_LANES,), jnp.int32)])
                for l in range(1, SC_LANES):
                    s = s + plsc.load_gather(acc_v, [lanes + q * SC_LANES,
                                                     jnp.full((SC_LANES,), l, jnp.int32)])
                pre_v[tl, pl.ds(c * rows + q * SC_LANES, SC_LANES)] = s

        @pl.loop(0, tpw // SC_TOKENS)
        def _(blk):
            tok0 = base + blk * SC_TOKENS
            pltpu.sync_copy(idx_hbm.at[pl.ds(tok0 * k, SC_TOKENS * k)], idx_v)
            pltpu.sync_copy(h_hbm.at[pl.ds(tok0, SC_TOKENS)], h_v)
            _sc_chunk_pipeline(u_hbm, idx_v, ((buf0, sem0), (buf1, sem1)), cpb, compute)
            pltpu.sync_copy(pre_v, o_hbm.at[pl.ds(tok0, SC_TOKENS)])

    return pl.kernel(
        body,
        out_type=jax.ShapeDtypeStruct((t, k), F32),
        mesh=_sc_mesh(),
        scratch_types=[pltpu.VMEM((SC_TOKENS * k,), jnp.int32),
                       pltpu.VMEM((SC_TOKENS, d), F32),
                       pltpu.VMEM((rows, words), jnp.int32),
                       pltpu.VMEM((rows, words), jnp.int32),
                       pltpu.VMEM((rows, SC_LANES), F32),
                       pltpu.VMEM((SC_TOKENS, k), F32),
                       pltpu.SemaphoreType.DMA,
                       pltpu.SemaphoreType.DMA],
        compiler_params=_sc_params(),
        name="peer_u_sc",
    )(u_pk, idx_flat, h2)


def _peer_v_sc(v_pk, idx_flat, w_flat, t, k, after):
    words = v_pk.shape[1]
    d = 2 * words
    tpw = t // SC_WORKERS
    rows = SC_ROWS_V
    cpt = k // rows
    cpb = SC_TOKENS * cpt
    pw = words // 2
    nj = pw // SC_LANES

    def body(v_hbm, idx_hbm, w_hbm, after_hbm, o_hbm, idx_v, w_v, buf0, buf1, out_v, sem0, sem1):
        del after_hbm
        base = _sc_worker_base(tpw)

        def compute(g, buf):
            tl = g // cpt
            c = g % cpt
            for p in range(2):
                lo_cols = [pl.ds(p * pw + j * SC_LANES, SC_LANES) for j in range(nj)]
                hi_cols = [pl.ds(words + p * pw + j * SC_LANES, SC_LANES) for j in range(nj)]

                def row(r, accs):
                    wb = plsc.load_gather(w_v, [jnp.full((SC_LANES,), g * rows + r, jnp.int32)])
                    new = []
                    for j in range(nj):
                        x = buf[r, lo_cols[j]]
                        new.append(accs[2 * j] + _unpack_lo(x) * wb)
                        new.append(accs[2 * j + 1] + _unpack_hi(x) * wb)
                    return tuple(new)

                init = []
                for j in range(nj):
                    init.append(jnp.where(c == 0, 0.0, out_v[tl, lo_cols[j]]))
                    init.append(jnp.where(c == 0, 0.0, out_v[tl, hi_cols[j]]))
                accs = lax.fori_loop(0, rows, row, tuple(init))
                for j in range(nj):
                    out_v[tl, lo_cols[j]] = accs[2 * j]
                    out_v[tl, hi_cols[j]] = accs[2 * j + 1]

        @pl.loop(0, tpw // SC_TOKENS)
        def _(blk):
            tok0 = base + blk * SC_TOKENS
            pltpu.sync_copy(idx_hbm.at[pl.ds(tok0 * k, SC_TOKENS * k)], idx_v)
            pltpu.sync_copy(w_hbm.at[pl.ds(tok0 * k, SC_TOKENS * k)], w_v)
            _sc_chunk_pipeline(v_hbm, idx_v, ((buf0, sem0), (buf1, sem1)), cpb, compute)
            pltpu.sync_copy(out_v, o_hbm.at[pl.ds(tok0, SC_TOKENS)])

    return pl.kernel(
        body,
        out_type=jax.ShapeDtypeStruct((t, d), F32),
        mesh=_sc_mesh(),
        scratch_types=[pltpu.VMEM((SC_TOKENS * k,), jnp.int32),
                       pltpu.VMEM((SC_TOKENS * k,), F32),
                       pltpu.VMEM((rows, words), jnp.int32),
                       pltpu.VMEM((rows, words), jnp.int32),
                       pltpu.VMEM((SC_TOKENS, d), F32),
                       pltpu.SemaphoreType.DMA,
                       pltpu.SemaphoreType.DMA],
        compiler_params=_sc_params(),
        name="peer_v_sc",
    )(v_pk, idx_flat, w_flat, after)


def _act_kernel(gate_ref, pre_ref, o_ref):
    o_ref[...] = gate_ref[...] * _gelu_exact(pre_ref[...])


def _act(gate, pre, tile):
    t, k = pre.shape
    spec = pl.BlockSpec((tile, k), lambda i: (i, 0))
    return pl.pallas_call(
        _act_kernel,
        out_shape=jax.ShapeDtypeStruct((t, k), F32),
        grid=(t // tile,),
        in_specs=[spec, spec],
        out_specs=spec,
        compiler_params=_cparams("parallel"),
        name="expert_act",
    )(gate, pre)


def _final_kernel(x1_ref, y_ref, mod_ref, fg_ref, out_so_far_ref, o_ref):
    del out_so_far_ref
    d = x1_ref.shape[-1]
    x2 = x1_ref[...] + mod_ref[:, 5 * d:6 * d] * y_ref[...]
    o_ref[...] = _rmsnorm(x2, fg_ref[...])


def _final(x1, y, mod3, fg, out_so_far, seq, tile, b0):
    t, d = y.shape
    per_seq = seq // tile
    row = pl.BlockSpec((tile, d), lambda i: (i, 0))
    return pl.pallas_call(
        _final_kernel,
        out_shape=jax.ShapeDtypeStruct(out_so_far.shape, F32),
        grid=(t // tile,),
        in_specs=[row, row,
                  pl.BlockSpec((None, 1, N_MOD * d), lambda i: (b0 + i // per_seq, 0, 0)),
                  pl.BlockSpec((1, d), lambda i: (0, 0)),
                  pl.BlockSpec(memory_space=pl.ANY)],
        out_specs=pl.BlockSpec((tile, d), lambda i: (b0 * per_seq + i, 0)),
        input_output_aliases={4: 0},
        compiler_params=_cparams("parallel"),
        name="final",
    )(x1, y, mod3, fg, out_so_far)


def _peer_sc_u(idx, h2, u_pk, t):
    k = idx.shape[1]
    assert t % (SC_WORKERS * SC_TOKENS) == 0 and k % (2 * SC_ROWS_U) == 0 and k % (2 * SC_ROWS_V) == 0
    return _peer_u_sc(u_pk, idx.reshape(idx.shape[0] * k), h2, t, k)


def _peer_sc_v(idx, gate, pre, v_pk, tile, after):
    t, k = pre.shape
    w = _act(gate, pre, tile)
    return _peer_v_sc(v_pk, idx.reshape(idx.shape[0] * k), w.reshape(t * k), t, k, after)


TC_PEER_TOKENS = 8


def _peer_tc_kernel(idx_ref, idxn_ref, gate_ref, h2_ref, x1_ref, mod_ref, fg_ref, u_hbm, v_hbm,
                    out_so_far_ref, o_ref, ubuf, vbuf, sem):
    del out_so_far_ref
    i = pl.program_id(0)
    tt, k = gate_ref.shape
    d = x1_ref.shape[-1]
    words = d // 2
    slot = i % 2

    def row_copy(tab, buf, which, sl, src_row, dst_row):
        return pltpu.make_async_copy(tab.at[pl.ds(src_row, 1)], buf.at[sl, pl.ds(dst_row, 1)],
                                     sem.at[which, sl])

    def issue_token(iref, sl, t):
        for j in range(k):
            row = iref[t, j]
            row_copy(u_hbm, ubuf, 0, sl, row, t * k + j).start(priority=j % 2)
            row_copy(v_hbm, vbuf, 1, sl, row, t * k + j).start(priority=(j + 1) % 2)

    @pl.when(i == 0)
    def _():
        def tok(t, carry):
            issue_token(idx_ref, 0, t)
            return carry
        lax.fori_loop(0, tt, tok, 0)

    @pl.when(i + 1 < pl.num_programs(0))
    def _():
        for t in range(tt):
            issue_token(idxn_ref, 1 - slot, t)

    pltpu.make_async_copy(u_hbm.at[pl.ds(0, tt * k)], ubuf.at[slot], sem.at[0, slot]).wait()
    pltpu.make_async_copy(v_hbm.at[pl.ds(0, tt * k)], vbuf.at[slot], sem.at[1, slot]).wait()

    def unpack(x):
        return _unpack_lo(x), _unpack_hi(x)

    cols = []
    for t in range(tt):
        lo, hi = unpack(ubuf[slot, t * k:(t + 1) * k, :])
        cols.append(jnp.sum(lo * h2_ref[t:t + 1, :words] + hi * h2_ref[t:t + 1, words:],
                            axis=1, keepdims=True))
    pre = jnp.concatenate(cols, axis=1)
    w = gate_ref[...].T * _gelu_exact(pre)
    outs = []
    for t in range(tt):
        lo, hi = unpack(vbuf[slot, t * k:(t + 1) * k, :])
        wt = w[:, t:t + 1]
        outs.append(jnp.concatenate([jnp.sum(wt * lo, axis=0, keepdims=True),
                                     jnp.sum(wt * hi, axis=0, keepdims=True)], axis=1))
    y = jnp.concatenate(outs, axis=0)
    x2 = x1_ref[...] + mod_ref[:, 5 * d:6 * d] * y
    o_ref[...] = _rmsnorm(x2, fg_ref[...])


def _peer_tc(idx, gate, h2, x1, mod3, fg, u_pk, v_pk, out_so_far, seq, b0, t0):
    tg, k = idx.shape
    d = x1.shape[1]
    tt = TC_PEER_TOKENS
    n = (tg - t0) // tt
    first = t0 // tt
    per_seq = seq // tt
    row = lambda i: (first + i, 0)
    return pl.pallas_call(
        _peer_tc_kernel,
        out_shape=jax.ShapeDtypeStruct(out_so_far.shape, F32),
        grid=(n,),
        in_specs=[pl.BlockSpec((tt, k), row, memory_space=pltpu.SMEM),
                  pl.BlockSpec((tt, k), lambda i: (first + jnp.minimum(i + 1, n - 1), 0),
                               memory_space=pltpu.SMEM),
                  pl.BlockSpec((tt, k), row),
                  pl.BlockSpec((tt, d), row),
                  pl.BlockSpec((tt, d), row),
                  pl.BlockSpec((None, 1, N_MOD * d), lambda i: (b0 + (first + i) // per_seq, 0, 0)),
                  pl.BlockSpec((1, d), lambda i: (0, 0)),
                  pl.BlockSpec(memory_space=pl.ANY),
                  pl.BlockSpec(memory_space=pl.ANY),
                  pl.BlockSpec(memory_space=pl.ANY)],
        out_specs=pl.BlockSpec((tt, d), lambda i: (b0 * per_seq + first + i, 0)),
        scratch_shapes=[pltpu.VMEM((2, tt * k, d // 2), jnp.int32),
                        pltpu.VMEM((2, tt * k, d // 2), jnp.int32),
                        pltpu.SemaphoreType.DMA((2, 2))],
        input_output_aliases={9: 0},
        compiler_params=_cparams("arbitrary"),
        name="peer_tc",
    )(idx, idx, gate, h2, x1, mod3, fg, u_pk, v_pk, out_so_far)


def _rope_tables(length):
    rows = length // GRID_W
    row = jnp.repeat(jnp.arange(rows, dtype=F32), GRID_W)
    col = jnp.tile(jnp.arange(GRID_W, dtype=F32), rows)
    inv_freq = ROPE_BASE ** (-jnp.arange(ROPE_PAIRS, dtype=F32) / ROPE_PAIRS)
    ang_r = row[:, None] * inv_freq
    ang_c = col[:, None] * inv_freq
    cos = jnp.concatenate([jnp.cos(ang_r)] * 2 + [jnp.cos(ang_c)] * 2, axis=-1)
    sin = jnp.concatenate([-jnp.sin(ang_r), jnp.sin(ang_r), -jnp.sin(ang_c), jnp.sin(ang_c)], axis=-1)
    reps = LANES // HEAD_DIM
    return jnp.tile(cos, (1, reps)), jnp.tile(sin, (1, reps))


def _layer(x, c, ctx, c_ctx, w_mod, b_mod, n1, n2, w_in, sink, conv_w, w_attn_out, w_conv_out,
           w_mix_out, pw_q, p_keys, p_u, p_v, final_g, tm, tq, tr, tt, groups):
    batch, seq, d = x.shape
    t = batch * seq
    aw = N_Q_HEADS * HEAD_DIM
    kw = N_KV_HEADS * HEAD_DIM

    rows = -(-(batch + 1) // 8) * 8
    cond = jnp.zeros((rows, d), F32).at[:batch].set(c).at[batch].set(c_ctx)
    mod3 = _adaln(cond, w_mod, b_mod).reshape(rows, 1, N_MOD * d)

    w_in_b = w_in.astype(BF16)
    kvc = _ctx_kv(ctx, mod3, batch, n1.reshape(1, d), w_in_b[:, aw:aw + 2 * kw])

    cos, sin = _rope_tables(seq)
    x2 = x.reshape(t, d)
    wa, wc, wm = w_attn_out.astype(BF16), w_conv_out.astype(BF16), w_mix_out.astype(BF16)
    wq, keys = pw_q.astype(BF16), p_keys.astype(BF16)
    u_pk, v_pk = _pack_table(p_u), _pack_table(p_v)
    nsel = PEER_HEADS * PEER_TOPK

    fg = final_g.reshape(1, d)
    out = pl.empty((t, d), F32)
    b0 = 0
    pending = None
    finals = []

    def launch_v(p, after):
        idx, gate, pre, x1, pb0 = p
        finals.append((x1, _peer_sc_v(idx, gate, pre, v_pk, tt, after), pb0))

    for g, nb in enumerate(groups):
        tg = nb * seq
        t_sc = _sparsecore_share(tg, g)
        q, kv, gb, cu, ga, gv = _proj(x2, mod3, n1.reshape(1, d), w_in_b, cos, sin, seq, tm, b0, nb)
        x1 = _mixer(x2, q, kv, kvc, cu, gb, ga, gv, sink.reshape(1, N_Q_HEADS), conv_w,
                    wa, wc, wm, mod3, seq, tq, b0, nb)
        h2, idx_t, gate_t = _route(x1, mod3, n2.reshape(1, d), wq, keys, seq, tr, b0)
        idx, gate = idx_t.reshape(nsel, tg).T, gate_t.reshape(nsel, tg).T
        pre = _peer_sc_u(idx, h2, u_pk, t_sc)
        if t_sc < tg:
            out = _peer_tc(idx, gate, h2, x1, mod3, fg, u_pk, v_pk, out, seq, b0, t_sc)
        launch_v((idx, gate, pre, x1, b0), pre)
        b0 += nb
    for x1, y, pb0 in finals:
        out = _final(x1, y, mod3, fg, out, seq, tt, pb0)
    return out.reshape(batch, seq, d)


MAX_TOKEN_GROUPS = 8
SC_SHARE_NUMS, SC_SHARE_DEN = (14, 12, 14, 14), 16


def _sparsecore_share(tokens, group):
    unit = SC_WORKERS * SC_TOKENS
    t_sc = tokens * SC_SHARE_NUMS[group % len(SC_SHARE_NUMS)] // SC_SHARE_DEN // unit * unit
    if t_sc == 0 or (tokens - t_sc) % TC_PEER_TOKENS:
        return tokens
    return t_sc


def _token_groups(batch, seq):
    unit = SC_WORKERS * SC_TOKENS
    for groups in range(min(MAX_TOKEN_GROUPS, batch), 0, -1):
        nb = batch // groups
        if batch % groups == 0 and (nb * seq) % unit == 0:
            return [nb] * groups
    raise ValueError("token count must be a multiple of the SparseCore work split")


def kernel(x, c, ctx, c_ctx, w_mod, b_mod, norm1_g, norm2_g, w_in, attn_sink, conv_w, w_attn_out,
           w_conv_out, w_mix_out, peer_w_q, peer_sub_keys, peer_u, peer_v, final_g):
    assert w_mod.shape[0] == 1, "only the single-layer configuration is implemented"
    seq = x.shape[1]
    return _layer(x, c, ctx, c_ctx, w_mod[0], b_mod[0], norm1_g[0], norm2_g[0], w_in[0],
                  attn_sink[0], conv_w[0], w_attn_out[0], w_conv_out[0], w_mix_out[0],
                  peer_w_q[0], peer_sub_keys[0], peer_u[0], peer_v[0], final_g,
                  tm=min(512, seq), tq=min(256, seq), tr=min(256, seq), tt=min(256, seq),
                  groups=_token_groups(x.shape[0], seq))
```

```python
import dataclasses

import jax
import jax.numpy as jnp
from jax import lax
from jax.experimental import pallas as pl
from jax.experimental.pallas import tpu as pltpu
from jax.experimental.pallas import tpu_sc as plsc

HEAD_DIM = 64
N_Q_HEADS = 8
N_KV_HEADS = 2
Q_PER_KV = N_Q_HEADS // N_KV_HEADS
WINDOW = 128
GRID_W = 64
ROPE_BASE = 10000.0
ROPE_PAIRS = HEAD_DIM // 4
PEER_HEADS = 8
PEER_N_KEYS = 128
PEER_TOPK = 16
N_MOD = 6
EPS = 1e-6
NEG_INF = -1e30

LANES = 128
VMEM_LIMIT = 56 * 1024 * 1024

F32 = jnp.float32
BF16 = jnp.bfloat16


def _cparams(*sem):
    return pltpu.CompilerParams(dimension_semantics=sem, vmem_limit_bytes=VMEM_LIMIT)


def _rmsnorm(x, g):
    return x * lax.rsqrt(jnp.mean(x * x, axis=-1, keepdims=True) + EPS) * g


def _gelu_exact(x):
    return 0.5 * x * (1.0 + lax.erf(x * (2.0 ** -0.5)))


def _adaln_kernel(cond_ref, w_ref, b_ref, o_ref):
    act = jax.nn.silu(cond_ref[...])
    o_ref[...] = jnp.dot(act, w_ref[...], precision=lax.Precision.HIGHEST,
                         preferred_element_type=F32) + b_ref[...]


def _adaln(cond, w_mod, b_mod):
    rows, d = cond.shape
    n = w_mod.shape[1]
    tn = d
    return pl.pallas_call(
        _adaln_kernel,
        out_shape=jax.ShapeDtypeStruct((rows, n), F32),
        grid=(n // tn,),
        in_specs=[pl.BlockSpec((rows, d), lambda j: (0, 0)),
                  pl.BlockSpec((d, tn), lambda j: (0, j)),
                  pl.BlockSpec((1, tn), lambda j: (0, j))],
        out_specs=pl.BlockSpec((rows, tn), lambda j: (0, j)),
        compiler_params=_cparams("parallel"),
        name="adaln",
    )(cond, w_mod, b_mod.reshape(1, n))


def _ctx_kv_kernel(xc_ref, mod_ref, n1_ref, w_ref, o_ref):
    d = xc_ref.shape[-1]
    mod = mod_ref[...]
    hc = _rmsnorm(xc_ref[...], n1_ref[...]) * (1.0 + mod[:, d:2 * d]) + mod[:, 0:d]
    o_ref[...] = jnp.dot(hc.astype(BF16), w_ref[...], preferred_element_type=F32).astype(BF16)


def _ctx_kv(ctx, mod3, ctx_row, n1, w_kv):
    b, c, d = ctx.shape
    kvw = w_kv.shape[1]
    return pl.pallas_call(
        _ctx_kv_kernel,
        out_shape=jax.ShapeDtypeStruct((b, c, kvw), BF16),
        grid=(b,),
        in_specs=[pl.BlockSpec((None, c, d), lambda i: (i, 0, 0)),
                  pl.BlockSpec((None, 1, N_MOD * d), lambda i: (ctx_row, 0, 0)),
                  pl.BlockSpec((1, d), lambda i: (0, 0)),
                  pl.BlockSpec((d, kvw), lambda i: (0, 0))],
        out_specs=pl.BlockSpec((None, c, kvw), lambda i: (i, 0, 0)),
        compiler_params=_cparams("parallel"),
        name="ctx_kv",
    )(ctx, mod3, n1, w_kv)


def _proj_kernel(x_ref, mod_ref, n1_ref, w_ref, cos_ref, sin_ref,
                 q_ref, kv_ref, gb_ref, cu_ref, ga_ref, gv_ref):
    d = x_ref.shape[-1]
    aw = q_ref.shape[-1]
    kw = kv_ref.shape[-1] // 2
    cw = gb_ref.shape[-1]
    mod = mod_ref[...]
    h = (_rmsnorm(x_ref[...], n1_ref[...]) * (1.0 + mod[:, d:2 * d]) + mod[:, 0:d]).astype(BF16)
    cos = cos_ref[...]
    sin = sin_ref[...]
    lane = lax.broadcasted_iota(jnp.int32, cos.shape, 1)
    first_half = (lane % (2 * ROPE_PAIRS)) < ROPE_PAIRS

    def rope(z):
        partner = jnp.where(first_half, pltpu.roll(z, LANES - ROPE_PAIRS, 1),
                            pltpu.roll(z, ROPE_PAIRS, 1))
        return z * cos + partner * sin

    def proj(lo, width):
        return jnp.dot(h, w_ref[:, lo:lo + width], preferred_element_type=F32)

    off = 0
    zq = proj(off, aw)
    for g in range(aw // LANES):
        q_ref[:, g * LANES:(g + 1) * LANES] = rope(zq[:, g * LANES:(g + 1) * LANES]).astype(BF16)
    off += aw
    zkv = proj(off, 2 * kw)
    for g in range(kw // LANES):
        kv_ref[:, g * LANES:(g + 1) * LANES] = rope(zkv[:, g * LANES:(g + 1) * LANES]).astype(BF16)
    kv_ref[:, kw:] = zkv[:, kw:].astype(BF16)
    off += 2 * kw
    gb_ref[...] = proj(off, cw)
    off += cw
    zc = proj(off, cw)
    off += cw
    cu_ref[...] = zc * proj(off, cw)
    off += cw
    ga_ref[...] = proj(off, d)
    off += d
    gv_ref[...] = proj(off, d)


def _proj(x2, mod3, n1, w_in, cos, sin, seq, tm, b0, nb):
    d = x2.shape[1]
    t = nb * seq
    aw = N_Q_HEADS * HEAD_DIM
    kw = N_KV_HEADS * HEAD_DIM
    cw = d // 2
    per_seq = seq // tm
    row = lambda i: (i, 0)
    return pl.pallas_call(
        _proj_kernel,
        out_shape=(jax.ShapeDtypeStruct((t, aw), BF16),
                   jax.ShapeDtypeStruct((t, 2 * kw), BF16),
                   jax.ShapeDtypeStruct((t, cw), F32),
                   jax.ShapeDtypeStruct((t, cw), F32),
                   jax.ShapeDtypeStruct((t, d), F32),
                   jax.ShapeDtypeStruct((t, d), F32)),
        grid=(t // tm,),
        in_specs=[pl.BlockSpec((tm, d), lambda i: (b0 * per_seq + i, 0)),
                  pl.BlockSpec((None, 1, N_MOD * d), lambda i: (b0 + i // per_seq, 0, 0)),
                  pl.BlockSpec((1, d), lambda i: (0, 0)),
                  pl.BlockSpec(w_in.shape, lambda i: (0, 0)),
                  pl.BlockSpec((tm, LANES), lambda i: (i % per_seq, 0)),
                  pl.BlockSpec((tm, LANES), lambda i: (i % per_seq, 0))],
        out_specs=(pl.BlockSpec((tm, aw), row), pl.BlockSpec((tm, 2 * kw), row),
                   pl.BlockSpec((tm, cw), row), pl.BlockSpec((tm, cw), row),
                   pl.BlockSpec((tm, d), row), pl.BlockSpec((tm, d), row)),
        compiler_params=_cparams("parallel"),
        name="proj",
    )(x2, mod3, n1, w_in, cos, sin)


def _mixer_kernel(sink_ref, x_ref, q_ref, kv_ref, kvp_ref, kvn_ref, kvc_ref,
                  cu_ref, cup_ref, cun_ref, gb_ref, ga_ref, gv_ref, convw_ref,
                  wa_ref, wc_ref, wm_ref, mod_ref, o_ref, attn_scr):
    n = pl.program_id(1)
    has_prev = n > 0
    has_next = n < pl.num_programs(1) - 1
    tq, d = x_ref.shape
    kw = N_KV_HEADS * HEAD_DIM
    scale = HEAD_DIM ** -0.5
    nt = (((1,), (1,)), ((), ()))

    kext = jnp.concatenate([kvp_ref[...], kv_ref[...], kvn_ref[...]], axis=0)
    kctx = kvc_ref[...]
    qi = lax.broadcasted_iota(jnp.int32, (WINDOW, WINDOW), 0)
    ki = lax.broadcasted_iota(jnp.int32, (WINDOW, WINDOW), 1)
    nblk = tq // WINDOW
    for j in range(nblk):
        prev_ok = ki >= qi
        next_ok = ki <= qi
        if j == 0:
            prev_ok = jnp.logical_and(prev_ok, has_prev)
        if j == nblk - 1:
            next_ok = jnp.logical_and(next_ok, has_next)
        mask = jnp.concatenate([prev_ok, jnp.ones_like(prev_ok), next_ok], axis=1)
        rows = slice(j * WINDOW, (j + 1) * WINDOW)
        krows = slice(j * WINDOW, (j + 3) * WINDOW)
        for hq in range(N_Q_HEADS):
            kvh = hq // Q_PER_KV
            kcol = slice(kvh * HEAD_DIM, (kvh + 1) * HEAD_DIM)
            vcol = slice(kw + kvh * HEAD_DIM, kw + (kvh + 1) * HEAD_DIM)
            qh = q_ref[rows, hq * HEAD_DIM:(hq + 1) * HEAD_DIM]
            s_loc = lax.dot_general(qh, kext[krows, kcol], nt, preferred_element_type=F32) * scale
            s_loc = jnp.where(mask, s_loc, NEG_INF)
            s_ctx = lax.dot_general(qh, kctx[:, kcol], nt, preferred_element_type=F32) * scale
            sink = sink_ref[0, hq]
            m = jnp.maximum(jnp.max(s_loc, axis=-1, keepdims=True),
                            jnp.max(s_ctx, axis=-1, keepdims=True))
            m = jnp.maximum(m, sink)
            p_loc = jnp.exp(s_loc - m)
            p_ctx = jnp.exp(s_ctx - m)
            den = (jnp.sum(p_loc, axis=-1, keepdims=True) + jnp.sum(p_ctx, axis=-1, keepdims=True)
                   + jnp.exp(sink - m))
            o = (jnp.dot(p_loc.astype(BF16), kext[krows, vcol], preferred_element_type=F32)
                 + jnp.dot(p_ctx.astype(BF16), kctx[:, vcol], preferred_element_type=F32))
            attn_scr[rows, hq * HEAD_DIM:(hq + 1) * HEAD_DIM] = o / den

    y_attn = jnp.dot(attn_scr[...].astype(BF16), wa_ref[...], preferred_element_type=F32)

    cu = cu_ref[...]
    ri = lax.broadcasted_iota(jnp.int32, cu.shape, 0)
    prev_row = jnp.where(has_prev, cup_ref[7:8, :], 0.0)
    next_row = jnp.where(has_next, cun_ref[0:1, :], 0.0)
    cu_m1 = jnp.where(ri == 0, prev_row, pltpu.roll(cu, 1, 0))
    cu_p1 = jnp.where(ri == tq - 1, next_row, pltpu.roll(cu, tq - 1, 0))
    cw = convw_ref[...]
    conv = cu_m1 * cw[0:1, :] + cu * cw[1:2, :] + cu_p1 * cw[2:3, :]
    y_conv = jnp.dot((gb_ref[...] * conv).astype(BF16), wc_ref[...], preferred_element_type=F32)

    merged = jax.nn.sigmoid(ga_ref[...]) * y_attn + jax.nn.sigmoid(gv_ref[...]) * y_conv
    y = jnp.dot(merged.astype(BF16), wm_ref[...], preferred_element_type=F32)
    g1 = mod_ref[:, 2 * d:3 * d]
    o_ref[...] = x_ref[...] + g1 * y


def _mixer(x2, q, kv, kvc, cu, gb, ga, gv, sink, conv_w, wa, wc, wm, mod3, seq, tq, b0, batch):
    d = x2.shape[1]
    t, aw = q.shape
    kv2 = kv.shape[1]
    cw = cu.shape[1]
    c = kvc.shape[1]
    nq = seq // tq
    nb = seq // WINDOW
    sub = tq // WINDOW
    kv3 = kv.reshape(t // WINDOW, WINDOW, kv2)
    cu3 = cu.reshape(t // 8, 8, cw)
    tile = lambda b, n: (b * nq + n, 0)
    const = lambda b, n: (0, 0)
    return pl.pallas_call(
        _mixer_kernel,
        out_shape=jax.ShapeDtypeStruct((t, d), F32),
        grid=(batch, nq),
        in_specs=[
            pl.BlockSpec(memory_space=pltpu.SMEM),
            pl.BlockSpec((tq, d), lambda b, n: ((b0 + b) * nq + n, 0)),
            pl.BlockSpec((tq, aw), tile),
            pl.BlockSpec((tq, kv2), tile),
            pl.BlockSpec((None, WINDOW, kv2), lambda b, n: (b * nb + jnp.maximum(n * sub - 1, 0), 0, 0)),
            pl.BlockSpec((None, WINDOW, kv2), lambda b, n: (b * nb + jnp.minimum((n + 1) * sub, nb - 1), 0, 0)),
            pl.BlockSpec((None, c, kv2), lambda b, n: (b0 + b, 0, 0)),
            pl.BlockSpec((tq, cw), tile),
            pl.BlockSpec((None, 8, cw), lambda b, n: (jnp.maximum((b * seq + n * tq) // 8 - 1, 0), 0, 0)),
            pl.BlockSpec((None, 8, cw), lambda b, n: (jnp.minimum((b * seq + (n + 1) * tq) // 8, t // 8 - 1), 0, 0)),
            pl.BlockSpec((tq, cw), tile),
            pl.BlockSpec((tq, d), tile),
            pl.BlockSpec((tq, d), tile),
            pl.BlockSpec(conv_w.shape, const),
            pl.BlockSpec(wa.shape, const),
            pl.BlockSpec(wc.shape, const),
            pl.BlockSpec(wm.shape, const),
            pl.BlockSpec((None, 1, N_MOD * d), lambda b, n: (b0 + b, 0, 0)),
        ],
        out_specs=pl.BlockSpec((tq, d), tile),
        scratch_shapes=[pltpu.VMEM((tq, aw), F32)],
        compiler_params=_cparams("parallel", "parallel"),
        name="mixer",
    )(sink, x2, q, kv, kv3, kv3, kvc, cu, cu3, cu3, gb, ga, gv, conv_w, wa, wc, wm, mod3)


def _topk_rows(s, k, payload=None):
    n = s.shape[0]
    rows = lax.broadcasted_iota(jnp.int32, s.shape, 0).astype(F32)
    vals, picks = [], []
    for _ in range(k):
        m = jnp.max(s, axis=0, keepdims=True)
        am = jnp.min(jnp.where(s == m, rows, float(n)), axis=0, keepdims=True)
        hit = rows == am
        vals.append(m)
        if payload is None:
            picks.append(am)
        else:
            picks.append(jnp.max(jnp.where(hit, payload, -1.0), axis=0, keepdims=True))
        s = jnp.where(hit, -jnp.inf, s)
    return jnp.concatenate(vals, axis=0), jnp.concatenate(picks, axis=0)


def _route_kernel(x1_ref, mod_ref, n2_ref, wq_ref, keys_ref, h2_ref, idx_ref, gate_ref, h2b_scr):
    d = x1_ref.shape[-1]
    nt = (((1,), (1,)), ((), ()))

    @pl.when(pl.program_id(1) == 0)
    def _():
        mod = mod_ref[...]
        h2 = _rmsnorm(x1_ref[...], n2_ref[...]) * (1.0 + mod[:, 4 * d:5 * d]) + mod[:, 3 * d:4 * d]
        h2_ref[...] = h2
        h2b_scr[...] = h2.astype(BF16)

    qp = jnp.dot(h2b_scr[...], wq_ref[...], preferred_element_type=F32)
    half = qp.shape[1] // 2
    tops = []
    for p in range(2):
        qh = qp[:, p * half:(p + 1) * half].astype(BF16)
        st = lax.dot_general(keys_ref[p], qh, nt, preferred_element_type=F32)
        tops.append(_topk_rows(st, PEER_TOPK))
    (a, ia), (b, ib) = tops
    width = [PEER_TOPK // (i + 1) for i in range(PEER_TOPK)]
    pad = -sum(width) % 8
    tm = a.shape[1]
    cand = jnp.concatenate([a[i:i + 1, :] + b[:width[i], :] for i in range(PEER_TOPK)]
                           + [jnp.full((pad, tm), -jnp.inf, F32)], axis=0)
    cidx = jnp.concatenate([ia[i:i + 1, :] * PEER_N_KEYS + ib[:width[i], :] for i in range(PEER_TOPK)]
                           + [jnp.zeros((pad, tm), F32)], axis=0)
    best, idx = _topk_rows(cand, PEER_TOPK, payload=cidx)
    e = jnp.exp(best - best[0:1, :])
    gate_ref[...] = e / jnp.sum(e, axis=0, keepdims=True)
    idx_ref[...] = idx.astype(jnp.int32)


def _route(x1, mod3, n2, wq, keys, seq, tm, b0):
    t, d = x1.shape
    hw = wq.shape[1] // PEER_HEADS
    per_seq = seq // tm
    return pl.pallas_call(
        _route_kernel,
        out_shape=(jax.ShapeDtypeStruct((t, d), F32),
                   jax.ShapeDtypeStruct((PEER_HEADS, PEER_TOPK, t), jnp.int32),
                   jax.ShapeDtypeStruct((PEER_HEADS, PEER_TOPK, t), F32)),
        grid=(t // tm, PEER_HEADS),
        in_specs=[pl.BlockSpec((tm, d), lambda i, h: (i, 0)),
                  pl.BlockSpec((None, 1, N_MOD * d), lambda i, h: (b0 + i // per_seq, 0, 0)),
                  pl.BlockSpec((1, d), lambda i, h: (0, 0)),
                  pl.BlockSpec((d, hw), lambda i, h: (0, h)),
                  pl.BlockSpec((None, 2, PEER_N_KEYS, hw // 2), lambda i, h: (h, 0, 0, 0))],
        out_specs=(pl.BlockSpec((tm, d), lambda i, h: (i, 0)),
                   pl.BlockSpec((None, PEER_TOPK, tm), lambda i, h: (h, 0, i)),
                   pl.BlockSpec((None, PEER_TOPK, tm), lambda i, h: (h, 0, i))),
        scratch_shapes=[pltpu.VMEM((tm, d), BF16)],
        compiler_params=_cparams("parallel", "arbitrary"),
        name="route",
    )(x1, mod3, n2, wq, keys)


SC_CORES = 2
SC_SUBCORES = 16
SC_LANES = 16
SC_WORKERS = SC_CORES * SC_SUBCORES
SC_ROWS_U = 64
SC_ROWS_V = 64
SC_ROWS_PER_ITER = 8
SC_TOKENS = 16


def _sc_mesh():
    return plsc.VectorSubcoreMesh(core_axis_name="c", subcore_axis_name="s")


def _sc_params():
    return dataclasses.replace(pltpu.CompilerParams(), needs_layout_passes=False)


def _sc_worker_base(tokens_per_worker):
    return (lax.axis_index("s") * SC_CORES + lax.axis_index("c")) * tokens_per_worker


def _sc_chunk_pipeline(tab_hbm, idx_v, bufs, n_chunks, compute):
    rows = bufs[0][0].shape[0]

    def gather(g, b):
        buf, sem = bufs[b]
        return pltpu.make_async_copy(tab_hbm.at[idx_v.at[pl.ds(g * rows, rows)]], buf, sem)

    gather(0, 0).start()

    @pl.loop(0, n_chunks, step=2)
    def _(g):
        gather(g, 0).wait()
        gather(g + 1, 1).start()
        compute(g, bufs[0][0])
        gather(g + 1, 1).wait()

        @pl.when(g + 2 < n_chunks)
        def _():
            gather(g + 2, 0).start()

        compute(g + 1, bufs[1][0])


def _pack_table(tab):
    half = tab.shape[1] // 2
    lo = lax.bitcast_convert_type(tab[:, :half].astype(BF16), jnp.uint16).astype(jnp.int32)
    bits = lax.bitcast_convert_type(tab[:, half:], jnp.int32)
    sign = bits & jnp.int32(-2 ** 31)
    magnitude = bits & jnp.int32(2 ** 31 - 1)
    top = jnp.maximum((magnitude - lo + 0x8000) >> 16, 0)
    return sign | ((top << 16) + lo)


def _unpack_lo(x):
    return lax.bitcast_convert_type(x << 16, F32)


def _unpack_hi(x):
    return lax.bitcast_convert_type(x, F32)


def _peer_u_sc(u_pk, idx_flat, h2, t, k):
    d = h2.shape[1]
    words = u_pk.shape[1]
    tpw = t // SC_WORKERS
    rows = SC_ROWS_U
    cpt = k // rows
    cpb = SC_TOKENS * cpt
    nj = words // SC_LANES

    def body(u_hbm, idx_hbm, h_hbm, o_hbm, idx_v, h_v, buf0, buf1, acc_v, pre_v, sem0, sem1):
        base = _sc_worker_base(tpw)
        lanes = lax.iota(jnp.int32, SC_LANES)

        def compute(g, buf):
            tl = g // cpt
            c = g % cpt

            @plsc.parallel_loop(0, rows // SC_ROWS_PER_ITER)
            def _(it):
                r0 = it * SC_ROWS_PER_ITER
                accs = [[None, None] for _ in range(SC_ROWS_PER_ITER)]
                for j in range(nj):
                    h_lo = h_v[tl, pl.ds(j * SC_LANES, SC_LANES)]
                    h_hi = h_v[tl, pl.ds(words + j * SC_LANES, SC_LANES)]
                    for a in range(SC_ROWS_PER_ITER):
                        x = buf[r0 + a, pl.ds(j * SC_LANES, SC_LANES)]
                        term = _unpack_lo(x) * h_lo + _unpack_hi(x) * h_hi
                        accs[a][j % 2] = term if accs[a][j % 2] is None else accs[a][j % 2] + term
                for a in range(SC_ROWS_PER_ITER):
                    acc_v[r0 + a, :] = accs[a][0] + accs[a][1]

            for q in range(rows // SC_LANES):
                s = plsc.load_gather(acc_v, [lanes + q * SC_LANES, jnp.zeros((SC_LANES,), jnp.int32)])
                for l in range(1, SC_LANES):
                    s = s + plsc.load_gather(acc_v, [lanes + q * SC_LANES,
                                                     jnp.full((SC_LANES,), l, jnp.int32)])
                pre_v[tl, pl.ds(c * rows + q * SC_LANES, SC_LANES)] = s

        @pl.loop(0, tpw // SC_TOKENS)
        def _(blk):
            tok0 = base + blk * SC_TOKENS
            pltpu.sync_copy(idx_hbm.at[pl.ds(tok0 * k, SC_TOKENS * k)], idx_v)
            pltpu.sync_copy(h_hbm.at[pl.ds(tok0, SC_TOKENS)], h_v)
            _sc_chunk_pipeline(u_hbm, idx_v, ((buf0, sem0), (buf1, sem1)), cpb, compute)
            pltpu.sync_copy(pre_v, o_hbm.at[pl.ds(tok0, SC_TOKENS)])

    return pl.kernel(
        body,
        out_type=jax.ShapeDtypeStruct((t, k), F32),
        mesh=_sc_mesh(),
        scratch_types=[pltpu.VMEM((SC_TOKENS * k,), jnp.int32),
                       pltpu.VMEM((SC_TOKENS, d), F32),
                       pltpu.VMEM((rows, words), jnp.int32),
                       pltpu.VMEM((rows, words), jnp.int32),
                       pltpu.VMEM((rows, SC_LANES), F32),
                       pltpu.VMEM((SC_TOKENS, k), F32),
                       pltpu.SemaphoreType.DMA,
                       pltpu.SemaphoreType.DMA],
        compiler_params=_sc_params(),
        name="peer_u_sc",
    )(u_pk, idx_flat, h2)


def _peer_v_sc(v_pk, idx_flat, w_flat, t, k, after):
    words = v_pk.shape[1]
    d = 2 * words
    tpw = t // SC_WORKERS
    rows = SC_ROWS_V
    cpt = k // rows
    cpb = SC_TOKENS * cpt
    pw = words // 2
    nj = pw // SC_LANES

    def body(v_hbm, idx_hbm, w_hbm, after_hbm, o_hbm, idx_v, w_v, buf0, buf1, out_v, sem0, sem1):
        del after_hbm
        base = _sc_worker_base(tpw)

        def compute(g, buf):
            tl = g // cpt
            c = g % cpt
            for p in range(2):
                lo_cols = [pl.ds(p * pw + j * SC_LANES, SC_LANES) for j in range(nj)]
                hi_cols = [pl.ds(words + p * pw + j * SC_LANES, SC_LANES) for j in range(nj)]

                def row(r, accs):
                    wb = plsc.load_gather(w_v, [jnp.full((SC_LANES,), g * rows + r, jnp.int32)])
                    new = []
                    for j in range(nj):
                        x = buf[r, lo_cols[j]]
                        new.append(accs[2 * j] + _unpack_lo(x) * wb)
                        new.append(accs[2 * j + 1] + _unpack_hi(x) * wb)
                    return tuple(new)

                init = []
                for j in range(nj):
                    init.append(jnp.where(c == 0, 0.0, out_v[tl, lo_cols[j]]))
                    init.append(jnp.where(c == 0, 0.0, out_v[tl, hi_cols[j]]))
                accs = lax.fori_loop(0, rows, row, tuple(init))
                for j in range(nj):
                    out_v[tl, lo_cols[j]] = accs[2 * j]
                    out_v[tl, hi_cols[j]] = accs[2 * j + 1]

        @pl.loop(0, tpw // SC_TOKENS)
        def _(blk):
            tok0 = base + blk * SC_TOKENS
            pltpu.sync_copy(idx_hbm.at[pl.ds(tok0 * k, SC_TOKENS * k)], idx_v)
            pltpu.sync_copy(w_hbm.at[pl.ds(tok0 * k, SC_TOKENS * k)], w_v)
            _sc_chunk_pipeline(v_hbm, idx_v, ((buf0, sem0), (buf1, sem1)), cpb, compute)
            pltpu.sync_copy(out_v, o_hbm.at[pl.ds(tok0, SC_TOKENS)])

    return pl.kernel(
        body,
        out_type=jax.ShapeDtypeStruct((t, d), F32),
        mesh=_sc_mesh(),
        scratch_types=[pltpu.VMEM((SC_TOKENS * k,), jnp.int32),
                       pltpu.VMEM((SC_TOKENS * k,), F32),
                       pltpu.VMEM((rows, words), jnp.int32),
                       pltpu.VMEM((rows, words), jnp.int32),
                       pltpu.VMEM((SC_TOKENS, d), F32),
                       pltpu.SemaphoreType.DMA,
                       pltpu.SemaphoreType.DMA],
        compiler_params=_sc_params(),
        name="peer_v_sc",
    )(v_pk, idx_flat, w_flat, after)


def _act_kernel(gate_ref, pre_ref, after_ref, o_ref):
    del after_ref
    o_ref[...] = gate_ref[...] * _gelu_exact(pre_ref[...])


def _act(gate, pre, tile, after):
    t, k = pre.shape
    spec = pl.BlockSpec((tile, k), lambda i: (i, 0))
    return pl.pallas_call(
        _act_kernel,
        out_shape=jax.ShapeDtypeStruct((t, k), F32),
        grid=(t // tile,),
        in_specs=[spec, spec, pl.BlockSpec(memory_space=pl.ANY)],
        out_specs=spec,
        compiler_params=_cparams("parallel"),
        name="expert_act",
    )(gate, pre, after)


def _final_kernel(x1_ref, y_ref, mod_ref, fg_ref, out_so_far_ref, o_ref):
    del out_so_far_ref
    d = x1_ref.shape[-1]
    x2 = x1_ref[...] + mod_ref[:, 5 * d:6 * d] * y_ref[...]
    o_ref[...] = _rmsnorm(x2, fg_ref[...])


def _final(x1, y, mod3, fg, out_so_far, seq, tile, b0):
    t, d = y.shape
    per_seq = seq // tile
    row = pl.BlockSpec((tile, d), lambda i: (i, 0))
    return pl.pallas_call(
        _final_kernel,
        out_shape=jax.ShapeDtypeStruct(out_so_far.shape, F32),
        grid=(t // tile,),
        in_specs=[row, row,
                  pl.BlockSpec((None, 1, N_MOD * d), lambda i: (b0 + i // per_seq, 0, 0)),
                  pl.BlockSpec((1, d), lambda i: (0, 0)),
                  pl.BlockSpec(memory_space=pl.ANY)],
        out_specs=pl.BlockSpec((tile, d), lambda i: (b0 * per_seq + i, 0)),
        input_output_aliases={4: 0},
        compiler_params=_cparams("parallel"),
        name="final",
    )(x1, y, mod3, fg, out_so_far)


def _peer_sc_u(idx, h2, u_pk, t):
    k = idx.shape[1]
    assert t % (SC_WORKERS * SC_TOKENS) == 0 and k % (2 * SC_ROWS_U) == 0 and k % (2 * SC_ROWS_V) == 0
    return _peer_u_sc(u_pk, idx.reshape(idx.shape[0] * k), h2, t, k)


def _peer_sc_v(idx, gate, pre, v_pk, tile, after, tc_done):
    t, k = pre.shape
    w = _act(gate, pre, tile, tc_done)
    return _peer_v_sc(v_pk, idx.reshape(idx.shape[0] * k), w.reshape(t * k), t, k, after)


TC_PEER_TOKENS = 8


def _peer_tc_kernel(idx_ref, idxn_ref, gate_ref, h2_ref, x1_ref, mod_ref, fg_ref, u_hbm, v_hbm,
                    out_so_far_ref, o_ref, done_ref, ubuf, vbuf, sem):
    del out_so_far_ref
    i = pl.program_id(0)
    done_ref[...] = jnp.zeros(done_ref.shape, F32)
    tt, k = gate_ref.shape
    d = x1_ref.shape[-1]
    words = d // 2
    slot = i % 2

    def row_copy(tab, buf, which, sl, src_row, dst_row):
        return pltpu.make_async_copy(tab.at[pl.ds(src_row, 1)], buf.at[sl, pl.ds(dst_row, 1)],
                                     sem.at[which, sl])

    def issue_token(iref, sl, t):
        for j in range(k):
            row = iref[t, j]
            row_copy(u_hbm, ubuf, 0, sl, row, t * k + j).start(priority=j % 2)
            row_copy(v_hbm, vbuf, 1, sl, row, t * k + j).start(priority=(j + 1) % 2)

    @pl.when(i == 0)
    def _():
        def tok(t, carry):
            issue_token(idx_ref, 0, t)
            return carry
        lax.fori_loop(0, tt, tok, 0)

    @pl.when(i + 1 < pl.num_programs(0))
    def _():
        for t in range(tt):
            issue_token(idxn_ref, 1 - slot, t)

    pltpu.make_async_copy(u_hbm.at[pl.ds(0, tt * k)], ubuf.at[slot], sem.at[0, slot]).wait()
    pltpu.make_async_copy(v_hbm.at[pl.ds(0, tt * k)], vbuf.at[slot], sem.at[1, slot]).wait()

    def unpack(x):
        return _unpack_lo(x), _unpack_hi(x)

    cols = []
    for t in range(tt):
        lo, hi = unpack(ubuf[slot, t * k:(t + 1) * k, :])
        cols.append(jnp.sum(lo * h2_ref[t:t + 1, :words] + hi * h2_ref[t:t + 1, words:],
                            axis=1, keepdims=True))
    pre = jnp.concatenate(cols, axis=1)
    w = gate_ref[...].T * _gelu_exact(pre)
    outs = []
    for t in range(tt):
        lo, hi = unpack(vbuf[slot, t * k:(t + 1) * k, :])
        wt = w[:, t:t + 1]
        outs.append(jnp.concatenate([jnp.sum(wt * lo, axis=0, keepdims=True),
                                     jnp.sum(wt * hi, axis=0, keepdims=True)], axis=1))
    y = jnp.concatenate(outs, axis=0)
    x2 = x1_ref[...] + mod_ref[:, 5 * d:6 * d] * y
    o_ref[...] = _rmsnorm(x2, fg_ref[...])


def _peer_tc(idx, gate, h2, x1, mod3, fg, u_pk, v_pk, out_so_far, seq, b0, t0):
    tg, k = idx.shape
    d = x1.shape[1]
    tt = TC_PEER_TOKENS
    n = (tg - t0) // tt
    first = t0 // tt
    per_seq = seq // tt
    row = lambda i: (first + i, 0)
    return pl.pallas_call(
        _peer_tc_kernel,
        out_shape=(jax.ShapeDtypeStruct(out_so_far.shape, F32), jax.ShapeDtypeStruct((8, LANES), F32)),
        grid=(n,),
        in_specs=[pl.BlockSpec((tt, k), row, memory_space=pltpu.SMEM),
                  pl.BlockSpec((tt, k), lambda i: (first + jnp.minimum(i + 1, n - 1), 0),
                               memory_space=pltpu.SMEM),
                  pl.BlockSpec((tt, k), row),
                  pl.BlockSpec((tt, d), row),
                  pl.BlockSpec((tt, d), row),
                  pl.BlockSpec((None, 1, N_MOD * d), lambda i: (b0 + (first + i) // per_seq, 0, 0)),
                  pl.BlockSpec((1, d), lambda i: (0, 0)),
                  pl.BlockSpec(memory_space=pl.ANY),
                  pl.BlockSpec(memory_space=pl.ANY),
                  pl.BlockSpec(memory_space=pl.ANY)],
        out_specs=(pl.BlockSpec((tt, d), lambda i: (b0 * per_seq + first + i, 0)),
                   pl.BlockSpec((8, LANES), lambda i: (0, 0))),
        scratch_shapes=[pltpu.VMEM((2, tt * k, d // 2), jnp.int32),
                        pltpu.VMEM((2, tt * k, d // 2), jnp.int32),
                        pltpu.SemaphoreType.DMA((2, 2))],
        input_output_aliases={9: 0},
        compiler_params=_cparams("arbitrary"),
        name="peer_tc",
    )(idx, idx, gate, h2, x1, mod3, fg, u_pk, v_pk, out_so_far)


def _rope_tables(length):
    rows = length // GRID_W
    row = jnp.repeat(jnp.arange(rows, dtype=F32), GRID_W)
    col = jnp.tile(jnp.arange(GRID_W, dtype=F32), rows)
    inv_freq = ROPE_BASE ** (-jnp.arange(ROPE_PAIRS, dtype=F32) / ROPE_PAIRS)
    ang_r = row[:, None] * inv_freq
    ang_c = col[:, None] * inv_freq
    cos = jnp.concatenate([jnp.cos(ang_r)] * 2 + [jnp.cos(ang_c)] * 2, axis=-1)
    sin = jnp.concatenate([-jnp.sin(ang_r), jnp.sin(ang_r), -jnp.sin(ang_c), jnp.sin(ang_c)], axis=-1)
    reps = LANES // HEAD_DIM
    return jnp.tile(cos, (1, reps)), jnp.tile(sin, (1, reps))


def _layer(x, c, ctx, c_ctx, w_mod, b_mod, n1, n2, w_in, sink, conv_w, w_attn_out, w_conv_out,
           w_mix_out, pw_q, p_keys, p_u, p_v, final_g, tm, tq, tr, tt, groups):
    batch, seq, d = x.shape
    t = batch * seq
    aw = N_Q_HEADS * HEAD_DIM
    kw = N_KV_HEADS * HEAD_DIM

    rows = -(-(batch + 1) // 8) * 8
    cond = jnp.zeros((rows, d), F32).at[:batch].set(c).at[batch].set(c_ctx)
    mod3 = _adaln(cond, w_mod, b_mod).reshape(rows, 1, N_MOD * d)

    w_in_b = w_in.astype(BF16)
    kvc = _ctx_kv(ctx, mod3, batch, n1.reshape(1, d), w_in_b[:, aw:aw + 2 * kw])

    cos, sin = _rope_tables(seq)
    x2 = x.reshape(t, d)
    wa, wc, wm = w_attn_out.astype(BF16), w_conv_out.astype(BF16), w_mix_out.astype(BF16)
    wq, keys = pw_q.astype(BF16), p_keys.astype(BF16)
    u_pk, v_pk = _pack_table(p_u), _pack_table(p_v)
    nsel = PEER_HEADS * PEER_TOPK

    fg = final_g.reshape(1, d)
    out = pl.empty((t, d), F32)
    b0 = 0
    pending = None
    finals = []

    def launch_v(p, after):
        idx, gate, pre, x1, pb0, tc_done = p
        finals.append((x1, _peer_sc_v(idx, gate, pre, v_pk, tt, after, tc_done), pb0))

    for g, nb in enumerate(groups):
        tg = nb * seq
        t_sc = _sparsecore_share(tg, g)
        q, kv, gb, cu, ga, gv = _proj(x2, mod3, n1.reshape(1, d), w_in_b, cos, sin, seq, tm, b0, nb)
        x1 = _mixer(x2, q, kv, kvc, cu, gb, ga, gv, sink.reshape(1, N_Q_HEADS), conv_w,
                    wa, wc, wm, mod3, seq, tq, b0, nb)
        h2, idx_t, gate_t = _route(x1, mod3, n2.reshape(1, d), wq, keys, seq, tr, b0)
        idx, gate = idx_t.reshape(nsel, tg).T, gate_t.reshape(nsel, tg).T
        pre = _peer_sc_u(idx, h2, u_pk, t_sc)
        if pending is not None:
            launch_v(pending, pre)
        tc_done = pre
        if t_sc < tg:
            out, tc_done = _peer_tc(idx, gate, h2, x1, mod3, fg, u_pk, v_pk, out, seq, b0, t_sc)
        pending = (idx, gate, pre, x1, b0, tc_done)
        b0 += nb
    launch_v(pending, pending[2])
    for x1, y, pb0 in finals:
        out = _final(x1, y, mod3, fg, out, seq, tt, pb0)
    return out.reshape(batch, seq, d)


MAX_TOKEN_GROUPS = 8
SC_SHARE_NUMS, SC_SHARE_DEN = (14, 12, 14, 14), 16


def _sparsecore_share(tokens, group):
    unit = SC_WORKERS * SC_TOKENS
    t_sc = tokens * SC_SHARE_NUMS[group % len(SC_SHARE_NUMS)] // SC_SHARE_DEN // unit * unit
    if t_sc == 0 or (tokens - t_sc) % TC_PEER_TOKENS:
        return tokens
    return t_sc


def _token_groups(batch, seq):
    unit = SC_WORKERS * SC_TOKENS
    for groups in range(min(MAX_TOKEN_GROUPS, batch), 0, -1):
        nb = batch // groups
        if batch % groups == 0 and (nb * seq) % unit == 0:
            return [nb] * groups
    raise ValueError("token count must be a multiple of the SparseCore work split")


def kernel(x, c, ctx, c_ctx, w_mod, b_mod, norm1_g, norm2_g, w_in, attn_sink, conv_w, w_attn_out,
           w_conv_out, w_mix_out, peer_w_q, peer_sub_keys, peer_u, peer_v, final_g):
    assert w_mod.shape[0] == 1, "only the single-layer configuration is implemented"
    seq = x.shape[1]
    return _layer(x, c, ctx, c_ctx, w_mod[0], b_mod[0], norm1_g[0], norm2_g[0], w_in[0],
                  attn_sink[0], conv_w[0], w_attn_out[0], w_conv_out[0], w_mix_out[0],
                  peer_w_q[0], peer_sub_keys[0], peer_u[0], peer_v[0], final_g,
                  tm=min(512, seq), tq=min(256, seq), tr=min(256, seq), tt=min(256, seq),
                  groups=_token_groups(x.shape[0], seq))
```
